```python
import math
import jax, jax.numpy as jnp
from jax import lax
import numpy as np

D_MODEL = 1024
BATCH = 4
SEQ = 8192
DEPTH = 4

GRID_W = 64
CTX_LEN = 256
RMS_EPS = 1e-6
ROPE_THETA = 10000.0
Q_BLOCK = 128

MLA_HEADS = 8
MLA_NOPE = 64
MLA_ROPE = 32
MLA_V = 64
MLA_Q_RANK = 256
MLA_KV_RANK = 128
MLA_WIDTH = MLA_HEADS * MLA_V
MLA_SCALE = (MLA_NOPE + MLA_ROPE) ** -0.5

LRU_WIDTH = 512
LRU_BLOCKS = 8
LRU_BLOCK = LRU_WIDTH // LRU_BLOCKS
LRU_CONV = 4
LRU_C = 8.0

NA_HEADS = 16
NA_HEAD_DIM = 64
NA_WIDTH = NA_HEADS * NA_HEAD_DIM
NA_WIN_R = 8
NA_WIN_C = 16

N_EVEN = (DEPTH + 1) // 2
N_ODD = DEPTH // 2
EVEN_SIZES = (MLA_Q_RANK, MLA_KV_RANK, MLA_ROPE, MLA_WIDTH, LRU_WIDTH, LRU_WIDTH)
EVEN_IN = sum(EVEN_SIZES)
EVEN_SPLITS = tuple(int(v) for v in np.cumsum(EVEN_SIZES)[:-1])
EVEN_MIX = MLA_WIDTH + LRU_WIDTH
ODD_IN = 4 * NA_WIDTH

kernel_name = 'hybrid_mla_rglru_natten_prefix_dit'


def _rmsnorm(x, g):
    xf = x.astype(jnp.float32)
    y = xf * lax.rsqrt(jnp.mean(xf * xf, axis=-1, keepdims=True) + RMS_EPS)
    return (y * g.astype(jnp.float32)).astype(x.dtype)


def _adaln(cond, w, b):
    m = jax.nn.silu(cond) @ w + b
    return jnp.split(m, 3, axis=-1)


def _rope_2d_tables(n_tok):
    n_freq = MLA_ROPE // 4
    inv = ROPE_THETA ** (-jnp.arange(n_freq, dtype=jnp.float32) / n_freq)
    t = jnp.arange(n_tok, dtype=jnp.int32)
    ang_r = (t // GRID_W).astype(jnp.float32)[:, None] * inv
    ang_c = (t % GRID_W).astype(jnp.float32)[:, None] * inv
    return (jnp.cos(ang_r), jnp.sin(ang_r), jnp.cos(ang_c), jnp.sin(ang_c))


def _rotate(x, cos, sin):
    x1, x2 = jnp.split(x, 2, axis=-1)
    return jnp.concatenate([x1 * cos - x2 * sin, x2 * cos + x1 * sin], axis=-1)


def _rope_2d(x, tabs):
    cr, sr, cc, sc = tabs
    xr, xc = jnp.split(x.astype(jnp.float32), 2, axis=-1)
    return jnp.concatenate([_rotate(xr, cr, sr), _rotate(xc, cc, sc)], axis=-1).astype(x.dtype)


def _mla_attend(qn, qr, kn, kr, v):
    s = jnp.einsum('bhqd,bhkd->bhqk', qn, kn) + jnp.einsum('bhqd,bkd->bhqk', qr, kr)
    p = jax.nn.softmax(s.astype(jnp.float32) * MLA_SCALE, axis=-1)
    return jnp.einsum('bhqk,bhkd->bhqd', p.astype(v.dtype), v)


def _short_conv(x, w, b):
    k = w.shape[0]
    y = lax.conv_general_dilated(x, w[:, None, :], window_strides=(1,),
                                 padding=[(k // 2, k - 1 - k // 2)],
                                 dimension_numbers=('NWC', 'WIO', 'NWC'),
                                 feature_group_count=x.shape[-1])
    return y + b


def _rglru_coeffs(x, wa, ba, wx, bx, lam):
    b_, t_, w_ = x.shape
    xb = x.reshape(b_, t_, LRU_BLOCKS, LRU_BLOCK)
    r = jax.nn.sigmoid(jnp.einsum('btnk,nkj->btnj', xb, wa).reshape(b_, t_, w_) + ba)
    i = jax.nn.sigmoid(jnp.einsum('btnk,nkj->btnj', xb, wx).reshape(b_, t_, w_) + bx)
    log_a = -LRU_C * r.astype(jnp.float32) * jax.nn.softplus(-lam.astype(jnp.float32))
    a = jnp.exp(log_a)
    u = jnp.sqrt(-jnp.expm1(2.0 * log_a)) * (i * x).astype(jnp.float32)
    return a, u


def _linear_scan(a, u, h0):
    u = u.at[:, 0].add(a[:, 0] * h0)
    def comb(left, right):
        al, ul = left
        ar, ur = right
        return al * ar, ar * ul + ur
    _, h = lax.associative_scan(comb, (a, u), axis=1)
    return h


def _bidir_rglru(x_c, x_l, wa, ba, wx, bx, lam):
    outs_c, outs_l = [], []
    for d in range(2):
        a_c, u_c = _rglru_coeffs(x_c, wa[d], ba[d], wx[d], bx[d], lam[d])
        a_l, u_l = _rglru_coeffs(x_l, wa[d], ba[d], wx[d], bx[d], lam[d])
        if d == 1:
            a_c, u_c, a_l, u_l = (jnp.flip(t, 1) for t in (a_c, u_c, a_l, u_l))
        h_c = _linear_scan(a_c, u_c, jnp.zeros((x_c.shape[0], x_c.shape[2]), jnp.float32))
        h_l = _linear_scan(a_l, u_l, h_c[:, -1])
        if d == 1:
            h_c, h_l = jnp.flip(h_c, 1), jnp.flip(h_l, 1)
        outs_c.append(h_c)
        outs_l.append(h_l)
    return (outs_c[0] + outs_c[1]).astype(x_c.dtype), (outs_l[0] + outs_l[1]).astype(x_l.dtype)


def _even_mixer(h_c, h_l, w_in, q_norm, w_uq, kv_norm, w_ukv, conv_w, conv_b,
                wa, ba, wx, bx, lam, w_out, tabs, need_ctx):
    b_, c_, _ = h_c.shape
    s_ = h_l.shape[1]
    t_ = c_ + s_
    z = jnp.concatenate([h_c, h_l], axis=1) @ w_in
    cq, ckv, kr, g_mla, x_lru, g_lru = jnp.split(z, EVEN_SPLITS, axis=-1)
    q = (_rmsnorm(cq, q_norm) @ w_uq).reshape(b_, t_, MLA_HEADS, MLA_NOPE + MLA_ROPE).transpose(0, 2, 1, 3)
    kv = (_rmsnorm(ckv, kv_norm) @ w_ukv).reshape(b_, t_, MLA_HEADS, MLA_NOPE + MLA_V).transpose(0, 2, 1, 3)
    q_nope, q_rope = q[..., :MLA_NOPE], q[..., MLA_NOPE:]
    k_nope, v = kv[..., :MLA_NOPE], kv[..., MLA_NOPE:]
    q_rope_l = _rope_2d(q_rope[:, :, c_:], tabs)
    k_rope = jnp.concatenate([kr[:, :c_], _rope_2d(kr[:, c_:], tabs)], axis=1)
    nb = s_ // Q_BLOCK
    def to_blocks(t):
        return jnp.moveaxis(t.reshape(b_, MLA_HEADS, nb, Q_BLOCK, t.shape[-1]), 2, 0)
    o_l = lax.map(lambda qs: _mla_attend(qs[0], qs[1], k_nope, k_rope, v),
                  (to_blocks(q_nope[:, :, c_:]), to_blocks(q_rope_l)))
    o_l = jnp.moveaxis(o_l, 0, 2).reshape(b_, MLA_HEADS, s_, MLA_V).transpose(0, 2, 1, 3).reshape(b_, s_, MLA_WIDTH)
    u_c = _short_conv(x_lru[:, :c_], conv_w, conv_b)
    u_l = _short_conv(x_lru[:, c_:], conv_w, conv_b)
    r_c, r_l = _bidir_rglru(u_c, u_l, wa, ba, wx, bx, lam)
    y_l = jnp.concatenate([o_l * jax.nn.silu(g_mla[:, c_:]), r_l * jax.nn.silu(g_lru[:, c_:])], axis=-1) @ w_out
    y_c = None
    if need_ctx:
        o_c = _mla_attend(q_nope[:, :, :c_], q_rope[:, :, :c_], k_nope[:, :, :c_], k_rope[:, :c_], v[:, :, :c_])
        o_c = o_c.transpose(0, 2, 1, 3).reshape(b_, c_, MLA_WIDTH)
        y_c = jnp.concatenate([o_c * jax.nn.silu(g_mla[:, :c_]), r_c * jax.nn.silu(g_lru[:, :c_])], axis=-1) @ w_out
    return y_c, y_l


def _odd_mixer(h_c, h_l, w_in, rpb, w_out, need_ctx):
    b_, c_, _ = h_c.shape
    s_ = h_l.shape[1]
    rows = s_ // GRID_W
    wr = min(NA_WIN_R, rows)
    z = jnp.concatenate([h_c, h_l], axis=1) @ w_in
    def heads(t):
        return t.reshape(b_, t.shape[1], NA_HEADS, NA_HEAD_DIM).transpose(0, 2, 1, 3)
    q, k, v, g = jnp.split(z, 4, axis=-1)
    q, k, v = heads(q) * (NA_HEAD_DIM ** -0.5), heads(k), heads(v)
    q_c, k_c, v_c = q[:, :, :c_], k[:, :, :c_], v[:, :, :c_]
    q_grid = q[:, :, c_:].reshape(b_, NA_HEADS, rows, GRID_W, NA_HEAD_DIM)
    k_grid = k[:, :, c_:].reshape(b_, NA_HEADS, rows, GRID_W, NA_HEAD_DIM)
    v_grid = v[:, :, c_:].reshape(b_, NA_HEADS, rows, GRID_W, NA_HEAD_DIM)
    cols = jnp.arange(GRID_W)
    col_start = jnp.clip(cols - NA_WIN_C // 2, 0, GRID_W - NA_WIN_C)
    col_idx = col_start[:, None] + jnp.arange(NA_WIN_C)
    dc_idx = col_idx - cols[:, None] + (NA_WIN_C - 1)
    n_loc = wr * NA_WIN_C
    def row_block(args):
        r, q_row = args
        rs = jnp.clip(r - wr // 2, 0, rows - wr)
        k_rows = lax.dynamic_slice_in_dim(k_grid, rs, wr, axis=2)
        v_rows = lax.dynamic_slice_in_dim(v_grid, rs, wr, axis=2)
        k_win = k_rows[:, :, :, col_idx]
        v_win = v_rows[:, :, :, col_idx]
        dr_idx = rs + jnp.arange(wr) - r + (NA_WIN_R - 1)
        bias = rpb[:, dr_idx[None, :, None], dc_idx[:, None, :]]
        s_loc = jnp.einsum('bhqd,bhrqcd->bhqrc', q_row, k_win) + bias
        s_ctx = jnp.einsum('bhqd,bhkd->bhqk', q_row, k_c)
        s = jnp.concatenate([s_loc.reshape(b_, NA_HEADS, GRID_W, n_loc), s_ctx], axis=-1)
        p = jax.nn.softmax(s.astype(jnp.float32), axis=-1).astype(v.dtype)
        p_loc = p[..., :n_loc].reshape(b_, NA_HEADS, GRID_W, wr, NA_WIN_C)
        return (jnp.einsum('bhqrc,bhrqcd->bhqd', p_loc, v_win)
                + jnp.einsum('bhqk,bhkd->bhqd', p[..., n_loc:], v_c))
    o_l = lax.map(row_block, (jnp.arange(rows), jnp.moveaxis(q_grid, 2, 0)))
    o_l = o_l.transpose(1, 0, 3, 2, 4).reshape(b_, s_, NA_WIDTH)
    y_l = (o_l * jax.nn.silu(g[:, c_:])) @ w_out
    y_c = None
    if need_ctx:
        p_c = jax.nn.softmax(jnp.einsum('bhqd,bhkd->bhqk', q_c, k_c).astype(jnp.float32), axis=-1)
        o_c = jnp.einsum('bhqk,bhkd->bhqd', p_c.astype(v.dtype), v_c).transpose(0, 2, 1, 3).reshape(b_, c_, NA_WIDTH)
        y_c = (o_c * jax.nn.silu(g[:, :c_])) @ w_out
    return y_c, y_l


def setup_inputs(seed: int = 0) -> dict:
    key = jax.random.key(seed)
    ks = iter(jax.random.split(key, 32))
    def nrm(shape, s):
        return jax.random.normal(next(ks), shape, jnp.float32) * s
    u = jax.random.uniform(next(ks), (N_EVEN, 2, LRU_WIDTH), jnp.float32, 0.9, 0.999)
    log_a = jnp.log(u) / LRU_C
    lru_lambda = log_a - jnp.log(-jnp.expm1(log_a))
    return {
        'x': nrm((BATCH, SEQ, D_MODEL), 1.0),
        'c': nrm((BATCH, D_MODEL), 1.0),
        'ctx': nrm((BATCH, CTX_LEN, D_MODEL), 1.0),
        'c_ctx': nrm((D_MODEL,), 1.0),
        'ada_w': nrm((DEPTH, D_MODEL, 3 * D_MODEL), 0.5 * D_MODEL ** -0.5),
        'ada_b': nrm((DEPTH, 3 * D_MODEL), 0.01),
        'norm_g': 1.0 + nrm((DEPTH, D_MODEL), 0.01),
        'ev_w_in': nrm((N_EVEN, D_MODEL, EVEN_IN), D_MODEL ** -0.5),
        'mla_q_norm': 1.0 + nrm((N_EVEN, MLA_Q_RANK), 0.01),
        'mla_w_uq': nrm((N_EVEN, MLA_Q_RANK, MLA_HEADS * (MLA_NOPE + MLA_ROPE)), MLA_Q_RANK ** -0.5),
        'mla_kv_norm': 1.0 + nrm((N_EVEN, MLA_KV_RANK), 0.01),
        'mla_w_ukv': nrm((N_EVEN, MLA_KV_RANK, MLA_HEADS * (MLA_NOPE + MLA_V)), MLA_KV_RANK ** -0.5),
        'lru_conv_w': nrm((N_EVEN, LRU_CONV, LRU_WIDTH), LRU_CONV ** -0.5),
        'lru_conv_b': nrm((N_EVEN, LRU_WIDTH), 0.01),
        'lru_wa': nrm((N_EVEN, 2, LRU_BLOCKS, LRU_BLOCK, LRU_BLOCK), LRU_BLOCK ** -0.5),
        'lru_ba': nrm((N_EVEN, 2, LRU_WIDTH), 0.01),
        'lru_wx': nrm((N_EVEN, 2, LRU_BLOCKS, LRU_BLOCK, LRU_BLOCK), LRU_BLOCK ** -0.5),
        'lru_bx': nrm((N_EVEN, 2, LRU_WIDTH), 0.01),
        'lru_lambda': lru_lambda,
        'ev_w_out': nrm((N_EVEN, EVEN_MIX, D_MODEL), EVEN_MIX ** -0.5),
        'od_w_in': nrm((N_ODD, D_MODEL, ODD_IN), D_MODEL ** -0.5),
        'na_rpb': nrm((N_ODD, NA_HEADS, 2 * NA_WIN_R - 1, 2 * NA_WIN_C - 1), 0.1),
        'od_w_out': nrm((N_ODD, NA_WIDTH, D_MODEL), NA_WIDTH ** -0.5),
        'final_norm_g': 1.0 + nrm((D_MODEL,), 0.01),
    }


def reference(x, c, ctx, c_ctx, ada_w, ada_b, norm_g, ev_w_in, mla_q_norm, mla_w_uq,
              mla_kv_norm, mla_w_ukv, lru_conv_w, lru_conv_b, lru_wa, lru_ba, lru_wx, lru_bx,
              lru_lambda, ev_w_out, od_w_in, na_rpb, od_w_out, final_norm_g):
    tabs = _rope_2d_tables(x.shape[1])
    cx = ctx
    for layer in range(DEPTH):
        need_ctx = layer < DEPTH - 1
        shift, scale, gate = _adaln(c, ada_w[layer], ada_b[layer])
        shift_c, scale_c, gate_c = _adaln(c_ctx, ada_w[layer], ada_b[layer])
        h_l = _rmsnorm(x, norm_g[layer]) * (1.0 + scale[:, None]) + shift[:, None]
        h_c = _rmsnorm(cx, norm_g[layer]) * (1.0 + scale_c) + shift_c
        i = layer // 2
        if layer % 2 == 0:
            y_c, y_l = _even_mixer(h_c, h_l, ev_w_in[i], mla_q_norm[i], mla_w_uq[i], mla_kv_norm[i],
                                   mla_w_ukv[i], lru_conv_w[i], lru_conv_b[i], lru_wa[i], lru_ba[i],
                                   lru_wx[i], lru_bx[i], lru_lambda[i], ev_w_out[i], tabs, need_ctx)
        else:
            y_c, y_l = _odd_mixer(h_c, h_l, od_w_in[i], na_rpb[i], od_w_out[i], need_ctx)
        x = x + gate[:, None] * y_l
        if need_ctx:
            cx = cx + gate_c * y_c
    return _rmsnorm(x, final_norm_g)
```

```python
import functools
import math

import numpy as np
import jax
import jax.numpy as jnp
from jax import lax
from jax.experimental import pallas as pl
from jax.experimental.pallas import tpu as pltpu

F32 = jnp.float32
BF16 = jnp.bfloat16

GRID_W = 64
RMS_EPS = 1e-6
ROPE_THETA = 10000.0
MLA_HEADS = 8
MLA_NOPE = 64
MLA_ROPE = 32
MLA_V = 64
MLA_Q_RANK = 256
MLA_KV_RANK = 128
MLA_WIDTH = MLA_HEADS * MLA_V
MLA_SCALE = (MLA_NOPE + MLA_ROPE) ** -0.5
LRU_WIDTH = 512
LRU_BLOCKS = 8
LRU_BLOCK = LRU_WIDTH // LRU_BLOCKS
LRU_CONV = 4
LRU_C = 8.0
NA_HEADS = 16
NA_HEAD_DIM = 64
NA_WIDTH = NA_HEADS * NA_HEAD_DIM
NA_WIN_R = 8
NA_WIN_C = 16

LANES = 128
LOG2E = 1.4426950408889634
NEG_BIG = -1e30
VMEM_LIMIT = 56 * 1024 * 1024

ROW_TILE = 768
FINAL_ROW_TILE = 512
MLA_TQ = 512
MLA_TK = 768
NA_QROWS = 4
NA_KROWS = 12
LRU_CHUNK = 256


def _cparams(sem):
    return pltpu.CompilerParams(dimension_semantics=sem, vmem_limit_bytes=VMEM_LIMIT)


def _silu(v):
    return v * jax.nn.sigmoid(v)


def _adaln_kernel(cond_ref, w_ref, b_ref, o_ref):
    a = _silu(cond_ref[...])
    o_ref[0] = jnp.dot(a, w_ref[0], preferred_element_type=F32) + b_ref[0]


def _adaln(cond, ada_w, ada_b):
    depth, d, d3 = ada_w.shape
    tn = 1024
    return pl.pallas_call(
        _adaln_kernel,
        grid=(depth, d3 // tn),
        in_specs=[
            pl.BlockSpec((8, d), lambda l, n: (0, 0)),
            pl.BlockSpec((1, d, tn), lambda l, n: (l, 0, n)),
            pl.BlockSpec((1, 1, tn), lambda l, n: (l, 0, n)),
        ],
        out_specs=pl.BlockSpec((1, 8, tn), lambda l, n: (l, 0, n)),
        out_shape=jax.ShapeDtypeStruct((depth, 8, d3), F32),
        compiler_params=_cparams(("arbitrary", "arbitrary")),
        name="adaln",
    )(cond, ada_w, ada_b.reshape(depth, 1, d3))


def _modulated_norm(x, g, modb_ref, modc_ref, is_ctx, d):
    ms = jnp.mean(x * x, axis=-1, keepdims=True)
    y = x * lax.rsqrt(ms + RMS_EPS) * g
    shift = jnp.where(is_ctx, modc_ref[0, :, 0:d], modb_ref[0, :, 0:d])
    scale = jnp.where(is_ctx, modc_ref[0, :, d:2 * d], modb_ref[0, :, d:2 * d])
    return y * (1.0 + scale) + shift


def _row_is_ctx(tm, n_latent):
    row = pl.program_id(1) * tm + lax.broadcasted_iota(jnp.int32, (tm, 1), 0)
    return row >= n_latent


def _rms(v, g):
    ms = jnp.mean(v * v, axis=-1, keepdims=True)
    return v * lax.rsqrt(ms + RMS_EPS) * g


def _even_pre_kernel(x_ref, modb_ref, modc_ref, g_ref, win_ref, qn_ref, wqa_ref, wqb_ref,
                     kvn_ref, wk_ref, wv_ref, cos_ref, sin_ref,
                     q_out, k_out, v_out, gm_out, xl_out, gl_out, *, tm, n_latent, d):
    is_ctx = _row_is_ctx(tm, n_latent)
    h = _modulated_norm(x_ref[0], g_ref[...], modb_ref, modc_ref, is_ctx, d).astype(BF16)

    def proj(lo, hi):
        return jnp.dot(h, win_ref[:, lo:hi], preferred_element_type=F32)

    cos = cos_ref[...]
    sin = sin_ref[...]
    cq = _rms(proj(0, 256), qn_ref[...]).astype(BF16)
    qa = jnp.dot(cq, wqa_ref[...], preferred_element_type=F32)
    qb = jnp.dot(cq, wqb_ref[...], preferred_element_type=F32)
    ckv = _rms(proj(256, 384), kvn_ref[...]).astype(BF16)
    kk = jnp.dot(ckv, wk_ref[...], preferred_element_type=F32)
    k_rope = proj(384, 512) * cos + proj(512, 640) * sin
    vv = jnp.dot(ckv, wv_ref[...], preferred_element_type=F32)
    ones_col = (lax.broadcasted_iota(jnp.int32, (tm, LANES), 1) == 0).astype(BF16)
    for hh in range(MLA_HEADS):
        sl = slice(hh * LANES, (hh + 1) * LANES)
        q_out[0, :, sl] = ((qa[:, sl] * cos + qb[:, sl] * sin) * (MLA_SCALE * LOG2E)).astype(BF16)
        k_out[0, :, sl] = (kk[:, sl] + k_rope).astype(BF16)
    for p in range(MLA_HEADS // 2):
        v_out[0, :, 2 * p * LANES:(2 * p + 1) * LANES] = vv[:, p * LANES:(p + 1) * LANES].astype(BF16)
        v_out[0, :, (2 * p + 1) * LANES:(2 * p + 2) * LANES] = ones_col
    gm_out[0] = _silu(proj(640, 1152))
    xl_out[0] = proj(1152, 1664)
    gl_out[0] = _silu(proj(1664, 2176))


def _const_spec(shape):
    nd = len(shape)
    return pl.BlockSpec(shape, lambda b, t: (0,) * nd)


def _even_pre(xc, modb, modc, g, wts, cos_t, sin_t, n_latent):
    b_, t_, d = xc.shape
    tm = ROW_TILE
    kern = functools.partial(_even_pre_kernel, tm=tm, n_latent=n_latent, d=d)
    tok = lambda w: pl.BlockSpec((1, tm, w), lambda b, t: (b, t, 0))
    hw = MLA_HEADS * LANES
    return pl.pallas_call(
        kern,
        grid=(b_, t_ // tm),
        in_specs=[
            tok(d),
            pl.BlockSpec((1, 1, 3 * d), lambda b, t: (b, 0, 0)),
            _const_spec((1, 1, 3 * d)),
            _const_spec((1, d)),
            _const_spec(wts["w_in"].shape),
            _const_spec((1, MLA_Q_RANK)),
            _const_spec(wts["wqa"].shape),
            _const_spec(wts["wqb"].shape),
            _const_spec((1, MLA_KV_RANK)),
            _const_spec(wts["wk"].shape),
            _const_spec(wts["wv"].shape),
            pl.BlockSpec((tm, LANES), lambda b, t: (t, 0)),
            pl.BlockSpec((tm, LANES), lambda b, t: (t, 0)),
        ],
        out_specs=[tok(hw), tok(hw), tok(hw), tok(MLA_WIDTH), tok(LRU_WIDTH), tok(LRU_WIDTH)],
        out_shape=[
            jax.ShapeDtypeStruct((b_, t_, hw), BF16),
            jax.ShapeDtypeStruct((b_, t_, hw), BF16),
            jax.ShapeDtypeStruct((b_, t_, hw), BF16),
            jax.ShapeDtypeStruct((b_, t_, MLA_WIDTH), F32),
            jax.ShapeDtypeStruct((b_, t_, LRU_WIDTH), F32),
            jax.ShapeDtypeStruct((b_, t_, LRU_WIDTH), F32),
        ],
        compiler_params=_cparams(("parallel", "parallel")),
        name="even_pre",
    )(xc, modb, modc, g, wts["w_in"], wts["q_norm"], wts["wqa"], wts["wqb"], wts["kv_norm"],
      wts["wk"], wts["wv"], cos_t, sin_t)


def _mla_attn_kernel(q_ref, k_ref, v_ref, o_ref, *, tq, tk, nk):
    accs = []
    for j in range(2):
        q = q_ref[0, :, j * LANES:(j + 1) * LANES]

        def body(c, carry, j=j, q=q):
            m, acc = carry
            start = pl.multiple_of(c * tk, LANES)
            k = k_ref[0, pl.ds(start, tk), j * LANES:(j + 1) * LANES]
            s = lax.dot_general(q, k, (((1,), (1,)), ((), ())), preferred_element_type=F32)
            m_new = jnp.maximum(m, jnp.max(s, axis=-1, keepdims=True))
            alpha = jnp.exp2(m - m_new)
            p = jnp.exp2(s - m_new).astype(BF16)
            v = v_ref[0, pl.ds(start, tk), :]
            acc = alpha * acc + jnp.dot(p, v, preferred_element_type=F32)
            return m_new, acc

        m0 = jnp.full((tq, 1), -jnp.inf, F32)
        acc0 = jnp.zeros((tq, 2 * LANES), F32)
        _, acc = lax.fori_loop(0, nk, body, (m0, acc0))
        accs.append(acc)
    lane = lax.broadcasted_iota(jnp.int32, (tq, LANES), 1)
    o0 = accs[0][:, :LANES] / accs[0][:, LANES:LANES + 1]
    o1 = accs[1][:, :LANES] / accs[1][:, LANES:LANES + 1]
    o_ref[0] = jnp.where(lane < MLA_V, o0, o1)


def _mla_attn_ctx_kernel(q_ref, k_ref, v_ref, prev_ref, o_ref, **kw):
    del prev_ref
    _mla_attn_kernel(q_ref, k_ref, v_ref, o_ref, **kw)


def _mla_attention(qc, kc, vx, n_latent):
    b_, t_, _ = qc.shape
    n_pairs = MLA_HEADS // 2
    tq, tk = MLA_TQ, MLA_TK
    n_ctx = t_ - n_latent
    out = pl.pallas_call(
        functools.partial(_mla_attn_kernel, tq=tq, tk=tk, nk=t_ // tk),
        grid=(b_, n_pairs, n_latent // tq),
        in_specs=[
            pl.BlockSpec((1, tq, 2 * LANES), lambda b, p, i: (b, i, p)),
            pl.BlockSpec((1, t_, 2 * LANES), lambda b, p, i: (b, 0, p)),
            pl.BlockSpec((1, t_, 2 * LANES), lambda b, p, i: (b, 0, p)),
        ],
        out_specs=pl.BlockSpec((1, tq, LANES), lambda b, p, i: (b, i, p)),
        out_shape=jax.ShapeDtypeStruct((b_, t_, MLA_WIDTH), F32),
        compiler_params=_cparams(("parallel", "parallel", "arbitrary")),
        name="mla_attn_latent",
    )(qc, kc, vx)
    cb = n_latent // n_ctx
    return pl.pallas_call(
        functools.partial(_mla_attn_ctx_kernel, tq=n_ctx, tk=n_ctx, nk=1),
        grid=(b_, n_pairs),
        in_specs=[
            pl.BlockSpec((1, n_ctx, 2 * LANES), lambda b, p: (b, cb, p)),
            pl.BlockSpec((1, n_ctx, 2 * LANES), lambda b, p: (b, cb, p)),
            pl.BlockSpec((1, n_ctx, 2 * LANES), lambda b, p: (b, cb, p)),
            pl.BlockSpec(memory_space=pl.ANY),
        ],
        out_specs=pl.BlockSpec((1, n_ctx, LANES), lambda b, p: (b, cb, p)),
        out_shape=jax.ShapeDtypeStruct((b_, t_, MLA_WIDTH), F32),
        input_output_aliases={3: 0},
        compiler_params=_cparams(("parallel", "parallel")),
        name="mla_attn_ctx",
    )(qc, kc, vx, out)


def _lru_kernel(x_ref, cw_ref, cb_ref, wg_ref, bg_ref, sp_ref, r_ref, hf_ref, hb_ref,
                *, tc, n_latent, n_ctx):
    nl = n_latent // tc
    ncx = n_ctx // tc
    row = lax.broadcasted_iota(jnp.int32, (tc, 1), 0)

    def conv_chunk(c0, seg_lo, seg_hi):
        xc = x_ref[0, pl.ds(c0, tc), :]
        has_prev = c0 > seg_lo
        has_next = c0 + tc < seg_hi
        p0 = pl.multiple_of(jnp.where(has_prev, c0 - 8, c0), 8)
        n0 = pl.multiple_of(jnp.where(has_next, c0 + tc, c0), 8)
        prev = jnp.where(has_prev, x_ref[0, pl.ds(p0, 8), :], 0.0)
        nxt = jnp.where(has_next, x_ref[0, pl.ds(n0, 8), :], 0.0)
        xm1 = jnp.where(row >= 1, pltpu.roll(xc, 1, 0), prev[7:8])
        xm2 = jnp.where(row >= 2, pltpu.roll(xc, 2, 0), jnp.where(row == 1, prev[7:8], prev[6:7]))
        xp1 = jnp.where(row <= tc - 2, pltpu.roll(xc, tc - 1, 0), nxt[0:1])
        return (cw_ref[0:1] * xm2 + cw_ref[1:2] * xm1 + cw_ref[2:3] * xc + cw_ref[3:4] * xp1
                + cb_ref[...])

    def coeffs(u, d):
        z = jnp.dot(u.astype(BF16), wg_ref[d], preferred_element_type=F32) + bg_ref[d]
        r = jax.nn.sigmoid(z[:, :LANES])
        i = jax.nn.sigmoid(z[:, LANES:])
        log_a = -LRU_C * r * sp_ref[d]
        a = jnp.exp(log_a)
        t = jnp.tanh(log_a)
        uu = jnp.sqrt(-2.0 * t / (1.0 - t)) * (i * u)
        return a, uu

    def scan_chunk(a, u, h_in, reverse):
        s = 1
        while s < tc:
            if reverse:
                keep = row < tc - s
                a_s = jnp.where(keep, pltpu.roll(a, tc - s, 0), 1.0)
                u_s = jnp.where(keep, pltpu.roll(u, tc - s, 0), 0.0)
            else:
                keep = row >= s
                a_s = jnp.where(keep, pltpu.roll(a, s, 0), 1.0)
                u_s = jnp.where(keep, pltpu.roll(u, s, 0), 0.0)
            u = a * u_s + u
            a = a * a_s
            s *= 2
        return a * h_in + u

    def chunk_start(i):
        return pl.multiple_of(jnp.where(i < ncx, n_latent + i * tc, (i - ncx) * tc), tc)

    def seg_bounds(i):
        lo = jnp.where(i < ncx, n_latent, 0)
        hi = jnp.where(i < ncx, n_latent + n_ctx, n_latent)
        return lo, hi

    hf_ref[...] = jnp.zeros_like(hf_ref)
    hb_ref[...] = jnp.zeros_like(hb_ref)

    def fwd(i, _):
        c0 = chunk_start(i)
        lo, hi = seg_bounds(i)
        a, uu = coeffs(conv_chunk(c0, lo, hi), 0)
        h = scan_chunk(a, uu, hf_ref[0:1], False)
        hf_ref[...] = jnp.broadcast_to(h[tc - 1:tc], hf_ref.shape)
        r_ref[0, pl.ds(c0, tc), :] = h
        return 0

    lax.fori_loop(0, ncx + nl, fwd, 0)

    def bwd(i, _):
        ii = jnp.where(i < ncx, ncx - 1 - i, ncx + (nl - 1 - (i - ncx)))
        c0 = chunk_start(ii)
        lo, hi = seg_bounds(ii)
        a, uu = coeffs(conv_chunk(c0, lo, hi), 1)
        h = scan_chunk(a, uu, hb_ref[0:1], True)
        hb_ref[...] = jnp.broadcast_to(h[0:1], hb_ref.shape)
        r_ref[0, pl.ds(c0, tc), :] = r_ref[0, pl.ds(c0, tc), :] + h
        return 0

    lax.fori_loop(0, ncx + nl, bwd, 0)


def _lru(x_lru, wts, n_latent):
    b_, t_, w = x_lru.shape
    ng = w // LANES
    kern = functools.partial(_lru_kernel, tc=LRU_CHUNK, n_latent=n_latent, n_ctx=t_ - n_latent)
    return pl.pallas_call(
        kern,
        grid=(b_, ng),
        in_specs=[
            pl.BlockSpec((1, t_, LANES), lambda b, g: (b, 0, g)),
            pl.BlockSpec((LRU_CONV, LANES), lambda b, g: (0, g)),
            pl.BlockSpec((1, LANES), lambda b, g: (0, g)),
            pl.BlockSpec((2, None, LANES, 2 * LANES), lambda b, g: (0, g, 0, 0)),
            pl.BlockSpec((2, None, 1, 2 * LANES), lambda b, g: (0, g, 0, 0)),
            pl.BlockSpec((2, None, 1, LANES), lambda b, g: (0, g, 0, 0)),
        ],
        out_specs=pl.BlockSpec((1, t_, LANES), lambda b, g: (b, 0, g)),
        out_shape=jax.ShapeDtypeStruct((b_, t_, w), F32),
        scratch_shapes=[pltpu.VMEM((8, LANES), F32), pltpu.VMEM((8, LANES), F32)],
        compiler_params=_cparams(("parallel", "parallel")),
        name="rglru",
    )(x_lru, wts["conv_w"], wts["conv_b"], wts["w_gate"], wts["b_gate"], wts["softplus"])


def _residual(x_ref, y, modb_ref, modc_ref, is_ctx, d):
    gate = jnp.where(is_ctx, modc_ref[0, :, 2 * d:3 * d], modb_ref[0, :, 2 * d:3 * d])
    return x_ref[0] + gate * y


def _even_post_kernel(x_ref, modb_ref, modc_ref, o_ref, gm_ref, r_ref, gl_ref, wo_ref, out_ref,
                      *, tm, n_latent, d):
    is_ctx = _row_is_ctx(tm, n_latent)
    a = (o_ref[0] * gm_ref[0]).astype(BF16)
    bb = (r_ref[0] * gl_ref[0]).astype(BF16)
    y = (jnp.dot(a, wo_ref[0:MLA_WIDTH, :], preferred_element_type=F32)
         + jnp.dot(bb, wo_ref[MLA_WIDTH:, :], preferred_element_type=F32))
    out_ref[0] = _residual(x_ref, y, modb_ref, modc_ref, is_ctx, d)


def _even_post(xc, modb, modc, o, gm, r, gl, w_out, n_latent):
    b_, t_, d = xc.shape
    tm = ROW_TILE
    tok = lambda w: pl.BlockSpec((1, tm, w), lambda b, t: (b, t, 0))
    return pl.pallas_call(
        functools.partial(_even_post_kernel, tm=tm, n_latent=n_latent, d=d),
        grid=(b_, t_ // tm),
        in_specs=[
            tok(d),
            pl.BlockSpec((1, 1, 3 * d), lambda b, t: (b, 0, 0)),
            _const_spec((1, 1, 3 * d)),
            tok(MLA_WIDTH), tok(MLA_WIDTH), tok(LRU_WIDTH), tok(LRU_WIDTH),
            _const_spec(w_out.shape),
        ],
        out_specs=tok(d),
        out_shape=jax.ShapeDtypeStruct((b_, t_, d), F32),
        compiler_params=_cparams(("parallel", "parallel")),
        name="even_post",
    )(xc, modb, modc, o, gm, r, gl, w_out)


def _odd_post_kernel(x_ref, modb_ref, modc_ref, o_ref, g_ref, wo_ref, fg_ref, out_ref,
                     *, tm, n_latent, d, final):
    is_ctx = _row_is_ctx(tm, n_latent)
    a = (o_ref[0] * g_ref[0]).astype(BF16)
    y = jnp.dot(a, wo_ref[...], preferred_element_type=F32)
    xn = _residual(x_ref, y, modb_ref, modc_ref, is_ctx, d)
    if final:
        xn = _rms(xn, fg_ref[...])
    out_ref[0] = xn


def _odd_post(xc, modb, modc, o, g, w_out, final_g, n_latent, final):
    b_, t_, d = xc.shape
    tm = FINAL_ROW_TILE if final else ROW_TILE
    n_rows = n_latent if final else t_
    tok = lambda w: pl.BlockSpec((1, tm, w), lambda b, t: (b, t, 0))
    return pl.pallas_call(
        functools.partial(_odd_post_kernel, tm=tm, n_latent=n_latent, d=d, final=final),
        grid=(b_, n_rows // tm),
        in_specs=[
            tok(d),
            pl.BlockSpec((1, 1, 3 * d), lambda b, t: (b, 0, 0)),
            _const_spec((1, 1, 3 * d)),
            tok(NA_WIDTH), tok(NA_WIDTH),
            _const_spec(w_out.shape),
            _const_spec((1, d)),
        ],
        out_specs=tok(d),
        out_shape=jax.ShapeDtypeStruct((b_, n_rows, d), F32),
        compiler_params=_cparams(("parallel", "parallel")),
        name="odd_post_final" if final else "odd_post",
    )(xc, modb, modc, o, g, w_out, final_g)


def _odd_pre_kernel(x_ref, modb_ref, modc_ref, g_ref, win_ref, q_out, k_out, v_out, g_out,
                    *, tm, n_latent, d):
    is_ctx = _row_is_ctx(tm, n_latent)
    h = _modulated_norm(x_ref[0], g_ref[...], modb_ref, modc_ref, is_ctx, d).astype(BF16)
    w = NA_WIDTH
    q = jnp.dot(h, win_ref[:, 0:w], preferred_element_type=F32)
    q_out[0] = (q * (NA_HEAD_DIM ** -0.5 * LOG2E)).astype(BF16)
    k_out[0] = jnp.dot(h, win_ref[:, w:2 * w], preferred_element_type=F32).astype(BF16)
    v_out[0] = jnp.dot(h, win_ref[:, 2 * w:3 * w], preferred_element_type=F32).astype(BF16)
    g_out[0] = _silu(jnp.dot(h, win_ref[:, 3 * w:4 * w], preferred_element_type=F32))


def _odd_pre_call(xc, modb, modc, g, w_in, n_latent):
    b_, t_, d = xc.shape
    tm = ROW_TILE
    tok = lambda w: pl.BlockSpec((1, tm, w), lambda b, t: (b, t, 0))
    return pl.pallas_call(
        functools.partial(_odd_pre_kernel, tm=tm, n_latent=n_latent, d=d),
        grid=(b_, t_ // tm),
        in_specs=[
            tok(d),
            pl.BlockSpec((1, 1, 3 * d), lambda b, t: (b, 0, 0)),
            _const_spec((1, 1, 3 * d)),
            _const_spec((1, d)),
            _const_spec(w_in.shape),
        ],
        out_specs=[tok(NA_WIDTH)] * 4,
        out_shape=[
            jax.ShapeDtypeStruct((b_, t_, NA_WIDTH), BF16),
            jax.ShapeDtypeStruct((b_, t_, NA_WIDTH), BF16),
            jax.ShapeDtypeStruct((b_, t_, NA_WIDTH), BF16),
            jax.ShapeDtypeStruct((b_, t_, NA_WIDTH), F32),
        ],
        compiler_params=_cparams(("parallel", "parallel")),
        name="odd_pre",
    )(xc, modb, modc, g, w_in)


def _na_kernel(q_ref, k_ref, v_ref, bias_ref, o_ref, *, nq, nkl, n_latent, n_ctx, n_blocks, rows):
    blk = pl.program_id(2)
    k_row0 = jnp.clip(blk * NA_QROWS - NA_WIN_R // 2, 0, rows - NA_KROWS)
    kstart = pl.multiple_of(k_row0 * GRID_W, GRID_W * NA_QROWS)
    q2 = q_ref[0]
    lane = lax.broadcasted_iota(jnp.int32, (nq, LANES), 1)
    k_loc = k_ref[0, pl.ds(kstart, nkl), :]
    v_loc = v_ref[0, pl.ds(kstart, nkl), :]
    k_ctx = k_ref[0, pl.ds(n_latent, n_ctx), :]
    v_ctx = v_ref[0, pl.ds(n_latent, n_ctx), :]
    outs = []
    nt = (((1,), (1,)), ((), ()))
    for j in range(2):
        in_head = (lane < NA_HEAD_DIM) if j == 0 else (lane >= NA_HEAD_DIM)
        q = jnp.where(in_head, q2, jnp.zeros_like(q2))
        s_loc = lax.dot_general(q, k_loc, nt, preferred_element_type=F32) + bias_ref[0, 0, j]
        s_ctx = lax.dot_general(q, k_ctx, nt, preferred_element_type=F32)
        m = jnp.maximum(jnp.max(s_loc, axis=-1, keepdims=True), jnp.max(s_ctx, axis=-1, keepdims=True))
        p_loc = jnp.exp2(s_loc - m)
        p_ctx = jnp.exp2(s_ctx - m)
        l = jnp.sum(p_loc, axis=-1, keepdims=True) + jnp.sum(p_ctx, axis=-1, keepdims=True)
        o = (jnp.dot(p_loc.astype(BF16), v_loc, preferred_element_type=F32)
             + jnp.dot(p_ctx.astype(BF16), v_ctx, preferred_element_type=F32))
        outs.append(o / l)
    o_ref[0] = jnp.where(lane < NA_HEAD_DIM, outs[0], outs[1])


def _na_bias_tables(rpb, rows):
    nq = NA_QROWS * GRID_W
    nkl = NA_KROWS * GRID_W
    qr = np.arange(nq) // GRID_W
    qc = np.arange(nq) % GRID_W
    kr = np.arange(nkl) // GRID_W
    kc = np.arange(nkl) % GRID_W
    cs = np.clip(qc - NA_WIN_C // 2, 0, GRID_W - NA_WIN_C)
    valid_c = (kc[None, :] >= cs[:, None]) & (kc[None, :] < cs[:, None] + NA_WIN_C)
    dc = np.clip(kc[None, :] - qc[:, None] + (NA_WIN_C - 1), 0, 2 * NA_WIN_C - 2)
    tabs = []
    for r0 in (0, 2 * NA_QROWS, rows - NA_QROWS):
        r = r0 + qr
        rs = np.clip(r - NA_WIN_R // 2, 0, rows - NA_WIN_R)
        ks = int(np.clip(r0 - NA_WIN_R // 2, 0, rows - NA_KROWS))
        kabs = ks + kr
        valid_r = (kabs[None, :] >= rs[:, None]) & (kabs[None, :] < rs[:, None] + NA_WIN_R)
        dr = np.clip(kabs[None, :] - r[:, None] + (NA_WIN_R - 1), 0, 2 * NA_WIN_R - 2)
        valid = valid_r & valid_c
        tab = jnp.where(valid[None], rpb[:, dr, dc] * LOG2E, NEG_BIG)
        tabs.append(tab)
    tabs.append(jnp.full_like(tabs[0], NEG_BIG))
    t = jnp.stack(tabs, axis=1)
    return t.reshape(NA_HEADS // 2, 2, 4, nq, nkl).transpose(0, 2, 1, 3, 4)


def _na_attention(q, k, v, bias, n_latent, need_ctx):
    b_, t_, _ = q.shape
    n_ctx = t_ - n_latent
    rows = n_latent // GRID_W
    nq = NA_QROWS * GRID_W
    nkl = NA_KROWS * GRID_W
    n_lat_blocks = rows // NA_QROWS
    n_blocks = n_lat_blocks + (1 if need_ctx else 0)
    n_rows_out = n_latent + (n_ctx if need_ctx else 0)

    def bias_idx(p, b, i):
        typ = jnp.where(i == 0, 0, jnp.where(i == n_lat_blocks - 1, 2, jnp.where(i >= n_lat_blocks, 3, 1)))
        return (p, typ, 0, 0, 0)

    return pl.pallas_call(
        functools.partial(_na_kernel, nq=nq, nkl=nkl, n_latent=n_latent, n_ctx=n_ctx,
                          n_blocks=n_blocks, rows=rows),
        grid=(NA_HEADS // 2, b_, n_blocks),
        in_specs=[
            pl.BlockSpec((1, nq, LANES), lambda p, b, i: (b, i, p)),
            pl.BlockSpec((1, t_, LANES), lambda p, b, i: (b, 0, p)),
            pl.BlockSpec((1, t_, LANES), lambda p, b, i: (b, 0, p)),
            pl.BlockSpec((1, 1, 2, nq, nkl), bias_idx),
        ],
        out_specs=pl.BlockSpec((1, nq, LANES), lambda p, b, i: (b, i, p)),
        out_shape=jax.ShapeDtypeStruct((b_, n_rows_out, NA_WIDTH), F32),
        compiler_params=_cparams(("parallel", "parallel", "arbitrary")),
        name="na_attn",
    )(q, k, v, bias)


def _rope_swap_index():
    half = MLA_ROPE // 4
    idx = np.arange(MLA_ROPE)
    return np.where((idx // half) % 2 == 0, idx + half, idx - half)


def _rope_tables(n_latent, n_ctx):
    n_freq = MLA_ROPE // 4
    inv = ROPE_THETA ** (-jnp.arange(n_freq, dtype=F32) / n_freq)
    t = jnp.arange(n_latent, dtype=jnp.int32)
    ang_r = (t // GRID_W).astype(F32)[:, None] * inv
    ang_c = (t % GRID_W).astype(F32)[:, None] * inv
    cr, sr, cc, sc = jnp.cos(ang_r), jnp.sin(ang_r), jnp.cos(ang_c), jnp.sin(ang_c)
    cos32 = jnp.concatenate([cr, cr, cc, cc], axis=-1)
    sin32 = jnp.concatenate([-sr, sr, -sc, sc], axis=-1)
    ones = jnp.ones((n_latent, MLA_NOPE), F32)
    zeros = jnp.zeros((n_latent, LANES - MLA_NOPE - MLA_ROPE), F32)
    cos_l = jnp.concatenate([ones, cos32, zeros], axis=-1)
    sin_l = jnp.concatenate([0 * ones, sin32, zeros], axis=-1)
    cos_c = jnp.concatenate([jnp.ones((n_ctx, MLA_NOPE + MLA_ROPE), F32),
                             jnp.zeros((n_ctx, LANES - MLA_NOPE - MLA_ROPE), F32)], axis=-1)
    sin_c = jnp.zeros((n_ctx, LANES), F32)
    return jnp.concatenate([cos_l, cos_c], axis=0), jnp.concatenate([sin_l, sin_c], axis=0)


def _even_weights(i, ev_w_in, mla_q_norm, mla_w_uq, mla_kv_norm, mla_w_ukv, lru_conv_w, lru_conv_b,
                  lru_wa, lru_ba, lru_wx, lru_bx, lru_lambda, ev_w_out):
    sw = _rope_swap_index()
    d = ev_w_in.shape[1]
    w_in = ev_w_in[i]
    o_cq, o_ckv, o_kr = MLA_Q_RANK, MLA_Q_RANK + MLA_KV_RANK, MLA_Q_RANK + MLA_KV_RANK + MLA_ROPE
    kr = w_in[:, o_ckv:o_kr]
    pad_lo = jnp.zeros((d, MLA_NOPE), F32)
    pad_hi = jnp.zeros((d, LANES - MLA_NOPE - MLA_ROPE), F32)
    w_in_ext = jnp.concatenate(
        [w_in[:, :o_ckv], pad_lo, kr, pad_hi, pad_lo, kr[:, sw], pad_hi, w_in[:, o_kr:]], axis=1)
    wq = mla_w_uq[i].reshape(MLA_Q_RANK, MLA_HEADS, MLA_NOPE + MLA_ROPE)
    zq = jnp.zeros((MLA_Q_RANK, MLA_HEADS, LANES - MLA_NOPE - MLA_ROPE), F32)
    wqa = jnp.concatenate([wq, zq], axis=-1).reshape(MLA_Q_RANK, MLA_HEADS * LANES)
    wqb = jnp.concatenate([jnp.zeros((MLA_Q_RANK, MLA_HEADS, MLA_NOPE), F32),
                           wq[:, :, MLA_NOPE:][:, :, sw], zq], axis=-1).reshape(MLA_Q_RANK, MLA_HEADS * LANES)
    wkv = mla_w_ukv[i].reshape(MLA_KV_RANK, MLA_HEADS, MLA_NOPE + MLA_V)
    wk = jnp.concatenate([wkv[:, :, :MLA_NOPE], jnp.zeros((MLA_KV_RANK, MLA_HEADS, LANES - MLA_NOPE), F32)],
                         axis=-1).reshape(MLA_KV_RANK, MLA_HEADS * LANES)
    wv = wkv[:, :, MLA_NOPE:].reshape(MLA_KV_RANK, MLA_WIDTH)
    ng = LRU_WIDTH // LANES
    per = LANES // LRU_BLOCK

    def blockdiag(w):
        w = w.reshape(2, ng, per, LRU_BLOCK, LRU_BLOCK)
        eye = jnp.eye(per, dtype=F32)
        return jnp.einsum('dgpkj,pq->dgpkqj', w, eye).reshape(2, ng, LANES, LANES)

    w_gate = jnp.concatenate([blockdiag(lru_wa[i]), blockdiag(lru_wx[i])], axis=-1).astype(BF16)
    b_gate = jnp.concatenate([lru_ba[i].reshape(2, ng, 1, LANES), lru_bx[i].reshape(2, ng, 1, LANES)], axis=-1)
    return dict(
        w_in=w_in_ext.astype(BF16), q_norm=mla_q_norm[i][None], wqa=wqa.astype(BF16), wqb=wqb.astype(BF16),
        kv_norm=mla_kv_norm[i][None], wk=wk.astype(BF16), wv=wv.astype(BF16),
        conv_w=lru_conv_w[i], conv_b=lru_conv_b[i][None], w_gate=w_gate, b_gate=b_gate,
        softplus=jax.nn.softplus(-lru_lambda[i]).reshape(2, ng, 1, LANES),
        w_out=ev_w_out[i].astype(BF16))


def kernel(x, c, ctx, c_ctx, ada_w, ada_b, norm_g, ev_w_in, mla_q_norm, mla_w_uq, mla_kv_norm, mla_w_ukv,
           lru_conv_w, lru_conv_b, lru_wa, lru_ba, lru_wx, lru_bx, lru_lambda, ev_w_out, od_w_in, na_rpb,
           od_w_out, final_norm_g):
    b_, s_, d = x.shape
    n_ctx = ctx.shape[1]
    depth = ada_w.shape[0]
    rows = s_ // GRID_W
    assert b_ <= 7 and s_ % GRID_W == 0 and rows % NA_QROWS == 0 and rows >= NA_KROWS
    assert (s_ + n_ctx) % ROW_TILE == 0 and s_ % FINAL_ROW_TILE == 0 and s_ % MLA_TQ == 0
    assert (s_ + n_ctx) % MLA_TK == 0 and s_ % n_ctx == 0 and n_ctx % LRU_CHUNK == 0 and s_ % LRU_CHUNK == 0
    assert n_ctx == NA_QROWS * GRID_W and depth % 2 == 0

    cond = jnp.zeros((8, d), F32).at[:b_].set(c).at[b_].set(c_ctx)
    mod = _adaln(cond, ada_w, ada_b)
    xc = jnp.concatenate([x, ctx], axis=1)
    cos_t, sin_t = _rope_tables(s_, n_ctx)

    for layer in range(depth):
        need_ctx = layer < depth - 1
        i = layer // 2
        modb = mod[layer, :b_][:, None, :]
        modc = mod[layer, b_][None, None, :]
        g = norm_g[layer][None]
        if layer % 2 == 0:
            wts = _even_weights(i, ev_w_in, mla_q_norm, mla_w_uq, mla_kv_norm, mla_w_ukv, lru_conv_w,
                                lru_conv_b, lru_wa, lru_ba, lru_wx, lru_bx, lru_lambda, ev_w_out)
            qc, kc, vx, gm, xl, gl = _even_pre(xc, modb, modc, g, wts, cos_t, sin_t, s_)
            o = _mla_attention(qc, kc, vx, s_)
            r = _lru(xl, wts, s_)
            xc = _even_post(xc, modb, modc, o, gm, r, gl, wts["w_out"], s_)
        else:
            q, k, v, gg = _odd_pre_call(xc, modb, modc, g, od_w_in[i].astype(BF16), s_)
            bias = _na_bias_tables(na_rpb[i], rows)
            o = _na_attention(q, k, v, bias, s_, need_ctx)
            xc = _odd_post(xc, modb, modc, o, gg, od_w_out[i].astype(BF16), final_norm_g[None], s_,
                           final=not need_ctx)
    return xc
```

```python
import functools
import math

import numpy as np
import jax
import jax.numpy as jnp
from jax import lax
from jax.experimental import pallas as pl
from jax.experimental.pallas import tpu as pltpu

F32 = jnp.float32
BF16 = jnp.bfloat16

GRID_W = 64
RMS_EPS = 1e-6
ROPE_THETA = 10000.0
MLA_HEADS = 8
MLA_NOPE = 64
MLA_ROPE = 32
MLA_V = 64
MLA_Q_RANK = 256
MLA_KV_RANK = 128
MLA_WIDTH = MLA_HEADS * MLA_V
MLA_SCALE = (MLA_NOPE + MLA_ROPE) ** -0.5
LRU_WIDTH = 512
LRU_BLOCKS = 8
LRU_BLOCK = LRU_WIDTH // LRU_BLOCKS
LRU_CONV = 4
LRU_C = 8.0
NA_HEADS = 16
NA_HEAD_DIM = 64
NA_WIDTH = NA_HEADS * NA_HEAD_DIM
NA_WIN_R = 8
NA_WIN_C = 16

LANES = 128
LOG2E = 1.4426950408889634
NEG_BIG = -1e30
VMEM_LIMIT = 56 * 1024 * 1024

ROW_TILE = 768
FINAL_ROW_TILE = 512
MLA_TQ = 512
MLA_TK = 768
NA_QROWS = 4
NA_KROWS = 12
LRU_CHUNK = 256


def _cparams(sem):
    return pltpu.CompilerParams(dimension_semantics=sem, vmem_limit_bytes=VMEM_LIMIT)


def _silu(v):
    return v * jax.nn.sigmoid(v)


def _adaln_kernel(cond_ref, w_ref, b_ref, o_ref):
    a = _silu(cond_ref[...])
    o_ref[0] = jnp.dot(a, w_ref[0], preferred_element_type=F32) + b_ref[0]


def _adaln(cond, ada_w, ada_b):
    depth, d, d3 = ada_w.shape
    tn = 1024
    return pl.pallas_call(
        _adaln_kernel,
        grid=(depth, d3 // tn),
        in_specs=[
            pl.BlockSpec((8, d), lambda l, n: (0, 0)),
            pl.BlockSpec((1, d, tn), lambda l, n: (l, 0, n)),
            pl.BlockSpec((1, 1, tn), lambda l, n: (l, 0, n)),
        ],
        out_specs=pl.BlockSpec((1, 8, tn), lambda l, n: (l, 0, n)),
        out_shape=jax.ShapeDtypeStruct((depth, 8, d3), F32),
        compiler_params=_cparams(("arbitrary", "arbitrary")),
        name="adaln",
    )(cond, ada_w, ada_b.reshape(depth, 1, d3))


def _modulated_norm(x, g, modb_ref, modc_ref, is_ctx, d):
    ms = jnp.mean(x * x, axis=-1, keepdims=True)
    y = x * lax.rsqrt(ms + RMS_EPS) * g
    shift = jnp.where(is_ctx, modc_ref[0, :, 0:d], modb_ref[0, :, 0:d])
    scale = jnp.where(is_ctx, modc_ref[0, :, d:2 * d], modb_ref[0, :, d:2 * d])
    return y * (1.0 + scale) + shift


def _row_is_ctx(tm, n_latent):
    row = pl.program_id(1) * tm + lax.broadcasted_iota(jnp.int32, (tm, 1), 0)
    return row >= n_latent


def _rms(v, g):
    ms = jnp.mean(v * v, axis=-1, keepdims=True)
    return v * lax.rsqrt(ms + RMS_EPS) * g


def _even_pre_kernel(x_ref, modb_ref, modc_ref, g_ref, win_ref, qn_ref, wqa_ref, wqb_ref,
                     kvn_ref, wk_ref, wv_ref, cos_ref, sin_ref,
                     q_out, k_out, v_out, gm_out, xl_out, gl_out, *, tm, n_latent, d):
    is_ctx = _row_is_ctx(tm, n_latent)
    h = _modulated_norm(x_ref[0], g_ref[...], modb_ref, modc_ref, is_ctx, d).astype(BF16)

    def proj(lo, hi):
        return jnp.dot(h, win_ref[:, lo:hi], preferred_element_type=F32)

    cos = cos_ref[...]
    sin = sin_ref[...]
    cq = _rms(proj(0, 256), qn_ref[...]).astype(BF16)
    qa = jnp.dot(cq, wqa_ref[...], preferred_element_type=F32)
    qb = jnp.dot(cq, wqb_ref[...], preferred_element_type=F32)
    ckv = _rms(proj(256, 384), kvn_ref[...]).astype(BF16)
    kk = jnp.dot(ckv, wk_ref[...], preferred_element_type=F32)
    k_rope = proj(384, 512) * cos + proj(512, 640) * sin
    vv = jnp.dot(ckv, wv_ref[...], preferred_element_type=F32)
    ones_col = (lax.broadcasted_iota(jnp.int32, (tm, LANES), 1) == 0).astype(BF16)
    for hh in range(MLA_HEADS):
        sl = slice(hh * LANES, (hh + 1) * LANES)
        q_out[0, :, sl] = ((qa[:, sl] * cos + qb[:, sl] * sin) * (MLA_SCALE * LOG2E)).astype(BF16)
        k_out[0, :, sl] = (kk[:, sl] + k_rope).astype(BF16)
    for p in range(MLA_HEADS // 2):
        v_out[0, :, 2 * p * LANES:(2 * p + 1) * LANES] = vv[:, p * LANES:(p + 1) * LANES].astype(BF16)
        v_out[0, :, (2 * p + 1) * LANES:(2 * p + 2) * LANES] = ones_col
    gm_out[0] = _silu(proj(640, 1152))
    xl_out[0] = proj(1152, 1664)
    gl_out[0] = _silu(proj(1664, 2176))


def _const_spec(shape):
    nd = len(shape)
    return pl.BlockSpec(shape, lambda b, t: (0,) * nd)


def _even_pre(xc, modb, modc, g, wts, cos_t, sin_t, n_latent):
    b_, t_, d = xc.shape
    tm = ROW_TILE
    kern = functools.partial(_even_pre_kernel, tm=tm, n_latent=n_latent, d=d)
    tok = lambda w: pl.BlockSpec((1, tm, w), lambda b, t: (b, t, 0))
    hw = MLA_HEADS * LANES
    return pl.pallas_call(
        kern,
        grid=(b_, t_ // tm),
        in_specs=[
            tok(d),
            pl.BlockSpec((1, 1, 3 * d), lambda b, t: (b, 0, 0)),
            _const_spec((1, 1, 3 * d)),
            _const_spec((1, d)),
            _const_spec(wts["w_in"].shape),
            _const_spec((1, MLA_Q_RANK)),
            _const_spec(wts["wqa"].shape),
            _const_spec(wts["wqb"].shape),
            _const_spec((1, MLA_KV_RANK)),
            _const_spec(wts["wk"].shape),
            _const_spec(wts["wv"].shape),
            pl.BlockSpec((tm, LANES), lambda b, t: (t, 0)),
            pl.BlockSpec((tm, LANES), lambda b, t: (t, 0)),
        ],
        out_specs=[tok(hw), tok(hw), tok(hw), tok(MLA_WIDTH), tok(LRU_WIDTH), tok(LRU_WIDTH)],
        out_shape=[
            jax.ShapeDtypeStruct((b_, t_, hw), BF16),
            jax.ShapeDtypeStruct((b_, t_, hw), BF16),
            jax.ShapeDtypeStruct((b_, t_, hw), BF16),
            jax.ShapeDtypeStruct((b_, t_, MLA_WIDTH), F32),
            jax.ShapeDtypeStruct((b_, t_, LRU_WIDTH), F32),
            jax.ShapeDtypeStruct((b_, t_, LRU_WIDTH), F32),
        ],
        compiler_params=_cparams(("parallel", "parallel")),
        name="even_pre",
    )(xc, modb, modc, g, wts["w_in"], wts["q_norm"], wts["wqa"], wts["wqb"], wts["kv_norm"],
      wts["wk"], wts["wv"], cos_t, sin_t)


def _mla_attn_kernel(q_ref, k_ref, v_ref, o_ref, p_a, p_b, acc_ref, m_ref, al_ref, *, tq, tk, nk):
    def chunk(c):
        return c * tk if isinstance(c, int) else pl.multiple_of(c * tk, LANES)

    def score_stage(c, p_ref):
        for j in range(2):
            k = k_ref[0, pl.ds(chunk(c), tk), j * LANES:(j + 1) * LANES]
            s = lax.dot_general(q_ref[0, :, j * LANES:(j + 1) * LANES], k, (((1,), (1,)), ((), ())),
                                preferred_element_type=F32)
            m = m_ref[j]
            m_new = jnp.maximum(m, jnp.max(s, axis=-1, keepdims=True))
            al_ref[j] = jnp.exp2(m - m_new)
            m_ref[j] = m_new
            p_ref[j] = jnp.exp2(s - m_new).astype(BF16)

    def value_stage(c, p_ref):
        v = v_ref[0, pl.ds(chunk(c), tk), :]
        for j in range(2):
            acc_ref[j] = al_ref[j] * acc_ref[j] + jnp.dot(p_ref[j], v, preferred_element_type=F32)

    m_ref[...] = jnp.full(m_ref.shape, -jnp.inf, F32)
    acc_ref[...] = jnp.zeros(acc_ref.shape, F32)
    score_stage(0, p_a)

    def body(i, _):
        c = 2 * i
        value_stage(c, p_a)
        score_stage(c + 1, p_b)
        value_stage(c + 1, p_b)
        score_stage(c + 2, p_a)
        return 0

    lax.fori_loop(0, (nk - 1) // 2, body, 0)
    if (nk - 1) % 2 == 1:
        value_stage(nk - 2, p_a)
        score_stage(nk - 1, p_b)
        value_stage(nk - 1, p_b)
    else:
        value_stage(nk - 1, p_a)
    lane = lax.broadcasted_iota(jnp.int32, (tq, LANES), 1)
    o0 = acc_ref[0, :, :LANES] / acc_ref[0, :, LANES:LANES + 1]
    o1 = acc_ref[1, :, :LANES] / acc_ref[1, :, LANES:LANES + 1]
    o_ref[0] = jnp.where(lane < MLA_V, o0, o1)


def _mla_attn_t_kernel(q_ref, k_ref, v_ref, o_ref, *scratch, tq, tk, nk):
    s_buf, p_buf = scratch[0:2], scratch[2:4]
    acc_ref, m_ref, l_ref, al_ref = scratch[4:8], scratch[8:12], scratch[12:16], scratch[16:20]
    nt = (((1,), (1,)), ((), ()))
    tn = (((0,), (0,)), ((), ()))
    hq = tq // 2
    n_sub = 4

    def chunk(c):
        return c * tk if isinstance(c, int) else pl.multiple_of(c * tk, LANES)

    def score(c, i):
        j, h = divmod(i, 2)
        k = k_ref[0, pl.ds(chunk(c), tk), j * LANES:(j + 1) * LANES]
        q = q_ref[0, h * hq:(h + 1) * hq, j * LANES:(j + 1) * LANES]
        s = lax.dot_general(k, q, nt, preferred_element_type=F32)
        m = m_ref[i][...]
        m_new = jnp.maximum(m, jnp.max(s, axis=0, keepdims=True))
        al_ref[i][...] = jnp.exp2(m - m_new)
        m_ref[i][...] = m_new
        s_buf[i % 2][...] = s

    def prob(c, i):
        p = jnp.exp2(s_buf[i % 2][...] - m_ref[i][...])
        l_ref[i][...] = al_ref[i][...] * l_ref[i][...] + jnp.sum(p, axis=0, keepdims=True)
        p_buf[i % 2][...] = p.astype(BF16)

    def value(c, i):
        v = v_ref[0, pl.ds(chunk(c), tk), :]
        pv = lax.dot_general(v, p_buf[i % 2][...], tn, preferred_element_type=F32)
        acc_ref[i][...] = al_ref[i][...] * acc_ref[i][...] + pv

    for i in range(n_sub):
        m_ref[i][...] = jnp.full(m_ref[i].shape, -jnp.inf, F32)
        l_ref[i][...] = jnp.zeros(l_ref[i].shape, F32)
        acc_ref[i][...] = jnp.zeros(acc_ref[i].shape, F32)

    def group(c, i, last):
        value(c, i)
        if i + 1 < n_sub:
            prob(c, i + 1)
        elif not last:
            prob(c + 1, 0)
        if i + 2 < n_sub:
            score(c, i + 2)
        elif not last:
            score(c + 1, i + 2 - n_sub)

    score(0, 0)
    prob(0, 0)
    score(0, 1)

    def body(c, _):
        for i in range(n_sub):
            group(c, i, False)
        return 0

    lax.fori_loop(0, nk - 1, body, 0)
    for i in range(n_sub):
        group(nk - 1, i, True)

    o_halves = []
    for h in range(2):
        o0 = acc_ref[h][...] / l_ref[h][...]
        o1 = acc_ref[2 + h][...] / l_ref[2 + h][...]
        o_halves.append(jnp.concatenate([o0[:MLA_V], o1[MLA_V:]], axis=0).T)
    o_ref[0] = jnp.concatenate(o_halves, axis=0)


def _mla_attn_ctx_kernel(q_ref, k_ref, v_ref, prev_ref, o_ref, *scratch, **kw):
    del prev_ref
    _mla_attn_kernel(q_ref, k_ref, v_ref, o_ref, *scratch, **kw)


def _mla_scratch(tq, tk):
    return [pltpu.VMEM((2, tq, tk), BF16), pltpu.VMEM((2, tq, tk), BF16),
            pltpu.VMEM((2, tq, 2 * LANES), F32), pltpu.VMEM((2, tq, 1), F32), pltpu.VMEM((2, tq, 1), F32)]


def _mla_attention(qc, kc, vx, n_latent):
    b_, t_, _ = qc.shape
    n_pairs = MLA_HEADS // 2
    tq, tk = MLA_TQ, MLA_TK
    n_ctx = t_ - n_latent
    out = pl.pallas_call(
        functools.partial(_mla_attn_t_kernel, tq=tq, tk=tk, nk=t_ // tk),
        grid=(b_, n_pairs, n_latent // tq),
        in_specs=[
            pl.BlockSpec((1, tq, 2 * LANES), lambda b, p, i: (b, i, p)),
            pl.BlockSpec((1, t_, 2 * LANES), lambda b, p, i: (b, 0, p)),
            pl.BlockSpec((1, t_, LANES), lambda b, p, i: (b, 0, 2 * p)),
        ],
        out_specs=pl.BlockSpec((1, tq, LANES), lambda b, p, i: (b, i, p)),
        out_shape=jax.ShapeDtypeStruct((b_, t_, MLA_WIDTH), F32),
        scratch_shapes=([pltpu.VMEM((tk, tq // 2), F32)] * 2 + [pltpu.VMEM((tk, tq // 2), BF16)] * 2
                        + [pltpu.VMEM((LANES, tq // 2), F32)] * 4 + [pltpu.VMEM((1, tq // 2), F32)] * 12),
        compiler_params=_cparams(("parallel", "parallel", "arbitrary")),
        name="mla_attn_latent",
    )(qc, kc, vx)
    cb = n_latent // n_ctx
    return pl.pallas_call(
        functools.partial(_mla_attn_ctx_kernel, tq=n_ctx, tk=n_ctx, nk=1),
        grid=(b_, n_pairs),
        in_specs=[
            pl.BlockSpec((1, n_ctx, 2 * LANES), lambda b, p: (b, cb, p)),
            pl.BlockSpec((1, n_ctx, 2 * LANES), lambda b, p: (b, cb, p)),
            pl.BlockSpec((1, n_ctx, 2 * LANES), lambda b, p: (b, cb, p)),
            pl.BlockSpec(memory_space=pl.ANY),
        ],
        out_specs=pl.BlockSpec((1, n_ctx, LANES), lambda b, p: (b, cb, p)),
        out_shape=jax.ShapeDtypeStruct((b_, t_, MLA_WIDTH), F32),
        input_output_aliases={3: 0},
        scratch_shapes=_mla_scratch(n_ctx, n_ctx),
        compiler_params=_cparams(("parallel", "parallel")),
        name="mla_attn_ctx",
    )(qc, kc, vx, out)


def _lru_kernel(x_ref, cw_ref, cb_ref, wg_ref, bg_ref, sp_ref, r_ref, hf_ref, hb_ref,
                *, tc, n_latent, n_ctx):
    nl = n_latent // tc
    ncx = n_ctx // tc
    row = lax.broadcasted_iota(jnp.int32, (tc, 1), 0)

    def conv_chunk(c0, seg_lo, seg_hi):
        xc = x_ref[0, pl.ds(c0, tc), :]
        has_prev = c0 > seg_lo
        has_next = c0 + tc < seg_hi
        p0 = pl.multiple_of(jnp.where(has_prev, c0 - 8, c0), 8)
        n0 = pl.multiple_of(jnp.where(has_next, c0 + tc, c0), 8)
        prev = jnp.where(has_prev, x_ref[0, pl.ds(p0, 8), :], 0.0)
        nxt = jnp.where(has_next, x_ref[0, pl.ds(n0, 8), :], 0.0)
        xm1 = jnp.where(row >= 1, pltpu.roll(xc, 1, 0), prev[7:8])
        xm2 = jnp.where(row >= 2, pltpu.roll(xc, 2, 0), jnp.where(row == 1, prev[7:8], prev[6:7]))
        xp1 = jnp.where(row <= tc - 2, pltpu.roll(xc, tc - 1, 0), nxt[0:1])
        return (cw_ref[0:1] * xm2 + cw_ref[1:2] * xm1 + cw_ref[2:3] * xc + cw_ref[3:4] * xp1
                + cb_ref[...])

    def coeffs(u, d):
        z = jnp.dot(u.astype(BF16), wg_ref[d], preferred_element_type=F32) + bg_ref[d]
        r = jax.nn.sigmoid(z[:, :LANES])
        i = jax.nn.sigmoid(z[:, LANES:])
        log_a = -LRU_C * r * sp_ref[d]
        a = jnp.exp(log_a)
        t = jnp.tanh(log_a)
        uu = jnp.sqrt(-2.0 * t / (1.0 - t)) * (i * u)
        return a, uu

    def scan_chunk(a, u, h_in, reverse):
        s = 1
        while s < tc:
            if reverse:
                keep = row < tc - s
                a_s = jnp.where(keep, pltpu.roll(a, tc - s, 0), 1.0)
                u_s = jnp.where(keep, pltpu.roll(u, tc - s, 0), 0.0)
            else:
                keep = row >= s
                a_s = jnp.where(keep, pltpu.roll(a, s, 0), 1.0)
                u_s = jnp.where(keep, pltpu.roll(u, s, 0), 0.0)
            u = a * u_s + u
            a = a * a_s
            s *= 2
        return a * h_in + u

    def chunk_start(i):
        return pl.multiple_of(jnp.where(i < ncx, n_latent + i * tc, (i - ncx) * tc), tc)

    def seg_bounds(i):
        lo = jnp.where(i < ncx, n_latent, 0)
        hi = jnp.where(i < ncx, n_latent + n_ctx, n_latent)
        return lo, hi

    hf_ref[...] = jnp.zeros_like(hf_ref)
    hb_ref[...] = jnp.zeros_like(hb_ref)

    def fwd(i, _):
        c0 = chunk_start(i)
        lo, hi = seg_bounds(i)
        a, uu = coeffs(conv_chunk(c0, lo, hi), 0)
        h = scan_chunk(a, uu, hf_ref[0:1], False)
        hf_ref[...] = jnp.broadcast_to(h[tc - 1:tc], hf_ref.shape)
        r_ref[0, pl.ds(c0, tc), :] = h
        return 0

    lax.fori_loop(0, ncx + nl, fwd, 0)

    def bwd(i, _):
        ii = jnp.where(i < ncx, ncx - 1 - i, ncx + (nl - 1 - (i - ncx)))
        c0 = chunk_start(ii)
        lo, hi = seg_bounds(ii)
        a, uu = coeffs(conv_chunk(c0, lo, hi), 1)
        h = scan_chunk(a, uu, hb_ref[0:1], True)
        hb_ref[...] = jnp.broadcast_to(h[0:1], hb_ref.shape)
        r_ref[0, pl.ds(c0, tc), :] = r_ref[0, pl.ds(c0, tc), :] + h
        return 0

    lax.fori_loop(0, ncx + nl, bwd, 0)


def _lru(x_lru, wts, n_latent):
    b_, t_, w = x_lru.shape
    ng = w // LANES
    kern = functools.partial(_lru_kernel, tc=LRU_CHUNK, n_latent=n_latent, n_ctx=t_ - n_latent)
    return pl.pallas_call(
        kern,
        grid=(b_, ng),
        in_specs=[
            pl.BlockSpec((1, t_, LANES), lambda b, g: (b, 0, g)),
            pl.BlockSpec((LRU_CONV, LANES), lambda b, g: (0, g)),
            pl.BlockSpec((1, LANES), lambda b, g: (0, g)),
            pl.BlockSpec((2, None, LANES, 2 * LANES), lambda b, g: (0, g, 0, 0)),
            pl.BlockSpec((2, None, 1, 2 * LANES), lambda b, g: (0, g, 0, 0)),
            pl.BlockSpec((2, None, 1, LANES), lambda b, g: (0, g, 0, 0)),
        ],
        out_specs=pl.BlockSpec((1, t_, LANES), lambda b, g: (b, 0, g)),
        out_shape=jax.ShapeDtypeStruct((b_, t_, w), F32),
        scratch_shapes=[pltpu.VMEM((8, LANES), F32), pltpu.VMEM((8, LANES), F32)],
        compiler_params=_cparams(("parallel", "parallel")),
        name="rglru",
    )(x_lru, wts["conv_w"], wts["conv_b"], wts["w_gate"], wts["b_gate"], wts["softplus"])


def _residual(x_ref, y, modb_ref, modc_ref, is_ctx, d):
    gate = jnp.where(is_ctx, modc_ref[0, :, 2 * d:3 * d], modb_ref[0, :, 2 * d:3 * d])
    return x_ref[0] + gate * y


def _even_post_kernel(x_ref, modb_ref, modc_ref, o_ref, gm_ref, r_ref, gl_ref, wo_ref, out_ref,
                      *, tm, n_latent, d):
    is_ctx = _row_is_ctx(tm, n_latent)
    a = (o_ref[0] * gm_ref[0]).astype(BF16)
    bb = (r_ref[0] * gl_ref[0]).astype(BF16)
    y = (jnp.dot(a, wo_ref[0:MLA_WIDTH, :], preferred_element_type=F32)
         + jnp.dot(bb, wo_ref[MLA_WIDTH:, :], preferred_element_type=F32))
    out_ref[0] = _residual(x_ref, y, modb_ref, modc_ref, is_ctx, d)


def _even_post(xc, modb, modc, o, gm, r, gl, w_out, n_latent):
    b_, t_, d = xc.shape
    tm = ROW_TILE
    tok = lambda w: pl.BlockSpec((1, tm, w), lambda b, t: (b, t, 0))
    return pl.pallas_call(
        functools.partial(_even_post_kernel, tm=tm, n_latent=n_latent, d=d),
        grid=(b_, t_ // tm),
        in_specs=[
            tok(d),
            pl.BlockSpec((1, 1, 3 * d), lambda b, t: (b, 0, 0)),
            _const_spec((1, 1, 3 * d)),
            tok(MLA_WIDTH), tok(MLA_WIDTH), tok(LRU_WIDTH), tok(LRU_WIDTH),
            _const_spec(w_out.shape),
        ],
        out_specs=tok(d),
        out_shape=jax.ShapeDtypeStruct((b_, t_, d), F32),
        compiler_params=_cparams(("parallel", "parallel")),
        name="even_post",
    )(xc, modb, modc, o, gm, r, gl, w_out)


def _odd_post_kernel(x_ref, modb_ref, modc_ref, o_ref, g_ref, wo_ref, fg_ref, out_ref,
                     *, tm, n_latent, d, final):
    is_ctx = _row_is_ctx(tm, n_latent)
    a = (o_ref[0] * g_ref[0]).astype(BF16)
    y = jnp.dot(a, wo_ref[...], preferred_element_type=F32)
    xn = _residual(x_ref, y, modb_ref, modc_ref, is_ctx, d)
    if final:
        xn = _rms(xn, fg_ref[...])
    out_ref[0] = xn


def _odd_post(xc, modb, modc, o, g, w_out, final_g, n_latent, final):
    b_, t_, d = xc.shape
    tm = FINAL_ROW_TILE if final else ROW_TILE
    n_rows = n_latent if final else t_
    tok = lambda w: pl.BlockSpec((1, tm, w), lambda b, t: (b, t, 0))
    return pl.pallas_call(
        functools.partial(_odd_post_kernel, tm=tm, n_latent=n_latent, d=d, final=final),
        grid=(b_, n_rows // tm),
        in_specs=[
            tok(d),
            pl.BlockSpec((1, 1, 3 * d), lambda b, t: (b, 0, 0)),
            _const_spec((1, 1, 3 * d)),
            tok(NA_WIDTH), tok(NA_WIDTH),
            _const_spec(w_out.shape),
            _const_spec((1, d)),
        ],
        out_specs=tok(d),
        out_shape=jax.ShapeDtypeStruct((b_, n_rows, d), F32),
        compiler_params=_cparams(("parallel", "parallel")),
        name="odd_post_final" if final else "odd_post",
    )(xc, modb, modc, o, g, w_out, final_g)


def _odd_pre_kernel(x_ref, modb_ref, modc_ref, g_ref, win_ref, q_out, k_out, v_out, g_out,
                    *, tm, n_latent, d):
    is_ctx = _row_is_ctx(tm, n_latent)
    h = _modulated_norm(x_ref[0], g_ref[...], modb_ref, modc_ref, is_ctx, d).astype(BF16)
    w = NA_WIDTH
    q = jnp.dot(h, win_ref[:, 0:w], preferred_element_type=F32)
    q_out[0] = (q * (NA_HEAD_DIM ** -0.5 * LOG2E)).astype(BF16)
    k_out[0] = jnp.dot(h, win_ref[:, w:2 * w], preferred_element_type=F32).astype(BF16)
    v_out[0] = jnp.dot(h, win_ref[:, 2 * w:3 * w], preferred_element_type=F32).astype(BF16)
    g_out[0] = _silu(jnp.dot(h, win_ref[:, 3 * w:4 * w], preferred_element_type=F32))


def _odd_pre_call(xc, modb, modc, g, w_in, n_latent):
    b_, t_, d = xc.shape
    tm = ROW_TILE
    tok = lambda w: pl.BlockSpec((1, tm, w), lambda b, t: (b, t, 0))
    return pl.pallas_call(
        functools.partial(_odd_pre_kernel, tm=tm, n_latent=n_latent, d=d),
        grid=(b_, t_ // tm),
        in_specs=[
            tok(d),
            pl.BlockSpec((1, 1, 3 * d), lambda b, t: (b, 0, 0)),
            _const_spec((1, 1, 3 * d)),
            _const_spec((1, d)),
            _const_spec(w_in.shape),
        ],
        out_specs=[tok(NA_WIDTH)] * 4,
        out_shape=[
            jax.ShapeDtypeStruct((b_, t_, NA_WIDTH), BF16),
            jax.ShapeDtypeStruct((b_, t_, NA_WIDTH), BF16),
            jax.ShapeDtypeStruct((b_, t_, NA_WIDTH), BF16),
            jax.ShapeDtypeStruct((b_, t_, NA_WIDTH), F32),
        ],
        compiler_params=_cparams(("parallel", "parallel")),
        name="odd_pre",
    )(xc, modb, modc, g, w_in)


def _na_kernel(q_ref, k_ref, v_ref, bias_ref, o_ref, *, nq, nkl, n_latent, n_ctx, n_blocks, rows):
    blk = pl.program_id(2)
    k_row0 = jnp.clip(blk * NA_QROWS - NA_WIN_R // 2, 0, rows - NA_KROWS)
    kstart = pl.multiple_of(k_row0 * GRID_W, GRID_W * NA_QROWS)
    q2 = q_ref[0]
    lane = lax.broadcasted_iota(jnp.int32, (nq, LANES), 1)
    k_loc = k_ref[0, pl.ds(kstart, nkl), :]
    v_loc = v_ref[0, pl.ds(kstart, nkl), :]
    k_ctx = k_ref[0, pl.ds(n_latent, n_ctx), :]
    v_ctx = v_ref[0, pl.ds(n_latent, n_ctx), :]
    outs = []
    nt = (((1,), (1,)), ((), ()))
    for j in range(2):
        in_head = (lane < NA_HEAD_DIM) if j == 0 else (lane >= NA_HEAD_DIM)
        q = jnp.where(in_head, q2, jnp.zeros_like(q2))
        s_loc = lax.dot_general(q, k_loc, nt, preferred_element_type=F32) + bias_ref[0, 0, j]
        s_ctx = lax.dot_general(q, k_ctx, nt, preferred_element_type=F32)
        m = jnp.maximum(jnp.max(s_loc, axis=-1, keepdims=True), jnp.max(s_ctx, axis=-1, keepdims=True))
        p_loc = jnp.exp2(s_loc - m)
        p_ctx = jnp.exp2(s_ctx - m)
        l = jnp.sum(p_loc, axis=-1, keepdims=True) + jnp.sum(p_ctx, axis=-1, keepdims=True)
        o = (jnp.dot(p_loc.astype(BF16), v_loc, preferred_element_type=F32)
             + jnp.dot(p_ctx.astype(BF16), v_ctx, preferred_element_type=F32))
        outs.append(o / l)
    o_ref[0] = jnp.where(lane < NA_HEAD_DIM, outs[0], outs[1])


def _na_bias_tables(rpb, rows):
    nq = NA_QROWS * GRID_W
    nkl = NA_KROWS * GRID_W
    n_heads = rpb.shape[0]
    pad = GRID_W - NA_WIN_C
    rp = jnp.pad(rpb * LOG2E, ((0, 0), (0, 0), (pad, pad)))
    toep = jnp.stack([rp[:, :, GRID_W - 1 - qc:2 * GRID_W - 1 - qc] for qc in range(GRID_W)], axis=2)
    qc = np.arange(GRID_W)
    cs = np.clip(qc - NA_WIN_C // 2, 0, GRID_W - NA_WIN_C)
    valid_c = (qc[None, :] >= cs[:, None]) & (qc[None, :] < cs[:, None] + NA_WIN_C)
    toep = jnp.where(valid_c, toep, NEG_BIG)
    masked = jnp.full((n_heads, GRID_W, GRID_W), NEG_BIG, F32)
    tabs = []
    for r0 in (0, 2 * NA_QROWS, rows - NA_QROWS):
        ks = int(np.clip(r0 - NA_WIN_R // 2, 0, rows - NA_KROWS))
        q_rows = []
        for qr in range(NA_QROWS):
            r = r0 + qr
            rs = int(np.clip(r - NA_WIN_R // 2, 0, rows - NA_WIN_R))
            slabs = []
            for kr in range(NA_KROWS):
                kabs = ks + kr
                slabs.append(toep[:, kabs - r + NA_WIN_R - 1] if rs <= kabs < rs + NA_WIN_R else masked)
            q_rows.append(jnp.concatenate(slabs, axis=-1))
        tabs.append(jnp.concatenate(q_rows, axis=1))
    tabs.append(jnp.full((n_heads, nq, nkl), NEG_BIG, F32))
    t = jnp.stack(tabs, axis=1)
    return t.reshape(n_heads // 2, 2, 4, nq, nkl).transpose(0, 2, 1, 3, 4)


def _na_attention(q, k, v, bias, n_latent, need_ctx):
    b_, t_, _ = q.shape
    n_ctx = t_ - n_latent
    rows = n_latent // GRID_W
    nq = NA_QROWS * GRID_W
    nkl = NA_KROWS * GRID_W
    n_lat_blocks = rows // NA_QROWS
    n_blocks = n_lat_blocks + (1 if need_ctx else 0)
    n_rows_out = n_latent + (n_ctx if need_ctx else 0)

    def bias_idx(p, b, i):
        typ = jnp.where(i == 0, 0, jnp.where(i == n_lat_blocks - 1, 2, jnp.where(i >= n_lat_blocks, 3, 1)))
        return (p, typ, 0, 0, 0)

    return pl.pallas_call(
        functools.partial(_na_kernel, nq=nq, nkl=nkl, n_latent=n_latent, n_ctx=n_ctx,
                          n_blocks=n_blocks, rows=rows),
        grid=(NA_HEADS // 2, b_, n_blocks),
        in_specs=[
            pl.BlockSpec((1, nq, LANES), lambda p, b, i: (b, i, p)),
            pl.BlockSpec((1, t_, LANES), lambda p, b, i: (b, 0, p)),
            pl.BlockSpec((1, t_, LANES), lambda p, b, i: (b, 0, p)),
            pl.BlockSpec((1, 1, 2, nq, nkl), bias_idx),
        ],
        out_specs=pl.BlockSpec((1, nq, LANES), lambda p, b, i: (b, i, p)),
        out_shape=jax.ShapeDtypeStruct((b_, n_rows_out, NA_WIDTH), F32),
        compiler_params=_cparams(("parallel", "parallel", "arbitrary")),
        name="na_attn",
    )(q, k, v, bias)


def _rope_swap_index():
    half = MLA_ROPE // 4
    idx = np.arange(MLA_ROPE)
    return np.where((idx // half) % 2 == 0, idx + half, idx - half)


def _rope_tables(n_latent, n_ctx):
    n_freq = MLA_ROPE // 4
    inv = ROPE_THETA ** (-jnp.arange(n_freq, dtype=F32) / n_freq)
    t = jnp.arange(n_latent, dtype=jnp.int32)
    ang_r = (t // GRID_W).astype(F32)[:, None] * inv
    ang_c = (t % GRID_W).astype(F32)[:, None] * inv
    cr, sr, cc, sc = jnp.cos(ang_r), jnp.sin(ang_r), jnp.cos(ang_c), jnp.sin(ang_c)
    cos32 = jnp.concatenate([cr, cr, cc, cc], axis=-1)
    sin32 = jnp.concatenate([-sr, sr, -sc, sc], axis=-1)
    ones = jnp.ones((n_latent, MLA_NOPE), F32)
    zeros = jnp.zeros((n_latent, LANES - MLA_NOPE - MLA_ROPE), F32)
    cos_l = jnp.concatenate([ones, cos32, zeros], axis=-1)
    sin_l = jnp.concatenate([0 * ones, sin32, zeros], axis=-1)
    cos_c = jnp.concatenate([jnp.ones((n_ctx, MLA_NOPE + MLA_ROPE), F32),
                             jnp.zeros((n_ctx, LANES - MLA_NOPE - MLA_ROPE), F32)], axis=-1)
    sin_c = jnp.zeros((n_ctx, LANES), F32)
    return jnp.concatenate([cos_l, cos_c], axis=0), jnp.concatenate([sin_l, sin_c], axis=0)


def _even_weights(i, ev_w_in, mla_q_norm, mla_w_uq, mla_kv_norm, mla_w_ukv, lru_conv_w, lru_conv_b,
                  lru_wa, lru_ba, lru_wx, lru_bx, lru_lambda, ev_w_out):
    sw = _rope_swap_index()
    d = ev_w_in.shape[1]
    w_in = ev_w_in[i]
    o_cq, o_ckv, o_kr = MLA_Q_RANK, MLA_Q_RANK + MLA_KV_RANK, MLA_Q_RANK + MLA_KV_RANK + MLA_ROPE
    kr = w_in[:, o_ckv:o_kr]
    pad_lo = jnp.zeros((d, MLA_NOPE), F32)
    pad_hi = jnp.zeros((d, LANES - MLA_NOPE - MLA_ROPE), F32)
    w_in_ext = jnp.concatenate(
        [w_in[:, :o_ckv], pad_lo, kr, pad_hi, pad_lo, kr[:, sw], pad_hi, w_in[:, o_kr:]], axis=1)
    wq = mla_w_uq[i].reshape(MLA_Q_RANK, MLA_HEADS, MLA_NOPE + MLA_ROPE)
    zq = jnp.zeros((MLA_Q_RANK, MLA_HEADS, LANES - MLA_NOPE - MLA_ROPE), F32)
    wqa = jnp.concatenate([wq, zq], axis=-1).reshape(MLA_Q_RANK, MLA_HEADS * LANES)
    wqb = jnp.concatenate([jnp.zeros((MLA_Q_RANK, MLA_HEADS, MLA_NOPE), F32),
                           wq[:, :, MLA_NOPE:][:, :, sw], zq], axis=-1).reshape(MLA_Q_RANK, MLA_HEADS * LANES)
    wkv = mla_w_ukv[i].reshape(MLA_KV_RANK, MLA_HEADS, MLA_NOPE + MLA_V)
    wk = jnp.concatenate([wkv[:, :, :MLA_NOPE], jnp.zeros((MLA_KV_RANK, MLA_HEADS, LANES - MLA_NOPE), F32)],
                         axis=-1).reshape(MLA_KV_RANK, MLA_HEADS * LANES)
    wv = wkv[:, :, MLA_NOPE:].reshape(MLA_KV_RANK, MLA_WIDTH)
    ng = LRU_WIDTH // LANES
    per = LANES // LRU_BLOCK

    def blockdiag(w):
        w = w.reshape(2, ng, per, LRU_BLOCK, LRU_BLOCK)
        eye = jnp.eye(per, dtype=F32)
        return jnp.einsum('dgpkj,pq->dgpkqj', w, eye).reshape(2, ng, LANES, LANES)

    w_gate = jnp.concatenate([blockdiag(lru_wa[i]), blockdiag(lru_wx[i])], axis=-1).astype(BF16)
    b_gate = jnp.concatenate([lru_ba[i].reshape(2, ng, 1, LANES), lru_bx[i].reshape(2, ng, 1, LANES)], axis=-1)
    return dict(
        w_in=w_in_ext.astype(BF16), q_norm=mla_q_norm[i][None], wqa=wqa.astype(BF16), wqb=wqb.astype(BF16),
        kv_norm=mla_kv_norm[i][None], wk=wk.astype(BF16), wv=wv.astype(BF16),
        conv_w=lru_conv_w[i], conv_b=lru_conv_b[i][None], w_gate=w_gate, b_gate=b_gate,
        softplus=jax.nn.softplus(-lru_lambda[i]).reshape(2, ng, 1, LANES),
        w_out=ev_w_out[i].astype(BF16))


def kernel(x, c, ctx, c_ctx, ada_w, ada_b, norm_g, ev_w_in, mla_q_norm, mla_w_uq, mla_kv_norm, mla_w_ukv,
           lru_conv_w, lru_conv_b, lru_wa, lru_ba, lru_wx, lru_bx, lru_lambda, ev_w_out, od_w_in, na_rpb,
           od_w_out, final_norm_g):
    b_, s_, d = x.shape
    n_ctx = ctx.shape[1]
    depth = ada_w.shape[0]
    rows = s_ // GRID_W
    assert b_ <= 7 and s_ % GRID_W == 0 and rows % NA_QROWS == 0 and rows >= NA_KROWS
    assert (s_ + n_ctx) % ROW_TILE == 0 and s_ % FINAL_ROW_TILE == 0 and s_ % MLA_TQ == 0
    assert (s_ + n_ctx) % MLA_TK == 0 and s_ % n_ctx == 0 and n_ctx % LRU_CHUNK == 0 and s_ % LRU_CHUNK == 0
    assert n_ctx == NA_QROWS * GRID_W and depth % 2 == 0

    cond = jnp.zeros((8, d), F32).at[:b_].set(c).at[b_].set(c_ctx)
    mod = _adaln(cond, ada_w, ada_b)
    xc = jnp.concatenate([x, ctx], axis=1)
    cos_t, sin_t = _rope_tables(s_, n_ctx)

    for layer in range(depth):
        need_ctx = layer < depth - 1
        i = layer // 2
        modb = mod[layer, :b_][:, None, :]
        modc = mod[layer, b_][None, None, :]
        g = norm_g[layer][None]
        if layer % 2 == 0:
            wts = _even_weights(i, ev_w_in, mla_q_norm, mla_w_uq, mla_kv_norm, mla_w_ukv, lru_conv_w,
                                lru_conv_b, lru_wa, lru_ba, lru_wx, lru_bx, lru_lambda, ev_w_out)
            qc, kc, vx, gm, xl, gl = _even_pre(xc, modb, modc, g, wts, cos_t, sin_t, s_)
            o = _mla_attention(qc, kc, vx, s_)
            r = _lru(xl, wts, s_)
            xc = _even_post(xc, modb, modc, o, gm, r, gl, wts["w_out"], s_)
        else:
            q, k, v, gg = _odd_pre_call(xc, modb, modc, g, od_w_in[i].astype(BF16), s_)
            bias = _na_bias_tables(na_rpb[i], rows)
            o = _na_attention(q, k, v, bias, s_, need_ctx)
            xc = _odd_post(xc, modb, modc, o, gg, od_w_out[i].astype(BF16), final_norm_g[None], s_,
                           final=not need_ctx)
    return xc
```

```python
import functools
import math

import numpy as np
import jax
import jax.numpy as jnp
from jax import lax
from jax.experimental import pallas as pl
from jax.experimental.pallas import tpu as pltpu

F32 = jnp.float32
BF16 = jnp.bfloat16

GRID_W = 64
RMS_EPS = 1e-6
ROPE_THETA = 10000.0
MLA_HEADS = 8
MLA_NOPE = 64
MLA_ROPE = 32
MLA_V = 64
MLA_Q_RANK = 256
MLA_KV_RANK = 128
MLA_WIDTH = MLA_HEADS * MLA_V
MLA_SCALE = (MLA_NOPE + MLA_ROPE) ** -0.5
LRU_WIDTH = 512
LRU_BLOCKS = 8
LRU_BLOCK = LRU_WIDTH // LRU_BLOCKS
LRU_CONV = 4
LRU_C = 8.0
NA_HEADS = 16
NA_HEAD_DIM = 64
NA_WIDTH = NA_HEADS * NA_HEAD_DIM
NA_WIN_R = 8
NA_WIN_C = 16

LANES = 128
LOG2E = 1.4426950408889634
NEG_BIG = -1e30
VMEM_LIMIT = 56 * 1024 * 1024

ROW_TILE = 768
FINAL_ROW_TILE = 512
MLA_TQ = 1024
MLA_WQ = 512
MLA_TK = 768
MLA_UNROLL = 2
PROB_ROWS = 16
V_EXT_ROWS = LANES + 16
NA_QROWS = 4
NA_KROWS = 12
LRU_CHUNK = 256


def _cparams(sem, flags=None):
    return pltpu.CompilerParams(dimension_semantics=sem, vmem_limit_bytes=VMEM_LIMIT, flags=flags)


def _silu(v):
    return v * jax.nn.sigmoid(v)


def _adaln_kernel(cond_ref, w_ref, b_ref, o_ref):
    a = _silu(cond_ref[...])
    o_ref[0] = jnp.dot(a, w_ref[0], preferred_element_type=F32) + b_ref[0]


def _adaln(cond, ada_w, ada_b):
    depth, d, d3 = ada_w.shape
    tn = 1024
    return pl.pallas_call(
        _adaln_kernel,
        grid=(depth, d3 // tn),
        in_specs=[
            pl.BlockSpec((8, d), lambda l, n: (0, 0)),
            pl.BlockSpec((1, d, tn), lambda l, n: (l, 0, n)),
            pl.BlockSpec((1, 1, tn), lambda l, n: (l, 0, n)),
        ],
        out_specs=pl.BlockSpec((1, 8, tn), lambda l, n: (l, 0, n)),
        out_shape=jax.ShapeDtypeStruct((depth, 8, d3), F32),
        compiler_params=_cparams(("arbitrary", "arbitrary")),
        name="adaln",
    )(cond, ada_w, ada_b.reshape(depth, 1, d3))


def _modulated_norm(x, g, modb_ref, modc_ref, is_ctx, d):
    ms = jnp.mean(x * x, axis=-1, keepdims=True)
    y = x * lax.rsqrt(ms + RMS_EPS) * g
    shift = jnp.where(is_ctx, modc_ref[0, :, 0:d], modb_ref[0, :, 0:d])
    scale = jnp.where(is_ctx, modc_ref[0, :, d:2 * d], modb_ref[0, :, d:2 * d])
    return y * (1.0 + scale) + shift


def _row_is_ctx(tm, n_latent):
    row = pl.program_id(1) * tm + lax.broadcasted_iota(jnp.int32, (tm, 1), 0)
    return row >= n_latent


def _rms(v, g):
    ms = jnp.mean(v * v, axis=-1, keepdims=True)
    return v * lax.rsqrt(ms + RMS_EPS) * g


def _even_pre_kernel(x_ref, modb_ref, modc_ref, g_ref, win_ref, qn_ref, wqa_ref, wqb_ref,
                     kvn_ref, wk_ref, wv_ref, cos_ref, sin_ref,
                     q_out, k_out, v_out, gm_out, xl_out, gl_out, *, tm, n_latent, d):
    is_ctx = _row_is_ctx(tm, n_latent)
    h = _modulated_norm(x_ref[0], g_ref[...], modb_ref, modc_ref, is_ctx, d).astype(BF16)

    def proj(lo, hi):
        return jnp.dot(h, win_ref[:, lo:hi], preferred_element_type=F32)

    cos = cos_ref[...]
    sin = sin_ref[...]
    cq = _rms(proj(0, 256), qn_ref[...]).astype(BF16)
    qa = jnp.dot(cq, wqa_ref[...], preferred_element_type=F32)
    qb = jnp.dot(cq, wqb_ref[...], preferred_element_type=F32)
    ckv = _rms(proj(256, 384), kvn_ref[...]).astype(BF16)
    kk = jnp.dot(ckv, wk_ref[...], preferred_element_type=F32)
    k_rope = proj(384, 512) * cos + proj(512, 640) * sin
    v_t = jnp.dot(ckv, wv_ref[...], preferred_element_type=F32).T
    ones_rows = jnp.ones((V_EXT_ROWS - LANES, tm), BF16)
    for p in range(MLA_HEADS // 2):
        v_out[0, p, 0, 0:LANES, :] = v_t[p * LANES:(p + 1) * LANES].astype(BF16)
        v_out[0, p, 0, LANES:V_EXT_ROWS, :] = ones_rows
    for hh in range(MLA_HEADS):
        sl = slice(hh * LANES, (hh + 1) * LANES)
        q = (qa[:, sl] * cos + qb[:, sl] * sin) * (MLA_SCALE * LOG2E)
        q_out[0, sl, :] = q.T.astype(BF16)
        k_out[0, :, sl] = (kk[:, sl] + k_rope).astype(BF16)
    gm_out[0] = _silu(proj(640, 1152))
    xl_out[0] = proj(1152, 1664)
    gl_out[0] = _silu(proj(1664, 2176))


def _const_spec(shape):
    nd = len(shape)
    return pl.BlockSpec(shape, lambda b, t: (0,) * nd)


def _even_pre(xc, modb, modc, g, wts, cos_t, sin_t, n_latent):
    b_, t_, d = xc.shape
    tm = ROW_TILE
    kern = functools.partial(_even_pre_kernel, tm=tm, n_latent=n_latent, d=d)
    tok = lambda w: pl.BlockSpec((1, tm, w), lambda b, t: (b, t, 0))
    hw = MLA_HEADS * LANES
    return pl.pallas_call(
        kern,
        grid=(b_, t_ // tm),
        in_specs=[
            tok(d),
            pl.BlockSpec((1, 1, 3 * d), lambda b, t: (b, 0, 0)),
            _const_spec((1, 1, 3 * d)),
            _const_spec((1, d)),
            _const_spec(wts["w_in"].shape),
            _const_spec((1, MLA_Q_RANK)),
            _const_spec(wts["wqa"].shape),
            _const_spec(wts["wqb"].shape),
            _const_spec((1, MLA_KV_RANK)),
            _const_spec(wts["wk"].shape),
            _const_spec(wts["wv"].shape),
            pl.BlockSpec((tm, LANES), lambda b, t: (t, 0)),
            pl.BlockSpec((tm, LANES), lambda b, t: (t, 0)),
        ],
        out_specs=[pl.BlockSpec((1, hw, tm), lambda b, t: (b, 0, t)), tok(hw),
                   pl.BlockSpec((1, MLA_HEADS // 2, 1, V_EXT_ROWS, tm), lambda b, t: (b, 0, t, 0, 0)),
                   tok(MLA_WIDTH), tok(LRU_WIDTH), tok(LRU_WIDTH)],
        out_shape=[
            jax.ShapeDtypeStruct((b_, hw, t_), BF16),
            jax.ShapeDtypeStruct((b_, t_, hw), BF16),
            jax.ShapeDtypeStruct((b_, MLA_HEADS // 2, t_ // tm, V_EXT_ROWS, tm), BF16),
            jax.ShapeDtypeStruct((b_, t_, MLA_WIDTH), F32),
            jax.ShapeDtypeStruct((b_, t_, LRU_WIDTH), F32),
            jax.ShapeDtypeStruct((b_, t_, LRU_WIDTH), F32),
        ],
        compiler_params=_cparams(("parallel", "parallel")),
        name="even_pre",
    )(xc, modb, modc, g, wts["w_in"], wts["q_norm"], wts["wqa"], wts["wqb"], wts["kv_norm"],
      wts["wk"], wts["wv"], cos_t, sin_t)


def _mla_attn_t_kernel(q_ref, k_ref, v_ref, o_ref, *scratch, tq, wq, tk, nk, v_lane0):
    nh = tq // wq
    n_sub = 2 * nh
    n_total = n_sub * nk
    s_buf, p_buf = scratch[0:n_sub], scratch[n_sub:n_sub + 2]
    acc_ref, m_ref, al_ref = (scratch[n_sub + 2 + r * n_sub:n_sub + 2 + (r + 1) * n_sub] for r in range(3))

    def score(c, i):
        j, h = divmod(i, nh)
        start = c * tk if isinstance(c, int) else pl.multiple_of(c * tk, LANES)
        k = k_ref[0, pl.ds(start, tk), j * LANES:(j + 1) * LANES]
        q_t = q_ref[0, j * LANES:(j + 1) * LANES, h * wq:(h + 1) * wq]
        s = jnp.dot(k, q_t, preferred_element_type=F32)
        m = m_ref[i][...]
        m_new = jnp.maximum(m, jnp.max(s, axis=0, keepdims=True))
        al_ref[i][...] = jnp.exp2(m - m_new)
        m_ref[i][...] = m_new
        s_buf[i][...] = s

    def prob(c, i):
        m = jnp.broadcast_to(m_ref[i][...], (PROB_ROWS, wq))
        for r in range(0, tk, PROB_ROWS):
            p = jnp.exp2(s_buf[i][r:r + PROB_ROWS, :] - m)
            p_buf[i % 2][r:r + PROB_ROWS, :] = p.astype(BF16)

    def value(c, i):
        v_t = v_ref[0, 0, c, :, v_lane0:v_lane0 + tk]
        pv = jnp.dot(v_t, p_buf[i % 2][...], preferred_element_type=F32)
        acc_ref[i][...] = al_ref[i][...] * acc_ref[i][...] + pv

    for i in range(n_sub):
        m_ref[i][...] = jnp.full(m_ref[i].shape, -jnp.inf, F32)
        acc_ref[i][...] = jnp.zeros(acc_ref[i].shape, F32)

    def group(c, r):
        for stage, lag in ((value, 3), (prob, 2), (score, 0)):
            dc, i = divmod(r - lag, n_sub)
            if isinstance(c, int) and not 0 <= (c + dc) * n_sub + i < n_total:
                continue
            stage(c + dc, i)

    for g in range(3):
        group(0, g)

    def body(it, _):
        for r in range(3, 3 + MLA_UNROLL * n_sub):
            group(it * MLA_UNROLL, r)
        return 0

    trips = (nk - 1) // MLA_UNROLL
    lax.fori_loop(0, trips, body, 0)
    for g in range(n_sub * MLA_UNROLL * trips + 3, n_total + 3):
        group(0, g)

    for h in range(nh):
        a0 = acc_ref[h][...]
        a1 = acc_ref[nh + h][...]
        o0 = a0[0:MLA_V] / a0[LANES:LANES + 1]
        o1 = a1[MLA_V:LANES] / a1[LANES:LANES + 1]
        o_ref[0, h * wq:(h + 1) * wq, :] = jnp.concatenate([o0, o1], axis=0).T


def _mla_scratch(tq, wq, tk):
    n_sub = 2 * (tq // wq)
    return ([pltpu.VMEM((tk, wq), F32)] * n_sub + [pltpu.VMEM((tk, wq), BF16)] * 2
            + [pltpu.VMEM((V_EXT_ROWS, wq), F32)] * n_sub + [pltpu.VMEM((1, wq), F32)] * (2 * n_sub))


def _mla_attention(q_t, kc, v_t, n_latent):
    b_, t_, _ = kc.shape
    n_pairs = MLA_HEADS // 2
    tq, wq, tk = MLA_TQ, MLA_WQ, MLA_TK
    n_ctx = t_ - n_latent
    nk = t_ // tk
    o_lat = pl.pallas_call(
        functools.partial(_mla_attn_t_kernel, tq=tq, wq=wq, tk=tk, nk=nk, v_lane0=0),
        grid=(b_, n_pairs, n_latent // tq),
        in_specs=[
            pl.BlockSpec((1, 2 * LANES, tq), lambda b, p, i: (b, p, i)),
            pl.BlockSpec((1, t_, 2 * LANES), lambda b, p, i: (b, 0, p)),
            pl.BlockSpec((1, 1, nk, V_EXT_ROWS, tk), lambda b, p, i: (b, p, 0, 0, 0)),
        ],
        out_specs=pl.BlockSpec((1, tq, LANES), lambda b, p, i: (b, i, p)),
        out_shape=jax.ShapeDtypeStruct((b_, n_latent, MLA_WIDTH), F32),
        scratch_shapes=_mla_scratch(tq, wq, tk),
        compiler_params=_cparams(("parallel", "parallel", "arbitrary")),
        name="mla_attn_latent",
    )(q_t, kc, v_t)
    cb = n_latent // n_ctx
    o_ctx = pl.pallas_call(
        functools.partial(_mla_attn_t_kernel, tq=n_ctx, wq=n_ctx, tk=n_ctx, nk=1, v_lane0=tk - n_ctx),
        grid=(b_, n_pairs),
        in_specs=[
            pl.BlockSpec((1, 2 * LANES, n_ctx), lambda b, p: (b, p, cb)),
            pl.BlockSpec((1, n_ctx, 2 * LANES), lambda b, p: (b, cb, p)),
            pl.BlockSpec((1, 1, 1, V_EXT_ROWS, tk), lambda b, p: (b, p, nk - 1, 0, 0)),
        ],
        out_specs=pl.BlockSpec((1, n_ctx, LANES), lambda b, p: (b, 0, p)),
        out_shape=jax.ShapeDtypeStruct((b_, n_ctx, MLA_WIDTH), F32),
        scratch_shapes=_mla_scratch(n_ctx, n_ctx, n_ctx),
        compiler_params=_cparams(("parallel", "parallel")),
        name="mla_attn_ctx",
    )(q_t, kc, v_t)
    return jnp.concatenate([o_lat, o_ctx], axis=1)


def _lru_kernel(x_ref, cw_ref, cb_ref, wg_ref, bg_ref, sp_ref, r_ref, hf_ref, hb_ref,
                *, tc, n_latent, n_ctx):
    nl = n_latent // tc
    ncx = n_ctx // tc
    row = lax.broadcasted_iota(jnp.int32, (tc, 1), 0)

    def conv_chunk(c0, seg_lo, seg_hi):
        xc = x_ref[0, pl.ds(c0, tc), :]
        has_prev = c0 > seg_lo
        has_next = c0 + tc < seg_hi
        p0 = pl.multiple_of(jnp.where(has_prev, c0 - 8, c0), 8)
        n0 = pl.multiple_of(jnp.where(has_next, c0 + tc, c0), 8)
        prev = jnp.where(has_prev, x_ref[0, pl.ds(p0, 8), :], 0.0)
        nxt = jnp.where(has_next, x_ref[0, pl.ds(n0, 8), :], 0.0)
        xm1 = jnp.where(row >= 1, pltpu.roll(xc, 1, 0), prev[7:8])
        xm2 = jnp.where(row >= 2, pltpu.roll(xc, 2, 0), jnp.where(row == 1, prev[7:8], prev[6:7]))
        xp1 = jnp.where(row <= tc - 2, pltpu.roll(xc, tc - 1, 0), nxt[0:1])
        return (cw_ref[0:1] * xm2 + cw_ref[1:2] * xm1 + cw_ref[2:3] * xc + cw_ref[3:4] * xp1
                + cb_ref[...])

    def coeffs(u, d):
        z = jnp.dot(u.astype(BF16), wg_ref[d], preferred_element_type=F32) + bg_ref[d]
        r = jax.nn.sigmoid(z[:, :LANES])
        i = jax.nn.sigmoid(z[:, LANES:])
        log_a = -LRU_C * r * sp_ref[d]
        a = jnp.exp(log_a)
        t = jnp.tanh(log_a)
        uu = jnp.sqrt(-2.0 * t / (1.0 - t)) * (i * u)
        return a, uu

    def scan_chunk(a, u, h_in, reverse):
        s = 1
        while s < tc:
            if reverse:
                keep = row < tc - s
                a_s = jnp.where(keep, pltpu.roll(a, tc - s, 0), 1.0)
                u_s = jnp.where(keep, pltpu.roll(u, tc - s, 0), 0.0)
            else:
                keep = row >= s
                a_s = jnp.where(keep, pltpu.roll(a, s, 0), 1.0)
                u_s = jnp.where(keep, pltpu.roll(u, s, 0), 0.0)
            u = a * u_s + u
            a = a * a_s
            s *= 2
        return a * h_in + u

    def chunk_start(i):
        return pl.multiple_of(jnp.where(i < ncx, n_latent + i * tc, (i - ncx) * tc), tc)

    def seg_bounds(i):
        lo = jnp.where(i < ncx, n_latent, 0)
        hi = jnp.where(i < ncx, n_latent + n_ctx, n_latent)
        return lo, hi

    hf_ref[...] = jnp.zeros_like(hf_ref)
    hb_ref[...] = jnp.zeros_like(hb_ref)

    def fwd(i, _):
        c0 = chunk_start(i)
        lo, hi = seg_bounds(i)
        a, uu = coeffs(conv_chunk(c0, lo, hi), 0)
        h = scan_chunk(a, uu, hf_ref[0:1], False)
        hf_ref[...] = jnp.broadcast_to(h[tc - 1:tc], hf_ref.shape)
        r_ref[0, pl.ds(c0, tc), :] = h
        return 0

    lax.fori_loop(0, ncx + nl, fwd, 0)

    def bwd(i, _):
        ii = jnp.where(i < ncx, ncx - 1 - i, ncx + (nl - 1 - (i - ncx)))
        c0 = chunk_start(ii)
        lo, hi = seg_bounds(ii)
        a, uu = coeffs(conv_chunk(c0, lo, hi), 1)
        h = scan_chunk(a, uu, hb_ref[0:1], True)
        hb_ref[...] = jnp.broadcast_to(h[0:1], hb_ref.shape)
        r_ref[0, pl.ds(c0, tc), :] = r_ref[0, pl.ds(c0, tc), :] + h
        return 0

    lax.fori_loop(0, ncx + nl, bwd, 0)


def _lru(x_lru, wts, n_latent):
    b_, t_, w = x_lru.shape
    ng = w // LANES
    kern = functools.partial(_lru_kernel, tc=LRU_CHUNK, n_latent=n_latent, n_ctx=t_ - n_latent)
    return pl.pallas_call(
        kern,
        grid=(b_, ng),
        in_specs=[
            pl.BlockSpec((1, t_, LANES), lambda b, g: (b, 0, g)),
            pl.BlockSpec((LRU_CONV, LANES), lambda b, g: (0, g)),
            pl.BlockSpec((1, LANES), lambda b, g: (0, g)),
            pl.BlockSpec((2, None, LANES, 2 * LANES), lambda b, g: (0, g, 0, 0)),
            pl.BlockSpec((2, None, 1, 2 * LANES), lambda b, g: (0, g, 0, 0)),
            pl.BlockSpec((2, None, 1, LANES), lambda b, g: (0, g, 0, 0)),
        ],
        out_specs=pl.BlockSpec((1, t_, LANES), lambda b, g: (b, 0, g)),
        out_shape=jax.ShapeDtypeStruct((b_, t_, w), F32),
        scratch_shapes=[pltpu.VMEM((8, LANES), F32), pltpu.VMEM((8, LANES), F32)],
        compiler_params=_cparams(("parallel", "parallel")),
        name="rglru",
    )(x_lru, wts["conv_w"], wts["conv_b"], wts["w_gate"], wts["b_gate"], wts["softplus"])


def _residual(x_ref, y, modb_ref, modc_ref, is_ctx, d):
    gate = jnp.where(is_ctx, modc_ref[0, :, 2 * d:3 * d], modb_ref[0, :, 2 * d:3 * d])
    return x_ref[0] + gate * y


def _even_post_kernel(x_ref, modb_ref, modc_ref, o_ref, gm_ref, r_ref, gl_ref, wo_ref, out_ref,
                      *, tm, n_latent, d):
    is_ctx = _row_is_ctx(tm, n_latent)
    a = (o_ref[0] * gm_ref[0]).astype(BF16)
    bb = (r_ref[0] * gl_ref[0]).astype(BF16)
    y = (jnp.dot(a, wo_ref[0:MLA_WIDTH, :], preferred_element_type=F32)
         + jnp.dot(bb, wo_ref[MLA_WIDTH:, :], preferred_element_type=F32))
    out_ref[0] = _residual(x_ref, y, modb_ref, modc_ref, is_ctx, d)


def _even_post(xc, modb, modc, o, gm, r, gl, w_out, n_latent):
    b_, t_, d = xc.shape
    tm = ROW_TILE
    tok = lambda w: pl.BlockSpec((1, tm, w), lambda b, t: (b, t, 0))
    return pl.pallas_call(
        functools.partial(_even_post_kernel, tm=tm, n_latent=n_latent, d=d),
        grid=(b_, t_ // tm),
        in_specs=[
            tok(d),
            pl.BlockSpec((1, 1, 3 * d), lambda b, t: (b, 0, 0)),
            _const_spec((1, 1, 3 * d)),
            tok(MLA_WIDTH), tok(MLA_WIDTH), tok(LRU_WIDTH), tok(LRU_WIDTH),
            _const_spec(w_out.shape),
        ],
        out_specs=tok(d),
        out_shape=jax.ShapeDtypeStruct((b_, t_, d), F32),
        compiler_params=_cparams(("parallel", "parallel")),
        name="even_post",
    )(xc, modb, modc, o, gm, r, gl, w_out)


def _odd_post_kernel(x_ref, modb_ref, modc_ref, o_ref, g_ref, wo_ref, fg_ref, out_ref,
                     *, tm, n_latent, d, final):
    is_ctx = _row_is_ctx(tm, n_latent)
    a = (o_ref[0] * g_ref[0]).astype(BF16)
    y = jnp.dot(a, wo_ref[...], preferred_element_type=F32)
    xn = _residual(x_ref, y, modb_ref, modc_ref, is_ctx, d)
    if final:
        xn = _rms(xn, fg_ref[...])
    out_ref[0] = xn


def _odd_post(xc, modb, modc, o, g, w_out, final_g, n_latent, final):
    b_, t_, d = xc.shape
    tm = FINAL_ROW_TILE if final else ROW_TILE
    n_rows = n_latent if final else t_
    tok = lambda w: pl.BlockSpec((1, tm, w), lambda b, t: (b, t, 0))
    return pl.pallas_call(
        functools.partial(_odd_post_kernel, tm=tm, n_latent=n_latent, d=d, final=final),
        grid=(b_, n_rows // tm),
        in_specs=[
            tok(d),
            pl.BlockSpec((1, 1, 3 * d), lambda b, t: (b, 0, 0)),
            _const_spec((1, 1, 3 * d)),
            tok(NA_WIDTH), tok(NA_WIDTH),
            _const_spec(w_out.shape),
            _const_spec((1, d)),
        ],
        out_specs=tok(d),
        out_shape=jax.ShapeDtypeStruct((b_, n_rows, d), F32),
        compiler_params=_cparams(("parallel", "parallel")),
        name="odd_post_final" if final else "odd_post",
    )(xc, modb, modc, o, g, w_out, final_g)


def _odd_pre_kernel(x_ref, modb_ref, modc_ref, g_ref, win_ref, q_out, k_out, v_out, g_out,
                    *, tm, n_latent, d):
    is_ctx = _row_is_ctx(tm, n_latent)
    h = _modulated_norm(x_ref[0], g_ref[...], modb_ref, modc_ref, is_ctx, d).astype(BF16)
    w = NA_WIDTH
    q = jnp.dot(h, win_ref[:, 0:w], preferred_element_type=F32)
    q_out[0] = (q * (NA_HEAD_DIM ** -0.5 * LOG2E)).astype(BF16)
    k_out[0] = jnp.dot(h, win_ref[:, w:2 * w], preferred_element_type=F32).astype(BF16)
    v_out[0] = jnp.dot(h, win_ref[:, 2 * w:3 * w], preferred_element_type=F32).astype(BF16)
    g_out[0] = _silu(jnp.dot(h, win_ref[:, 3 * w:4 * w], preferred_element_type=F32))


def _odd_pre_call(xc, modb, modc, g, w_in, n_latent):
    b_, t_, d = xc.shape
    tm = ROW_TILE
    tok = lambda w: pl.BlockSpec((1, tm, w), lambda b, t: (b, t, 0))
    return pl.pallas_call(
        functools.partial(_odd_pre_kernel, tm=tm, n_latent=n_latent, d=d),
        grid=(b_, t_ // tm),
        in_specs=[
            tok(d),
            pl.BlockSpec((1, 1, 3 * d), lambda b, t: (b, 0, 0)),
            _const_spec((1, 1, 3 * d)),
            _const_spec((1, d)),
            _const_spec(w_in.shape),
        ],
        out_specs=[tok(NA_WIDTH)] * 4,
        out_shape=[
            jax.ShapeDtypeStruct((b_, t_, NA_WIDTH), BF16),
            jax.ShapeDtypeStruct((b_, t_, NA_WIDTH), BF16),
            jax.ShapeDtypeStruct((b_, t_, NA_WIDTH), BF16),
            jax.ShapeDtypeStruct((b_, t_, NA_WIDTH), F32),
        ],
        compiler_params=_cparams(("parallel", "parallel")),
        name="odd_pre",
    )(xc, modb, modc, g, w_in)


def _na_kernel(q_ref, k_ref, v_ref, bias_ref, o_ref, *, nq, nkl, n_latent, n_ctx, n_blocks, rows):
    blk = pl.program_id(2)
    k_row0 = jnp.clip(blk * NA_QROWS - NA_WIN_R // 2, 0, rows - NA_KROWS)
    kstart = pl.multiple_of(k_row0 * GRID_W, GRID_W * NA_QROWS)
    q2 = q_ref[0]
    lane = lax.broadcasted_iota(jnp.int32, (nq, LANES), 1)
    k_loc = k_ref[0, pl.ds(kstart, nkl), :]
    v_loc = v_ref[0, pl.ds(kstart, nkl), :]
    k_ctx = k_ref[0, pl.ds(n_latent, n_ctx), :]
    v_ctx = v_ref[0, pl.ds(n_latent, n_ctx), :]
    outs = []
    nt = (((1,), (1,)), ((), ()))
    for j in range(2):
        in_head = (lane < NA_HEAD_DIM) if j == 0 else (lane >= NA_HEAD_DIM)
        q = jnp.where(in_head, q2, jnp.zeros_like(q2))
        s_loc = lax.dot_general(q, k_loc, nt, preferred_element_type=F32) + bias_ref[0, 0, j]
        s_ctx = lax.dot_general(q, k_ctx, nt, preferred_element_type=F32)
        m = jnp.maximum(jnp.max(s_loc, axis=-1, keepdims=True), jnp.max(s_ctx, axis=-1, keepdims=True))
        p_loc = jnp.exp2(s_loc - m)
        p_ctx = jnp.exp2(s_ctx - m)
        l = jnp.sum(p_loc, axis=-1, keepdims=True) + jnp.sum(p_ctx, axis=-1, keepdims=True)
        o = (jnp.dot(p_loc.astype(BF16), v_loc, preferred_element_type=F32)
             + jnp.dot(p_ctx.astype(BF16), v_ctx, preferred_element_type=F32))
        outs.append(o / l)
    o_ref[0] = jnp.where(lane < NA_HEAD_DIM, outs[0], outs[1])


def _na_bias_tables(rpb, rows):
    nq = NA_QROWS * GRID_W
    nkl = NA_KROWS * GRID_W
    n_heads = rpb.shape[0]
    pad = GRID_W - NA_WIN_C
    rp = jnp.pad(rpb * LOG2E, ((0, 0), (0, 0), (pad, pad)))
    toep = jnp.stack([rp[:, :, GRID_W - 1 - qc:2 * GRID_W - 1 - qc] for qc in range(GRID_W)], axis=2)
    qc = np.arange(GRID_W)
    cs = np.clip(qc - NA_WIN_C // 2, 0, GRID_W - NA_WIN_C)
    valid_c = (qc[None, :] >= cs[:, None]) & (qc[None, :] < cs[:, None] + NA_WIN_C)
    toep = jnp.where(valid_c, toep, NEG_BIG)
    masked = jnp.full((n_heads, GRID_W, GRID_W), NEG_BIG, F32)
    tabs = []
    for r0 in (0, 2 * NA_QROWS, rows - NA_QROWS):
        ks = int(np.clip(r0 - NA_WIN_R // 2, 0, rows - NA_KROWS))
        q_rows = []
        for qr in range(NA_QROWS):
            r = r0 + qr
            rs = int(np.clip(r - NA_WIN_R // 2, 0, rows - NA_WIN_R))
            slabs = []
            for kr in range(NA_KROWS):
                kabs = ks + kr
                slabs.append(toep[:, kabs - r + NA_WIN_R - 1] if rs <= kabs < rs + NA_WIN_R else masked)
            q_rows.append(jnp.concatenate(slabs, axis=-1))
        tabs.append(jnp.concatenate(q_rows, axis=1))
    tabs.append(jnp.full((n_heads, nq, nkl), NEG_BIG, F32))
    t = jnp.stack(tabs, axis=1)
    return t.reshape(n_heads // 2, 2, 4, nq, nkl).transpose(0, 2, 1, 3, 4)


def _na_attention(q, k, v, bias, n_latent, need_ctx):
    b_, t_, _ = q.shape
    n_ctx = t_ - n_latent
    rows = n_latent // GRID_W
    nq = NA_QROWS * GRID_W
    nkl = NA_KROWS * GRID_W
    n_lat_blocks = rows // NA_QROWS
    n_blocks = n_lat_blocks + (1 if need_ctx else 0)
    n_rows_out = n_latent + (n_ctx if need_ctx else 0)

    def bias_idx(p, b, i):
        typ = jnp.where(i == 0, 0, jnp.where(i == n_lat_blocks - 1, 2, jnp.where(i >= n_lat_blocks, 3, 1)))
        return (p, typ, 0, 0, 0)

    return pl.pallas_call(
        functools.partial(_na_kernel, nq=nq, nkl=nkl, n_latent=n_latent, n_ctx=n_ctx,
                          n_blocks=n_blocks, rows=rows),
        grid=(NA_HEADS // 2, b_, n_blocks),
        in_specs=[
            pl.BlockSpec((1, nq, LANES), lambda p, b, i: (b, i, p)),
            pl.BlockSpec((1, t_, LANES), lambda p, b, i: (b, 0, p)),
            pl.BlockSpec((1, t_, LANES), lambda p, b, i: (b, 0, p)),
            pl.BlockSpec((1, 1, 2, nq, nkl), bias_idx),
        ],
        out_specs=pl.BlockSpec((1, nq, LANES), lambda p, b, i: (b, i, p)),
        out_shape=jax.ShapeDtypeStruct((b_, n_rows_out, NA_WIDTH), F32),
        compiler_params=_cparams(("parallel", "parallel", "arbitrary")),
        name="na_attn",
    )(q, k, v, bias)


def _rope_swap_index():
    half = MLA_ROPE // 4
    idx = np.arange(MLA_ROPE)
    return np.where((idx // half) % 2 == 0, idx + half, idx - half)


def _rope_tables(n_latent, n_ctx):
    n_freq = MLA_ROPE // 4
    inv = ROPE_THETA ** (-jnp.arange(n_freq, dtype=F32) / n_freq)
    t = jnp.arange(n_latent, dtype=jnp.int32)
    ang_r = (t // GRID_W).astype(F32)[:, None] * inv
    ang_c = (t % GRID_W).astype(F32)[:, None] * inv
    cr, sr, cc, sc = jnp.cos(ang_r), jnp.sin(ang_r), jnp.cos(ang_c), jnp.sin(ang_c)
    cos32 = jnp.concatenate([cr, cr, cc, cc], axis=-1)
    sin32 = jnp.concatenate([-sr, sr, -sc, sc], axis=-1)
    ones = jnp.ones((n_latent, MLA_NOPE), F32)
    zeros = jnp.zeros((n_latent, LANES - MLA_NOPE - MLA_ROPE), F32)
    cos_l = jnp.concatenate([ones, cos32, zeros], axis=-1)
    sin_l = jnp.concatenate([0 * ones, sin32, zeros], axis=-1)
    cos_c = jnp.concatenate([jnp.ones((n_ctx, MLA_NOPE + MLA_ROPE), F32),
                             jnp.zeros((n_ctx, LANES - MLA_NOPE - MLA_ROPE), F32)], axis=-1)
    sin_c = jnp.zeros((n_ctx, LANES), F32)
    return jnp.concatenate([cos_l, cos_c], axis=0), jnp.concatenate([sin_l, sin_c], axis=0)


def _even_weights(i, ev_w_in, mla_q_norm, mla_w_uq, mla_kv_norm, mla_w_ukv, lru_conv_w, lru_conv_b,
                  lru_wa, lru_ba, lru_wx, lru_bx, lru_lambda, ev_w_out):
    sw = _rope_swap_index()
    d = ev_w_in.shape[1]
    w_in = ev_w_in[i]
    o_cq, o_ckv, o_kr = MLA_Q_RANK, MLA_Q_RANK + MLA_KV_RANK, MLA_Q_RANK + MLA_KV_RANK + MLA_ROPE
    kr = w_in[:, o_ckv:o_kr]
    pad_lo = jnp.zeros((d, MLA_NOPE), F32)
    pad_hi = jnp.zeros((d, LANES - MLA_NOPE - MLA_ROPE), F32)
    w_in_ext = jnp.concatenate(
        [w_in[:, :o_ckv], pad_lo, kr, pad_hi, pad_lo, kr[:, sw], pad_hi, w_in[:, o_kr:]], axis=1)
    wq = mla_w_uq[i].reshape(MLA_Q_RANK, MLA_HEADS, MLA_NOPE + MLA_ROPE)
    zq = jnp.zeros((MLA_Q_RANK, MLA_HEADS, LANES - MLA_NOPE - MLA_ROPE), F32)
    wqa = jnp.concatenate([wq, zq], axis=-1).reshape(MLA_Q_RANK, MLA_HEADS * LANES)
    wqb = jnp.concatenate([jnp.zeros((MLA_Q_RANK, MLA_HEADS, MLA_NOPE), F32),
                           wq[:, :, MLA_NOPE:][:, :, sw], zq], axis=-1).reshape(MLA_Q_RANK, MLA_HEADS * LANES)
    wkv = mla_w_ukv[i].reshape(MLA_KV_RANK, MLA_HEADS, MLA_NOPE + MLA_V)
    wk = jnp.concatenate([wkv[:, :, :MLA_NOPE], jnp.zeros((MLA_KV_RANK, MLA_HEADS, LANES - MLA_NOPE), F32)],
                         axis=-1).reshape(MLA_KV_RANK, MLA_HEADS * LANES)
    wv = wkv[:, :, MLA_NOPE:].reshape(MLA_KV_RANK, MLA_WIDTH)
    ng = LRU_WIDTH // LANES
    per = LANES // LRU_BLOCK

    def blockdiag(w):
        w = w.reshape(2, ng, per, LRU_BLOCK, LRU_BLOCK)
        eye = jnp.eye(per, dtype=F32)
        return jnp.einsum('dgpkj,pq->dgpkqj', w, eye).reshape(2, ng, LANES, LANES)

    w_gate = jnp.concatenate([blockdiag(lru_wa[i]), blockdiag(lru_wx[i])], axis=-1).astype(BF16)
    b_gate = jnp.concatenate([lru_ba[i].reshape(2, ng, 1, LANES), lru_bx[i].reshape(2, ng, 1, LANES)], axis=-1)
    return dict(
        w_in=w_in_ext.astype(BF16), q_norm=mla_q_norm[i][None], wqa=wqa.astype(BF16), wqb=wqb.astype(BF16),
        kv_norm=mla_kv_norm[i][None], wk=wk.astype(BF16), wv=wv.astype(BF16),
        conv_w=lru_conv_w[i], conv_b=lru_conv_b[i][None], w_gate=w_gate, b_gate=b_gate,
        softplus=jax.nn.softplus(-lru_lambda[i]).reshape(2, ng, 1, LANES),
        w_out=ev_w_out[i].astype(BF16))


def kernel(x, c, ctx, c_ctx, ada_w, ada_b, norm_g, ev_w_in, mla_q_norm, mla_w_uq, mla_kv_norm, mla_w_ukv,
           lru_conv_w, lru_conv_b, lru_wa, lru_ba, lru_wx, lru_bx, lru_lambda, ev_w_out, od_w_in, na_rpb,
           od_w_out, final_norm_g):
    b_, s_, d = x.shape
    n_ctx = ctx.shape[1]
    depth = ada_w.shape[0]
    rows = s_ // GRID_W
    assert b_ <= 7 and s_ % GRID_W == 0 and rows % NA_QROWS == 0 and rows >= NA_KROWS
    assert (s_ + n_ctx) % ROW_TILE == 0 and s_ % FINAL_ROW_TILE == 0 and s_ % MLA_TQ == 0
    assert ROW_TILE == MLA_TK and n_ctx <= MLA_TK and s_ % n_ctx == 0 and n_ctx % LRU_CHUNK == 0 and s_ % LRU_CHUNK == 0
    assert n_ctx == NA_QROWS * GRID_W and depth % 2 == 0

    cond = jnp.zeros((8, d), F32).at[:b_].set(c).at[b_].set(c_ctx)
    mod = _adaln(cond, ada_w, ada_b)
    xc = jnp.concatenate([x, ctx], axis=1)
    cos_t, sin_t = _rope_tables(s_, n_ctx)

    for layer in range(depth):
        need_ctx = layer < depth - 1
        i = layer // 2
        modb = mod[layer, :b_][:, None, :]
        modc = mod[layer, b_][None, None, :]
        g = norm_g[layer][None]
        if layer % 2 == 0:
            wts = _even_weights(i, ev_w_in, mla_q_norm, mla_w_uq, mla_kv_norm, mla_w_ukv, lru_conv_w,
                                lru_conv_b, lru_wa, lru_ba, lru_wx, lru_bx, lru_lambda, ev_w_out)
            q_t, kc, v_t, gm, xl, gl = _even_pre(xc, modb, modc, g, wts, cos_t, sin_t, s_)
            o = _mla_attention(q_t, kc, v_t, s_)
            r = _lru(xl, wts, s_)
            xc = _even_post(xc, modb, modc, o, gm, r, gl, wts["w_out"], s_)
        else:
            q, k, v, gg = _odd_pre_call(xc, modb, modc, g, od_w_in[i].astype(BF16), s_)
            bias = _na_bias_tables(na_rpb[i], rows)
            o = _na_attention(q, k, v, bias, s_, need_ctx)
            xc = _odd_post(xc, modb, modc, o, gg, od_w_out[i].astype(BF16), final_norm_g[None], s_,
                           final=not need_ctx)
    return xc
```

```python
import functools
import math

import numpy as np
import jax
import jax.numpy as jnp
from jax import lax
from jax.experimental import pallas as pl
from jax.experimental.pallas import tpu as pltpu

F32 = jnp.float32
BF16 = jnp.bfloat16

GRID_W = 64
RMS_EPS = 1e-6
ROPE_THETA = 10000.0
MLA_HEADS = 8
MLA_NOPE = 64
MLA_ROPE = 32
MLA_V = 64
MLA_Q_RANK = 256
MLA_KV_RANK = 128
MLA_WIDTH = MLA_HEADS * MLA_V
MLA_SCALE = (MLA_NOPE + MLA_ROPE) ** -0.5
LRU_WIDTH = 512
LRU_BLOCKS = 8
LRU_BLOCK = LRU_WIDTH // LRU_BLOCKS
LRU_CONV = 4
LRU_C = 8.0
NA_HEADS = 16
NA_HEAD_DIM = 64
NA_WIDTH = NA_HEADS * NA_HEAD_DIM
NA_WIN_R = 8
NA_WIN_C = 16

LANES = 128
LOG2E = 1.4426950408889634
NEG_BIG = -1e30
VMEM_LIMIT = 56 * 1024 * 1024

ROW_TILE = 768
FINAL_ROW_TILE = 512
MLA_TQ = 1024
MLA_S_SLOTS = 4
MLA_WQ = 512
MLA_TK = 768
MLA_UNROLL = 2
PROB_ROWS = 16
V_EXT_ROWS = LANES + 16
NA_QROWS = 8
NA_KROWS = 16
NA_VCHUNK = 256
LRU_CHUNK = 256


def _cparams(sem, flags=None):
    return pltpu.CompilerParams(dimension_semantics=sem, vmem_limit_bytes=VMEM_LIMIT, flags=flags)


def _silu(v):
    return v * jax.nn.sigmoid(v)


def _adaln_kernel(cond_ref, w_ref, b_ref, o_ref):
    a = _silu(cond_ref[...])
    o_ref[0] = jnp.dot(a, w_ref[0], preferred_element_type=F32) + b_ref[0]


def _adaln(cond, ada_w, ada_b):
    depth, d, d3 = ada_w.shape
    tn = 1024
    return pl.pallas_call(
        _adaln_kernel,
        grid=(depth, d3 // tn),
        in_specs=[
            pl.BlockSpec((8, d), lambda l, n: (0, 0)),
            pl.BlockSpec((1, d, tn), lambda l, n: (l, 0, n)),
            pl.BlockSpec((1, 1, tn), lambda l, n: (l, 0, n)),
        ],
        out_specs=pl.BlockSpec((1, 8, tn), lambda l, n: (l, 0, n)),
        out_shape=jax.ShapeDtypeStruct((depth, 8, d3), F32),
        compiler_params=_cparams(("arbitrary", "arbitrary")),
        name="adaln",
    )(cond, ada_w, ada_b.reshape(depth, 1, d3))


def _modulated_norm(x, g, modb_ref, modc_ref, is_ctx, d):
    ms = jnp.mean(x * x, axis=-1, keepdims=True)
    y = x * lax.rsqrt(ms + RMS_EPS) * g
    shift = jnp.where(is_ctx, modc_ref[0, :, 0:d], modb_ref[0, :, 0:d])
    scale = jnp.where(is_ctx, modc_ref[0, :, d:2 * d], modb_ref[0, :, d:2 * d])
    return y * (1.0 + scale) + shift


def _row_is_ctx(tm, n_latent):
    row = pl.program_id(1) * tm + lax.broadcasted_iota(jnp.int32, (tm, 1), 0)
    return row >= n_latent


def _rms(v, g):
    ms = jnp.mean(v * v, axis=-1, keepdims=True)
    return v * lax.rsqrt(ms + RMS_EPS) * g


def _even_pre_kernel(x_ref, modb_ref, modc_ref, g_ref, win_ref, qn_ref, wqa_ref, wqb_ref,
                     kvn_ref, wk_ref, wv_ref, cos_ref, sin_ref,
                     q_out, k_out, v_out, gm_out, xl_out, gl_out, *, tm, n_latent, d):
    is_ctx = _row_is_ctx(tm, n_latent)
    h = _modulated_norm(x_ref[0], g_ref[...], modb_ref, modc_ref, is_ctx, d).astype(BF16)

    def proj(lo, hi):
        return jnp.dot(h, win_ref[:, lo:hi], preferred_element_type=F32)

    cos = cos_ref[...]
    sin = sin_ref[...]
    cq = _rms(proj(0, 256), qn_ref[...]).astype(BF16)
    qa = jnp.dot(cq, wqa_ref[...], preferred_element_type=F32)
    qb = jnp.dot(cq, wqb_ref[...], preferred_element_type=F32)
    ckv = _rms(proj(256, 384), kvn_ref[...]).astype(BF16)
    kk = jnp.dot(ckv, wk_ref[...], preferred_element_type=F32)
    k_rope = proj(384, 512) * cos + proj(512, 640) * sin
    v_t = jnp.dot(ckv, wv_ref[...], preferred_element_type=F32).T
    ones_rows = jnp.ones((V_EXT_ROWS - LANES, tm), BF16)
    for p in range(MLA_HEADS // 2):
        v_out[0, p, 0, 0:LANES, :] = v_t[p * LANES:(p + 1) * LANES].astype(BF16)
        v_out[0, p, 0, LANES:V_EXT_ROWS, :] = ones_rows
    for hh in range(MLA_HEADS):
        sl = slice(hh * LANES, (hh + 1) * LANES)
        q = (qa[:, sl] * cos + qb[:, sl] * sin) * (MLA_SCALE * LOG2E)
        q_out[0, sl, :] = q.T.astype(BF16)
        k_out[0, :, sl] = (kk[:, sl] + k_rope).astype(BF16)
    gm_out[0] = _silu(proj(640, 1152))
    xl_out[0] = proj(1152, 1664)
    gl_out[0] = _silu(proj(1664, 2176))


def _const_spec(shape):
    nd = len(shape)
    return pl.BlockSpec(shape, lambda b, t: (0,) * nd)


def _even_pre(xc, modb, modc, g, wts, cos_t, sin_t, n_latent):
    b_, t_, d = xc.shape
    tm = ROW_TILE
    kern = functools.partial(_even_pre_kernel, tm=tm, n_latent=n_latent, d=d)
    tok = lambda w: pl.BlockSpec((1, tm, w), lambda b, t: (b, t, 0))
    hw = MLA_HEADS * LANES
    return pl.pallas_call(
        kern,
        grid=(b_, t_ // tm),
        in_specs=[
            tok(d),
            pl.BlockSpec((1, 1, 3 * d), lambda b, t: (b, 0, 0)),
            _const_spec((1, 1, 3 * d)),
            _const_spec((1, d)),
            _const_spec(wts["w_in"].shape),
            _const_spec((1, MLA_Q_RANK)),
            _const_spec(wts["wqa"].shape),
            _const_spec(wts["wqb"].shape),
            _const_spec((1, MLA_KV_RANK)),
            _const_spec(wts["wk"].shape),
            _const_spec(wts["wv"].shape),
            pl.BlockSpec((tm, LANES), lambda b, t: (t, 0)),
            pl.BlockSpec((tm, LANES), lambda b, t: (t, 0)),
        ],
        out_specs=[pl.BlockSpec((1, hw, tm), lambda b, t: (b, 0, t)), tok(hw),
                   pl.BlockSpec((1, MLA_HEADS // 2, 1, V_EXT_ROWS, tm), lambda b, t: (b, 0, t, 0, 0)),
                   tok(MLA_WIDTH), tok(LRU_WIDTH), tok(LRU_WIDTH)],
        out_shape=[
            jax.ShapeDtypeStruct((b_, hw, t_), BF16),
            jax.ShapeDtypeStruct((b_, t_, hw), BF16),
            jax.ShapeDtypeStruct((b_, MLA_HEADS // 2, t_ // tm, V_EXT_ROWS, tm), BF16),
            jax.ShapeDtypeStruct((b_, t_, MLA_WIDTH), F32),
            jax.ShapeDtypeStruct((b_, t_, LRU_WIDTH), F32),
            jax.ShapeDtypeStruct((b_, t_, LRU_WIDTH), F32),
        ],
        compiler_params=_cparams(("parallel", "parallel")),
        name="even_pre",
    )(xc, modb, modc, g, wts["w_in"], wts["q_norm"], wts["wqa"], wts["wqb"], wts["kv_norm"],
      wts["wk"], wts["wv"], cos_t, sin_t)


def _mla_attn_t_kernel(q_ref, k_ref, v_ref, o_ref, *scratch, tq, wq, tk, nk, v_lane0):
    nh = tq // wq
    n_sub = 2 * nh
    n_total = n_sub * nk
    ns = min(n_sub, MLA_S_SLOTS)
    s_buf, p_buf = scratch[0:ns], scratch[ns:ns + 2]
    acc_ref, m_ref, al_ref = (scratch[ns + 2 + r * n_sub:ns + 2 + (r + 1) * n_sub] for r in range(3))

    def score(c, i):
        j, h = divmod(i, nh)
        start = c * tk if isinstance(c, int) else pl.multiple_of(c * tk, LANES)
        k = k_ref[0, pl.ds(start, tk), j * LANES:(j + 1) * LANES]
        q_t = q_ref[0, j * LANES:(j + 1) * LANES, h * wq:(h + 1) * wq]
        s = jnp.dot(k, q_t, preferred_element_type=F32)
        m = m_ref[i][...]
        m_new = jnp.maximum(m, jnp.max(s, axis=0, keepdims=True))
        al_ref[i][...] = jnp.exp2(m - m_new)
        m_ref[i][...] = m_new
        s_buf[i % ns][...] = s

    def prob(c, i):
        m = jnp.broadcast_to(m_ref[i][...], (PROB_ROWS, wq))
        for r in range(0, tk, PROB_ROWS):
            p = jnp.exp2(s_buf[i % ns][r:r + PROB_ROWS, :] - m)
            p_buf[i % 2][r:r + PROB_ROWS, :] = p.astype(BF16)

    def value(c, i):
        v_t = v_ref[0, 0, c, :, v_lane0:v_lane0 + tk]
        pv = jnp.dot(v_t, p_buf[i % 2][...], preferred_element_type=F32)
        acc_ref[i][...] = al_ref[i][...] * acc_ref[i][...] + pv

    for i in range(n_sub):
        m_ref[i][...] = jnp.full(m_ref[i].shape, -jnp.inf, F32)
        acc_ref[i][...] = jnp.zeros(acc_ref[i].shape, F32)

    def group(c, r):
        for stage, lag in ((value, 3), (prob, 2), (score, 0)):
            dc, i = divmod(r - lag, n_sub)
            if isinstance(c, int) and not 0 <= (c + dc) * n_sub + i < n_total:
                continue
            stage(c + dc, i)

    for g in range(3):
        group(0, g)

    def body(it, _):
        for r in range(3, 3 + MLA_UNROLL * n_sub):
            group(it * MLA_UNROLL, r)
        return 0

    trips = (nk - 1) // MLA_UNROLL
    lax.fori_loop(0, trips, body, 0)
    for g in range(n_sub * MLA_UNROLL * trips + 3, n_total + 3):
        group(0, g)

    for h in range(nh):
        a0 = acc_ref[h][...]
        a1 = acc_ref[nh + h][...]
        o0 = a0[0:MLA_V] / a0[LANES:LANES + 1]
        o1 = a1[MLA_V:LANES] / a1[LANES:LANES + 1]
        o_ref[0, h * wq:(h + 1) * wq, :] = jnp.concatenate([o0, o1], axis=0).T


def _mla_scratch(tq, wq, tk):
    n_sub = 2 * (tq // wq)
    return ([pltpu.VMEM((tk, wq), F32)] * min(n_sub, MLA_S_SLOTS) + [pltpu.VMEM((tk, wq), BF16)] * 2
            + [pltpu.VMEM((V_EXT_ROWS, wq), F32)] * n_sub + [pltpu.VMEM((1, wq), F32)] * (2 * n_sub))


def _mla_attention(q_t, kc, v_t, n_latent):
    b_, t_, _ = kc.shape
    n_pairs = MLA_HEADS // 2
    tq, wq, tk = MLA_TQ, MLA_WQ, MLA_TK
    n_ctx = t_ - n_latent
    nk = t_ // tk
    o_lat = pl.pallas_call(
        functools.partial(_mla_attn_t_kernel, tq=tq, wq=wq, tk=tk, nk=nk, v_lane0=0),
        grid=(b_, n_pairs, n_latent // tq),
        in_specs=[
            pl.BlockSpec((1, 2 * LANES, tq), lambda b, p, i: (b, p, i)),
            pl.BlockSpec((1, t_, 2 * LANES), lambda b, p, i: (b, 0, p)),
            pl.BlockSpec((1, 1, nk, V_EXT_ROWS, tk), lambda b, p, i: (b, p, 0, 0, 0)),
        ],
        out_specs=pl.BlockSpec((1, tq, LANES), lambda b, p, i: (b, i, p)),
        out_shape=jax.ShapeDtypeStruct((b_, n_latent, MLA_WIDTH), F32),
        scratch_shapes=_mla_scratch(tq, wq, tk),
        compiler_params=_cparams(("parallel", "parallel", "arbitrary")),
        name="mla_attn_latent",
    )(q_t, kc, v_t)
    cb = n_latent // n_ctx
    o_ctx = pl.pallas_call(
        functools.partial(_mla_attn_t_kernel, tq=n_ctx, wq=n_ctx, tk=n_ctx, nk=1, v_lane0=tk - n_ctx),
        grid=(b_, n_pairs),
        in_specs=[
            pl.BlockSpec((1, 2 * LANES, n_ctx), lambda b, p: (b, p, cb)),
            pl.BlockSpec((1, n_ctx, 2 * LANES), lambda b, p: (b, cb, p)),
            pl.BlockSpec((1, 1, 1, V_EXT_ROWS, tk), lambda b, p: (b, p, nk - 1, 0, 0)),
        ],
        out_specs=pl.BlockSpec((1, n_ctx, LANES), lambda b, p: (b, 0, p)),
        out_shape=jax.ShapeDtypeStruct((b_, n_ctx, MLA_WIDTH), F32),
        scratch_shapes=_mla_scratch(n_ctx, n_ctx, n_ctx),
        compiler_params=_cparams(("parallel", "parallel")),
        name="mla_attn_ctx",
    )(q_t, kc, v_t)
    return jnp.concatenate([o_lat, o_ctx], axis=1)


def _lru_kernel(x_ref, cw_ref, cb_ref, wg_ref, bg_ref, sp_ref, r_ref, hf_ref, hb_ref,
                *, tc, n_latent, n_ctx):
    nl = n_latent // tc
    ncx = n_ctx // tc
    row = lax.broadcasted_iota(jnp.int32, (tc, 1), 0)

    def conv_chunk(c0, seg_lo, seg_hi):
        xc = x_ref[0, pl.ds(c0, tc), :]
        has_prev = c0 > seg_lo
        has_next = c0 + tc < seg_hi
        p0 = pl.multiple_of(jnp.where(has_prev, c0 - 8, c0), 8)
        n0 = pl.multiple_of(jnp.where(has_next, c0 + tc, c0), 8)
        prev = jnp.where(has_prev, x_ref[0, pl.ds(p0, 8), :], 0.0)
        nxt = jnp.where(has_next, x_ref[0, pl.ds(n0, 8), :], 0.0)
        xm1 = jnp.where(row >= 1, pltpu.roll(xc, 1, 0), prev[7:8])
        xm2 = jnp.where(row >= 2, pltpu.roll(xc, 2, 0), jnp.where(row == 1, prev[7:8], prev[6:7]))
        xp1 = jnp.where(row <= tc - 2, pltpu.roll(xc, tc - 1, 0), nxt[0:1])
        return (cw_ref[0:1] * xm2 + cw_ref[1:2] * xm1 + cw_ref[2:3] * xc + cw_ref[3:4] * xp1
                + cb_ref[...])

    def coeffs(u, d):
        z = jnp.dot(u.astype(BF16), wg_ref[d], preferred_element_type=F32) + bg_ref[d]
        r = jax.nn.sigmoid(z[:, :LANES])
        i = jax.nn.sigmoid(z[:, LANES:])
        log_a = -LRU_C * r * sp_ref[d]
        a = jnp.exp(log_a)
        t = jnp.tanh(log_a)
        uu = jnp.sqrt(-2.0 * t / (1.0 - t)) * (i * u)
        return a, uu

    def scan_chunk(a, u, h_in, reverse):
        s = 1
        while s < tc:
            if reverse:
                keep = row < tc - s
                a_s = jnp.where(keep, pltpu.roll(a, tc - s, 0), 1.0)
                u_s = jnp.where(keep, pltpu.roll(u, tc - s, 0), 0.0)
            else:
                keep = row >= s
                a_s = jnp.where(keep, pltpu.roll(a, s, 0), 1.0)
                u_s = jnp.where(keep, pltpu.roll(u, s, 0), 0.0)
            u = a * u_s + u
            a = a * a_s
            s *= 2
        return a * h_in + u

    def chunk_start(i):
        return pl.multiple_of(jnp.where(i < ncx, n_latent + i * tc, (i - ncx) * tc), tc)

    def seg_bounds(i):
        lo = jnp.where(i < ncx, n_latent, 0)
        hi = jnp.where(i < ncx, n_latent + n_ctx, n_latent)
        return lo, hi

    hf_ref[...] = jnp.zeros_like(hf_ref)
    hb_ref[...] = jnp.zeros_like(hb_ref)

    def fwd(i, _):
        c0 = chunk_start(i)
        lo, hi = seg_bounds(i)
        a, uu = coeffs(conv_chunk(c0, lo, hi), 0)
        h = scan_chunk(a, uu, hf_ref[0:1], False)
        hf_ref[...] = jnp.broadcast_to(h[tc - 1:tc], hf_ref.shape)
        r_ref[0, pl.ds(c0, tc), :] = h
        return 0

    lax.fori_loop(0, ncx + nl, fwd, 0)

    def bwd(i, _):
        ii = jnp.where(i < ncx, ncx - 1 - i, ncx + (nl - 1 - (i - ncx)))
        c0 = chunk_start(ii)
        lo, hi = seg_bounds(ii)
        a, uu = coeffs(conv_chunk(c0, lo, hi), 1)
        h = scan_chunk(a, uu, hb_ref[0:1], True)
        hb_ref[...] = jnp.broadcast_to(h[0:1], hb_ref.shape)
        r_ref[0, pl.ds(c0, tc), :] = r_ref[0, pl.ds(c0, tc), :] + h
        return 0

    lax.fori_loop(0, ncx + nl, bwd, 0)


def _lru(x_lru, wts, n_latent):
    b_, t_, w = x_lru.shape
    ng = w // LANES
    kern = functools.partial(_lru_kernel, tc=LRU_CHUNK, n_latent=n_latent, n_ctx=t_ - n_latent)
    return pl.pallas_call(
        kern,
        grid=(b_, ng),
        in_specs=[
            pl.BlockSpec((1, t_, LANES), lambda b, g: (b, 0, g)),
            pl.BlockSpec((LRU_CONV, LANES), lambda b, g: (0, g)),
            pl.BlockSpec((1, LANES), lambda b, g: (0, g)),
            pl.BlockSpec((2, None, LANES, 2 * LANES), lambda b, g: (0, g, 0, 0)),
            pl.BlockSpec((2, None, 1, 2 * LANES), lambda b, g: (0, g, 0, 0)),
            pl.BlockSpec((2, None, 1, LANES), lambda b, g: (0, g, 0, 0)),
        ],
        out_specs=pl.BlockSpec((1, t_, LANES), lambda b, g: (b, 0, g)),
        out_shape=jax.ShapeDtypeStruct((b_, t_, w), F32),
        scratch_shapes=[pltpu.VMEM((8, LANES), F32), pltpu.VMEM((8, LANES), F32)],
        compiler_params=_cparams(("parallel", "parallel")),
        name="rglru",
    )(x_lru, wts["conv_w"], wts["conv_b"], wts["w_gate"], wts["b_gate"], wts["softplus"])


def _residual(x_ref, y, modb_ref, modc_ref, is_ctx, d):
    gate = jnp.where(is_ctx, modc_ref[0, :, 2 * d:3 * d], modb_ref[0, :, 2 * d:3 * d])
    return x_ref[0] + gate * y


def _even_post_kernel(x_ref, modb_ref, modc_ref, o_ref, gm_ref, r_ref, gl_ref, wo_ref, out_ref,
                      *, tm, n_latent, d):
    is_ctx = _row_is_ctx(tm, n_latent)
    a = (o_ref[0] * gm_ref[0]).astype(BF16)
    bb = (r_ref[0] * gl_ref[0]).astype(BF16)
    y = (jnp.dot(a, wo_ref[0:MLA_WIDTH, :], preferred_element_type=F32)
         + jnp.dot(bb, wo_ref[MLA_WIDTH:, :], preferred_element_type=F32))
    out_ref[0] = _residual(x_ref, y, modb_ref, modc_ref, is_ctx, d)


def _even_post(xc, modb, modc, o, gm, r, gl, w_out, n_latent):
    b_, t_, d = xc.shape
    tm = ROW_TILE
    tok = lambda w: pl.BlockSpec((1, tm, w), lambda b, t: (b, t, 0))
    return pl.pallas_call(
        functools.partial(_even_post_kernel, tm=tm, n_latent=n_latent, d=d),
        grid=(b_, t_ // tm),
        in_specs=[
            tok(d),
            pl.BlockSpec((1, 1, 3 * d), lambda b, t: (b, 0, 0)),
            _const_spec((1, 1, 3 * d)),
            tok(MLA_WIDTH), tok(MLA_WIDTH), tok(LRU_WIDTH), tok(LRU_WIDTH),
            _const_spec(w_out.shape),
        ],
        out_specs=tok(d),
        out_shape=jax.ShapeDtypeStruct((b_, t_, d), F32),
        compiler_params=_cparams(("parallel", "parallel")),
        name="even_post",
    )(xc, modb, modc, o, gm, r, gl, w_out)


def _odd_post_kernel(x_ref, modb_ref, modc_ref, o_ref, g_ref, wo_ref, fg_ref, out_ref,
                     *, tm, n_latent, d, final):
    is_ctx = _row_is_ctx(tm, n_latent)
    a = (o_ref[0] * g_ref[0]).astype(BF16)
    y = jnp.dot(a, wo_ref[...], preferred_element_type=F32)
    xn = _residual(x_ref, y, modb_ref, modc_ref, is_ctx, d)
    if final:
        xn = _rms(xn, fg_ref[...])
    out_ref[0] = xn


def _odd_post(xc, modb, modc, o, g, w_out, final_g, n_latent, final):
    b_, t_, d = xc.shape
    tm = FINAL_ROW_TILE if final else ROW_TILE
    n_rows = n_latent if final else t_
    tok = lambda w: pl.BlockSpec((1, tm, w), lambda b, t: (b, t, 0))
    return pl.pallas_call(
        functools.partial(_odd_post_kernel, tm=tm, n_latent=n_latent, d=d, final=final),
        grid=(b_, n_rows // tm),
        in_specs=[
            tok(d),
            pl.BlockSpec((1, 1, 3 * d), lambda b, t: (b, 0, 0)),
            _const_spec((1, 1, 3 * d)),
            tok(NA_WIDTH), tok(NA_WIDTH),
            _const_spec(w_out.shape),
            _const_spec((1, d)),
        ],
        out_specs=tok(d),
        out_shape=jax.ShapeDtypeStruct((b_, n_rows, d), F32),
        compiler_params=_cparams(("parallel", "parallel")),
        name="odd_post_final" if final else "odd_post",
    )(xc, modb, modc, o, g, w_out, final_g)


def _odd_pre_kernel(x_ref, modb_ref, modc_ref, g_ref, win_ref, q_out, k_out, v_out, g_out,
                    *, tm, n_latent, d):
    is_ctx = _row_is_ctx(tm, n_latent)
    h = _modulated_norm(x_ref[0], g_ref[...], modb_ref, modc_ref, is_ctx, d).astype(BF16)
    w = NA_WIDTH
    hd = NA_HEAD_DIM
    q_t = (jnp.dot(h, win_ref[:, 0:w], preferred_element_type=F32) * (hd ** -0.5 * LOG2E)).T
    zeros = jnp.zeros((hd, tm), BF16)
    for hh in range(NA_HEADS):
        qh = q_t[hh * hd:(hh + 1) * hd].astype(BF16)
        q_out[0, hh * LANES:(hh + 1) * LANES, :] = jnp.concatenate(
            [qh, zeros] if hh % 2 == 0 else [zeros, qh], axis=0)
    k_out[0] = jnp.dot(h, win_ref[:, w:2 * w], preferred_element_type=F32).astype(BF16)
    v_t = jnp.dot(h, win_ref[:, 2 * w:3 * w], preferred_element_type=F32).T
    ones_rows = jnp.ones((V_EXT_ROWS - LANES, NA_VCHUNK), BF16)
    for p in range(NA_HEADS // 2):
        for cc in range(tm // NA_VCHUNK):
            v_out[0, p, cc, 0:LANES, :] = v_t[p * LANES:(p + 1) * LANES,
                                              cc * NA_VCHUNK:(cc + 1) * NA_VCHUNK].astype(BF16)
            v_out[0, p, cc, LANES:V_EXT_ROWS, :] = ones_rows
    g_out[0] = _silu(jnp.dot(h, win_ref[:, 3 * w:4 * w], preferred_element_type=F32))


def _odd_pre_call(xc, modb, modc, g, w_in, n_latent):
    b_, t_, d = xc.shape
    tm = ROW_TILE
    tok = lambda w: pl.BlockSpec((1, tm, w), lambda b, t: (b, t, 0))
    return pl.pallas_call(
        functools.partial(_odd_pre_kernel, tm=tm, n_latent=n_latent, d=d),
        grid=(b_, t_ // tm),
        in_specs=[
            tok(d),
            pl.BlockSpec((1, 1, 3 * d), lambda b, t: (b, 0, 0)),
            _const_spec((1, 1, 3 * d)),
            _const_spec((1, d)),
            _const_spec(w_in.shape),
        ],
        out_specs=[
            pl.BlockSpec((1, NA_HEADS * LANES, tm), lambda b, t: (b, 0, t)),
            tok(NA_WIDTH),
            pl.BlockSpec((1, NA_HEADS // 2, tm // NA_VCHUNK, V_EXT_ROWS, NA_VCHUNK),
                         lambda b, t: (b, 0, t, 0, 0)),
            tok(NA_WIDTH),
        ],
        out_shape=[
            jax.ShapeDtypeStruct((b_, NA_HEADS * LANES, t_), BF16),
            jax.ShapeDtypeStruct((b_, t_, NA_WIDTH), BF16),
            jax.ShapeDtypeStruct((b_, NA_HEADS // 2, t_ // NA_VCHUNK, V_EXT_ROWS, NA_VCHUNK), BF16),
            jax.ShapeDtypeStruct((b_, t_, NA_WIDTH), F32),
        ],
        compiler_params=_cparams(("parallel", "parallel")),
        name="odd_pre",
    )(xc, modb, modc, g, w_in)


def _na_kernel(q_ref, k_ref, v_ref, *rest, nq, nkl, n_ctx, rows, k_ctx0, v_ctx0):
    local = nkl > 0
    if local:
        bias_ref, o_ref = rest[0:2]
        scratch = rest[2:]
        k_row0 = jnp.clip(pl.program_id(2) * NA_QROWS - NA_WIN_R // 2, 0, rows - NA_KROWS)
        kstart = pl.multiple_of(k_row0 * GRID_W, NA_VCHUNK)
        v_loc0 = k_row0 * GRID_W // NA_VCHUNK
    else:
        o_ref = rest[0]
        scratch = rest[1:]
    s_buf, p_buf, m_ref = scratch[0:2], scratch[2:4], scratch[4:6]

    def score(j):
        q_t = q_ref[0, j * LANES:(j + 1) * LANES, :]
        s_c = jnp.dot(k_ref[0, k_ctx0:k_ctx0 + n_ctx, :], q_t, preferred_element_type=F32)
        s_buf[j][nkl:nkl + n_ctx, :] = s_c
        m = jnp.max(s_c, axis=0, keepdims=True)
        if local:
            s_l = jnp.dot(k_ref[0, pl.ds(kstart, nkl), :], q_t, preferred_element_type=F32)
            s_l = s_l + bias_ref[0, 0, j]
            s_buf[j][0:nkl, :] = s_l
            m = jnp.maximum(m, jnp.max(s_l, axis=0, keepdims=True))
        m_ref[j][...] = m

    def prob(j):
        m = jnp.broadcast_to(m_ref[j][...], (PROB_ROWS, nq))
        for r in range(0, nkl + n_ctx, PROB_ROWS):
            p_buf[j][r:r + PROB_ROWS, :] = jnp.exp2(s_buf[j][r:r + PROB_ROWS, :] - m).astype(BF16)

    def value(j):
        acc = None
        for cc in range((nkl + n_ctx) // NA_VCHUNK):
            in_window = cc < nkl // NA_VCHUNK
            v_t = v_ref[0, 0, v_loc0 + cc] if in_window else v_ref[0, 0, v_ctx0 + cc - nkl // NA_VCHUNK]
            pv = jnp.dot(v_t, p_buf[j][cc * NA_VCHUNK:(cc + 1) * NA_VCHUNK, :],
                         preferred_element_type=F32)
            acc = pv if acc is None else acc + pv
        return acc

    score(0)
    score(1)
    prob(0)
    a0 = value(0)
    prob(1)
    a1 = value(1)
    o0 = a0[0:NA_HEAD_DIM] / a0[LANES:LANES + 1]
    o1 = a1[NA_HEAD_DIM:LANES] / a1[LANES:LANES + 1]
    o_ref[0] = jnp.concatenate([o0, o1], axis=0).T


def _na_bias_tables(rpb, rows):
    nq = NA_QROWS * GRID_W
    nkl = NA_KROWS * GRID_W
    n_heads = rpb.shape[0]
    pad = GRID_W - NA_WIN_C
    rp = jnp.pad(rpb * LOG2E, ((0, 0), (0, 0), (pad, pad)))
    toep = jnp.stack([rp[:, :, GRID_W - 1 - qc:2 * GRID_W - 1 - qc] for qc in range(GRID_W)], axis=3)
    col = np.arange(GRID_W)
    cs = np.clip(col - NA_WIN_C // 2, 0, GRID_W - NA_WIN_C)
    valid_c = (col[:, None] >= cs[None, :]) & (col[:, None] < cs[None, :] + NA_WIN_C)
    toep = jnp.where(valid_c, toep, NEG_BIG)
    masked = jnp.full((n_heads, GRID_W, GRID_W), NEG_BIG, F32)
    tabs = []
    for r0 in (0, 2 * NA_QROWS, rows - NA_QROWS):
        ks = int(np.clip(r0 - NA_WIN_R // 2, 0, rows - NA_KROWS))
        k_rows = []
        for kr in range(NA_KROWS):
            kabs = ks + kr
            slabs = []
            for qr in range(NA_QROWS):
                r = r0 + qr
                rs = int(np.clip(r - NA_WIN_R // 2, 0, rows - NA_WIN_R))
                slabs.append(toep[:, kabs - r + NA_WIN_R - 1] if rs <= kabs < rs + NA_WIN_R else masked)
            k_rows.append(jnp.concatenate(slabs, axis=-1))
        tabs.append(jnp.concatenate(k_rows, axis=1))
    t = jnp.stack(tabs, axis=1)
    return t.reshape(n_heads // 2, 2, 3, nkl, nq).transpose(0, 2, 1, 3, 4)


def _na_scratch(n_keys, nq):
    return ([pltpu.VMEM((n_keys, nq), F32)] * 2 + [pltpu.VMEM((n_keys, nq), BF16)] * 2
            + [pltpu.VMEM((1, nq), F32)] * 2)


def _na_attention(q_t, k, v_t, bias, n_latent, need_ctx):
    b_, t_, _ = k.shape
    n_ctx = t_ - n_latent
    rows = n_latent // GRID_W
    nq = NA_QROWS * GRID_W
    nkl = NA_KROWS * GRID_W
    n_blocks = rows // NA_QROWS
    n_pairs = NA_HEADS // 2
    n_vc = t_ // NA_VCHUNK

    def bias_idx(p, b, i):
        return (p, jnp.where(i == 0, 0, jnp.where(i == n_blocks - 1, 2, 1)), 0, 0, 0)

    o_lat = pl.pallas_call(
        functools.partial(_na_kernel, nq=nq, nkl=nkl, n_ctx=n_ctx, rows=rows, k_ctx0=n_latent,
                          v_ctx0=n_latent // NA_VCHUNK),
        grid=(n_pairs, b_, n_blocks),
        in_specs=[
            pl.BlockSpec((1, 2 * LANES, nq), lambda p, b, i: (b, p, i)),
            pl.BlockSpec((1, t_, LANES), lambda p, b, i: (b, 0, p)),
            pl.BlockSpec((1, 1, n_vc, V_EXT_ROWS, NA_VCHUNK), lambda p, b, i: (b, p, 0, 0, 0)),
            pl.BlockSpec((1, 1, 2, nkl, nq), bias_idx),
        ],
        out_specs=pl.BlockSpec((1, nq, LANES), lambda p, b, i: (b, i, p)),
        out_shape=jax.ShapeDtypeStruct((b_, n_latent, NA_WIDTH), F32),
        scratch_shapes=_na_scratch(nkl + n_ctx, nq),
        compiler_params=_cparams(("parallel", "parallel", "arbitrary")),
        name="na_attn",
    )(q_t, k, v_t, bias)
    if not need_ctx:
        return o_lat
    cb = n_latent // n_ctx
    o_ctx = pl.pallas_call(
        functools.partial(_na_kernel, nq=n_ctx, nkl=0, n_ctx=n_ctx, rows=rows, k_ctx0=0, v_ctx0=0),
        grid=(n_pairs, b_),
        in_specs=[
            pl.BlockSpec((1, 2 * LANES, n_ctx), lambda p, b: (b, p, cb)),
            pl.BlockSpec((1, n_ctx, LANES), lambda p, b: (b, cb, p)),
            pl.BlockSpec((1, 1, n_ctx // NA_VCHUNK, V_EXT_ROWS, NA_VCHUNK), lambda p, b: (b, p, cb, 0, 0)),
        ],
        out_specs=pl.BlockSpec((1, n_ctx, LANES), lambda p, b: (b, 0, p)),
        out_shape=jax.ShapeDtypeStruct((b_, n_ctx, NA_WIDTH), F32),
        scratch_shapes=_na_scratch(n_ctx, n_ctx),
        compiler_params=_cparams(("parallel", "parallel")),
        name="na_attn_ctx",
    )(q_t, k, v_t)
    return jnp.concatenate([o_lat, o_ctx], axis=1)


def _rope_swap_index():
    half = MLA_ROPE // 4
    idx = np.arange(MLA_ROPE)
    return np.where((idx // half) % 2 == 0, idx + half, idx - half)


def _rope_tables(n_latent, n_ctx):
    n_freq = MLA_ROPE // 4
    inv = ROPE_THETA ** (-jnp.arange(n_freq, dtype=F32) / n_freq)
    t = jnp.arange(n_latent, dtype=jnp.int32)
    ang_r = (t // GRID_W).astype(F32)[:, None] * inv
    ang_c = (t % GRID_W).astype(F32)[:, None] * inv
    cr, sr, cc, sc = jnp.cos(ang_r), jnp.sin(ang_r), jnp.cos(ang_c), jnp.sin(ang_c)
    cos32 = jnp.concatenate([cr, cr, cc, cc], axis=-1)
    sin32 = jnp.concatenate([-sr, sr, -sc, sc], axis=-1)
    ones = jnp.ones((n_latent, MLA_NOPE), F32)
    zeros = jnp.zeros((n_latent, LANES - MLA_NOPE - MLA_ROPE), F32)
    cos_l = jnp.concatenate([ones, cos32, zeros], axis=-1)
    sin_l = jnp.concatenate([0 * ones, sin32, zeros], axis=-1)
    cos_c = jnp.concatenate([jnp.ones((n_ctx, MLA_NOPE + MLA_ROPE), F32),
                             jnp.zeros((n_ctx, LANES - MLA_NOPE - MLA_ROPE), F32)], axis=-1)
    sin_c = jnp.zeros((n_ctx, LANES), F32)
    return jnp.concatenate([cos_l, cos_c], axis=0), jnp.concatenate([sin_l, sin_c], axis=0)


def _even_weights(i, ev_w_in, mla_q_norm, mla_w_uq, mla_kv_norm, mla_w_ukv, lru_conv_w, lru_conv_b,
                  lru_wa, lru_ba, lru_wx, lru_bx, lru_lambda, ev_w_out):
    sw = _rope_swap_index()
    d = ev_w_in.shape[1]
    w_in = ev_w_in[i]
    o_cq, o_ckv, o_kr = MLA_Q_RANK, MLA_Q_RANK + MLA_KV_RANK, MLA_Q_RANK + MLA_KV_RANK + MLA_ROPE
    kr = w_in[:, o_ckv:o_kr]
    pad_lo = jnp.zeros((d, MLA_NOPE), F32)
    pad_hi = jnp.zeros((d, LANES - MLA_NOPE - MLA_ROPE), F32)
    w_in_ext = jnp.concatenate(
        [w_in[:, :o_ckv], pad_lo, kr, pad_hi, pad_lo, kr[:, sw], pad_hi, w_in[:, o_kr:]], axis=1)
    wq = mla_w_uq[i].reshape(MLA_Q_RANK, MLA_HEADS, MLA_NOPE + MLA_ROPE)
    zq = jnp.zeros((MLA_Q_RANK, MLA_HEADS, LANES - MLA_NOPE - MLA_ROPE), F32)
    wqa = jnp.concatenate([wq, zq], axis=-1).reshape(MLA_Q_RANK, MLA_HEADS * LANES)
    wqb = jnp.concatenate([jnp.zeros((MLA_Q_RANK, MLA_HEADS, MLA_NOPE), F32),
                           wq[:, :, MLA_NOPE:][:, :, sw], zq], axis=-1).reshape(MLA_Q_RANK, MLA_HEADS * LANES)
    wkv = mla_w_ukv[i].reshape(MLA_KV_RANK, MLA_HEADS, MLA_NOPE + MLA_V)
    wk = jnp.concatenate([wkv[:, :, :MLA_NOPE], jnp.zeros((MLA_KV_RANK, MLA_HEADS, LANES - MLA_NOPE), F32)],
                         axis=-1).reshape(MLA_KV_RANK, MLA_HEADS * LANES)
    wv = wkv[:, :, MLA_NOPE:].reshape(MLA_KV_RANK, MLA_WIDTH)
    ng = LRU_WIDTH // LANES
    per = LANES // LRU_BLOCK

    def blockdiag(w):
        w = w.reshape(2, ng, per, LRU_BLOCK, LRU_BLOCK)
        eye = jnp.eye(per, dtype=F32)
        return jnp.einsum('dgpkj,pq->dgpkqj', w, eye).reshape(2, ng, LANES, LANES)

    w_gate = jnp.concatenate([blockdiag(lru_wa[i]), blockdiag(lru_wx[i])], axis=-1).astype(BF16)
    b_gate = jnp.concatenate([lru_ba[i].reshape(2, ng, 1, LANES), lru_bx[i].reshape(2, ng, 1, LANES)], axis=-1)
    return dict(
        w_in=w_in_ext.astype(BF16), q_norm=mla_q_norm[i][None], wqa=wqa.astype(BF16), wqb=wqb.astype(BF16),
        kv_norm=mla_kv_norm[i][None], wk=wk.astype(BF16), wv=wv.astype(BF16),
        conv_w=lru_conv_w[i], conv_b=lru_conv_b[i][None], w_gate=w_gate, b_gate=b_gate,
        softplus=jax.nn.softplus(-lru_lambda[i]).reshape(2, ng, 1, LANES),
        w_out=ev_w_out[i].astype(BF16))


def kernel(x, c, ctx, c_ctx, ada_w, ada_b, norm_g, ev_w_in, mla_q_norm, mla_w_uq, mla_kv_norm, mla_w_ukv,
           lru_conv_w, lru_conv_b, lru_wa, lru_ba, lru_wx, lru_bx, lru_lambda, ev_w_out, od_w_in, na_rpb,
           od_w_out, final_norm_g):
    b_, s_, d = x.shape
    n_ctx = ctx.shape[1]
    depth = ada_w.shape[0]
    rows = s_ // GRID_W
    assert b_ <= 7 and s_ % GRID_W == 0 and rows % NA_QROWS == 0 and rows >= NA_KROWS
    assert (s_ + n_ctx) % ROW_TILE == 0 and s_ % FINAL_ROW_TILE == 0 and s_ % MLA_TQ == 0
    assert ROW_TILE == MLA_TK and n_ctx <= MLA_TK and s_ % n_ctx == 0 and n_ctx % LRU_CHUNK == 0 and s_ % LRU_CHUNK == 0
    assert n_ctx % NA_VCHUNK == 0 and ROW_TILE % NA_VCHUNK == 0 and rows >= 3 * NA_QROWS
    assert (NA_QROWS // 2 * GRID_W) % NA_VCHUNK == 0 and depth % 2 == 0

    cond = jnp.zeros((8, d), F32).at[:b_].set(c).at[b_].set(c_ctx)
    mod = _adaln(cond, ada_w, ada_b)
    xc = jnp.concatenate([x, ctx], axis=1)
    cos_t, sin_t = _rope_tables(s_, n_ctx)

    for layer in range(depth):
        need_ctx = layer < depth - 1
        i = layer // 2
        modb = mod[layer, :b_][:, None, :]
        modc = mod[layer, b_][None, None, :]
        g = norm_g[layer][None]
        if layer % 2 == 0:
            wts = _even_weights(i, ev_w_in, mla_q_norm, mla_w_uq, mla_kv_norm, mla_w_ukv, lru_conv_w,
                                lru_conv_b, lru_wa, lru_ba, lru_wx, lru_bx, lru_lambda, ev_w_out)
            q_t, kc, v_t, gm, xl, gl = _even_pre(xc, modb, modc, g, wts, cos_t, sin_t, s_)
            o = _mla_attention(q_t, kc, v_t, s_)
            r = _lru(xl, wts, s_)
            xc = _even_post(xc, modb, modc, o, gm, r, gl, wts["w_out"], s_)
        else:
            q, k, v, gg = _odd_pre_call(xc, modb, modc, g, od_w_in[i].astype(BF16), s_)
            bias = _na_bias_tables(na_rpb[i], rows)
            o = _na_attention(q, k, v, bias, s_, need_ctx)
            xc = _odd_post(xc, modb, modc, o, gg, od_w_out[i].astype(BF16), final_norm_g[None], s_,
                           final=not need_ctx)
    return xc
```

```python
import functools
import math

import numpy as np
import jax
import jax.numpy as jnp
from jax import lax
from jax.experimental import pallas as pl
from jax.experimental.pallas import tpu as pltpu

F32 = jnp.float32
BF16 = jnp.bfloat16

GRID_W = 64
RMS_EPS = 1e-6
ROPE_THETA = 10000.0
MLA_HEADS = 8
MLA_NOPE = 64
MLA_ROPE = 32
MLA_V = 64
MLA_Q_RANK = 256
MLA_KV_RANK = 128
MLA_WIDTH = MLA_HEADS * MLA_V
MLA_SCALE = (MLA_NOPE + MLA_ROPE) ** -0.5
LRU_WIDTH = 512
LRU_BLOCKS = 8
LRU_BLOCK = LRU_WIDTH // LRU_BLOCKS
LRU_CONV = 4
LRU_C = 8.0
NA_HEADS = 16
NA_HEAD_DIM = 64
NA_WIDTH = NA_HEADS * NA_HEAD_DIM
NA_WIN_R = 8
NA_WIN_C = 16

LANES = 128
LOG2E = 1.4426950408889634
NEG_BIG = -1e30
VMEM_LIMIT = 56 * 1024 * 1024

ROW_TILE = 768
FINAL_ROW_TILE = 512
MLA_TQ = 1024
MLA_S_SLOTS = 4
MLA_WQ = 512
MLA_TK = 768
MLA_UNROLL = 2
PROB_ROWS = 16
V_EXT_ROWS = LANES + 16
NA_QROWS = 8
NA_KROWS = 16
NA_VCHUNK = 256
LRU_CHUNK = 256


def _cparams(sem, flags=None):
    return pltpu.CompilerParams(dimension_semantics=sem, vmem_limit_bytes=VMEM_LIMIT, flags=flags)


def _silu(v):
    return v * jax.nn.sigmoid(v)


def _adaln_kernel(cond_ref, w_ref, b_ref, o_ref):
    a = _silu(cond_ref[...])
    o_ref[0] = jnp.dot(a, w_ref[0], preferred_element_type=F32) + b_ref[0]


def _adaln(cond, ada_w, ada_b):
    depth, d, d3 = ada_w.shape
    tn = 1024
    return pl.pallas_call(
        _adaln_kernel,
        grid=(depth, d3 // tn),
        in_specs=[
            pl.BlockSpec((8, d), lambda l, n: (0, 0)),
            pl.BlockSpec((1, d, tn), lambda l, n: (l, 0, n)),
            pl.BlockSpec((1, 1, tn), lambda l, n: (l, 0, n)),
        ],
        out_specs=pl.BlockSpec((1, 8, tn), lambda l, n: (l, 0, n)),
        out_shape=jax.ShapeDtypeStruct((depth, 8, d3), F32),
        compiler_params=_cparams(("arbitrary", "arbitrary")),
        name="adaln",
    )(cond, ada_w, ada_b.reshape(depth, 1, d3))


def _modulated_norm(x, g, modb_ref, modc_ref, is_ctx, d):
    ms = jnp.mean(x * x, axis=-1, keepdims=True)
    y = x * lax.rsqrt(ms + RMS_EPS) * g
    shift = jnp.where(is_ctx, modc_ref[0, :, 0:d], modb_ref[0, :, 0:d])
    scale = jnp.where(is_ctx, modc_ref[0, :, d:2 * d], modb_ref[0, :, d:2 * d])
    return y * (1.0 + scale) + shift


def _row_is_ctx(tm, n_latent):
    row = pl.program_id(1) * tm + lax.broadcasted_iota(jnp.int32, (tm, 1), 0)
    return row >= n_latent


def _rms(v, g):
    ms = jnp.mean(v * v, axis=-1, keepdims=True)
    return v * lax.rsqrt(ms + RMS_EPS) * g


def _even_pre_kernel(x_ref, modb_ref, modc_ref, g_ref, win_ref, qn_ref, wqa_ref, wqb_ref,
                     kvn_ref, wk_ref, wv_ref, cos_ref, sin_ref,
                     q_out, k_out, v_out, gm_out, xl_out, gl_out, *, tm, n_latent, d):
    is_ctx = _row_is_ctx(tm, n_latent)
    h = _modulated_norm(x_ref[0], g_ref[...], modb_ref, modc_ref, is_ctx, d).astype(BF16)

    def proj(lo, hi):
        return jnp.dot(h, win_ref[:, lo:hi], preferred_element_type=F32)

    cos = cos_ref[...]
    sin = sin_ref[...]
    cq = _rms(proj(0, 256), qn_ref[...]).astype(BF16)
    qa = jnp.dot(cq, wqa_ref[...], preferred_element_type=F32)
    qb = jnp.dot(cq, wqb_ref[...], preferred_element_type=F32)
    ckv = _rms(proj(256, 384), kvn_ref[...]).astype(BF16)
    kk = jnp.dot(ckv, wk_ref[...], preferred_element_type=F32)
    k_rope = proj(384, 512) * cos + proj(512, 640) * sin
    v_t = jnp.dot(ckv, wv_ref[...], preferred_element_type=F32).T
    ones_rows = jnp.ones((V_EXT_ROWS - LANES, tm), BF16)
    for p in range(MLA_HEADS // 2):
        v_out[0, p, 0, 0:LANES, :] = v_t[p * LANES:(p + 1) * LANES].astype(BF16)
        v_out[0, p, 0, LANES:V_EXT_ROWS, :] = ones_rows
    for hh in range(MLA_HEADS):
        sl = slice(hh * LANES, (hh + 1) * LANES)
        q = (qa[:, sl] * cos + qb[:, sl] * sin) * (MLA_SCALE * LOG2E)
        q_out[0, sl, :] = q.T.astype(BF16)
        k_out[0, :, sl] = (kk[:, sl] + k_rope).astype(BF16)
    gm_out[0] = _silu(proj(640, 1152))
    xl_out[0] = proj(1152, 1664)
    gl_out[0] = _silu(proj(1664, 2176))


def _const_spec(shape):
    nd = len(shape)
    return pl.BlockSpec(shape, lambda b, t: (0,) * nd)


def _even_pre(xc, modb, modc, g, wts, cos_t, sin_t, n_latent):
    b_, t_, d = xc.shape
    tm = ROW_TILE
    kern = functools.partial(_even_pre_kernel, tm=tm, n_latent=n_latent, d=d)
    tok = lambda w: pl.BlockSpec((1, tm, w), lambda b, t: (b, t, 0))
    hw = MLA_HEADS * LANES
    return pl.pallas_call(
        kern,
        grid=(b_, t_ // tm),
        in_specs=[
            tok(d),
            pl.BlockSpec((1, 1, 3 * d), lambda b, t: (b, 0, 0)),
            _const_spec((1, 1, 3 * d)),
            _const_spec((1, d)),
            _const_spec(wts["w_in"].shape),
            _const_spec((1, MLA_Q_RANK)),
            _const_spec(wts["wqa"].shape),
            _const_spec(wts["wqb"].shape),
            _const_spec((1, MLA_KV_RANK)),
            _const_spec(wts["wk"].shape),
            _const_spec(wts["wv"].shape),
            pl.BlockSpec((tm, LANES), lambda b, t: (t, 0)),
            pl.BlockSpec((tm, LANES), lambda b, t: (t, 0)),
        ],
        out_specs=[pl.BlockSpec((1, hw, tm), lambda b, t: (b, 0, t)), tok(hw),
                   pl.BlockSpec((1, MLA_HEADS // 2, 1, V_EXT_ROWS, tm), lambda b, t: (b, 0, t, 0, 0)),
                   tok(MLA_WIDTH), tok(LRU_WIDTH), tok(LRU_WIDTH)],
        out_shape=[
            jax.ShapeDtypeStruct((b_, hw, t_), BF16),
            jax.ShapeDtypeStruct((b_, t_, hw), BF16),
            jax.ShapeDtypeStruct((b_, MLA_HEADS // 2, t_ // tm, V_EXT_ROWS, tm), BF16),
            jax.ShapeDtypeStruct((b_, t_, MLA_WIDTH), F32),
            jax.ShapeDtypeStruct((b_, t_, LRU_WIDTH), F32),
            jax.ShapeDtypeStruct((b_, t_, LRU_WIDTH), F32),
        ],
        compiler_params=_cparams(("parallel", "parallel")),
        name="even_pre",
    )(xc, modb, modc, g, wts["w_in"], wts["q_norm"], wts["wqa"], wts["wqb"], wts["kv_norm"],
      wts["wk"], wts["wv"], cos_t, sin_t)


def _mla_attn_t_kernel(q_ref, k_ref, v_ref, o_ref, *scratch, tq, wq, tk, nk, v_lane0):
    nh = tq // wq
    n_sub = 2 * nh
    n_total = n_sub * nk
    ns = min(n_sub, MLA_S_SLOTS)
    s_buf, p_buf = scratch[0:ns], scratch[ns:ns + 2]
    acc_ref, m_ref, al_ref = (scratch[ns + 2 + r * n_sub:ns + 2 + (r + 1) * n_sub] for r in range(3))

    def score(c, i):
        j, h = divmod(i, nh)
        start = c * tk if isinstance(c, int) else pl.multiple_of(c * tk, LANES)
        k = k_ref[0, pl.ds(start, tk), j * LANES:(j + 1) * LANES]
        q_t = q_ref[0, j * LANES:(j + 1) * LANES, h * wq:(h + 1) * wq]
        s = jnp.dot(k, q_t, preferred_element_type=F32)
        m = m_ref[i][...]
        m_new = jnp.maximum(m, jnp.max(s, axis=0, keepdims=True))
        al_ref[i][...] = jnp.exp2(m - m_new)
        m_ref[i][...] = m_new
        s_buf[i % ns][...] = s

    def prob(c, i):
        m = jnp.broadcast_to(m_ref[i][...], (PROB_ROWS, wq))
        for r in range(0, tk, PROB_ROWS):
            p = jnp.exp2(s_buf[i % ns][r:r + PROB_ROWS, :] - m)
            p_buf[i % 2][r:r + PROB_ROWS, :] = p.astype(BF16)

    def value(c, i):
        v_t = v_ref[0, 0, c, :, v_lane0:v_lane0 + tk]
        pv = jnp.dot(v_t, p_buf[i % 2][...], preferred_element_type=F32)
        acc_ref[i][...] = al_ref[i][...] * acc_ref[i][...] + pv

    for i in range(n_sub):
        m_ref[i][...] = jnp.full(m_ref[i].shape, -jnp.inf, F32)
        acc_ref[i][...] = jnp.zeros(acc_ref[i].shape, F32)

    def group(c, r):
        for stage, lag in ((value, 3), (prob, 2), (score, 0)):
            dc, i = divmod(r - lag, n_sub)
            if isinstance(c, int) and not 0 <= (c + dc) * n_sub + i < n_total:
                continue
            stage(c + dc, i)

    for g in range(3):
        group(0, g)

    def body(it, _):
        for r in range(3, 3 + MLA_UNROLL * n_sub):
            group(it * MLA_UNROLL, r)
        return 0

    trips = (nk - 1) // MLA_UNROLL
    lax.fori_loop(0, trips, body, 0)
    for g in range(n_sub * MLA_UNROLL * trips + 3, n_total + 3):
        group(0, g)

    for h in range(nh):
        a0 = acc_ref[h][...]
        a1 = acc_ref[nh + h][...]
        o0 = a0[0:MLA_V] / a0[LANES:LANES + 1]
        o1 = a1[MLA_V:LANES] / a1[LANES:LANES + 1]
        o_ref[0, h * wq:(h + 1) * wq, :] = jnp.concatenate([o0, o1], axis=0).T


def _mla_scratch(tq, wq, tk):
    n_sub = 2 * (tq // wq)
    return ([pltpu.VMEM((tk, wq), F32)] * min(n_sub, MLA_S_SLOTS) + [pltpu.VMEM((tk, wq), BF16)] * 2
            + [pltpu.VMEM((V_EXT_ROWS, wq), F32)] * n_sub + [pltpu.VMEM((1, wq), F32)] * (2 * n_sub))


def _mla_attention(q_t, kc, v_t, n_latent):
    b_, t_, _ = kc.shape
    n_pairs = MLA_HEADS // 2
    tq, wq, tk = MLA_TQ, MLA_WQ, MLA_TK
    n_ctx = t_ - n_latent
    nk = t_ // tk
    o_lat = pl.pallas_call(
        functools.partial(_mla_attn_t_kernel, tq=tq, wq=wq, tk=tk, nk=nk, v_lane0=0),
        grid=(b_, n_pairs, n_latent // tq),
        in_specs=[
            pl.BlockSpec((1, 2 * LANES, tq), lambda b, p, i: (b, p, i)),
            pl.BlockSpec((1, t_, 2 * LANES), lambda b, p, i: (b, 0, p)),
            pl.BlockSpec((1, 1, nk, V_EXT_ROWS, tk), lambda b, p, i: (b, p, 0, 0, 0)),
        ],
        out_specs=pl.BlockSpec((1, tq, LANES), lambda b, p, i: (b, i, p)),
        out_shape=jax.ShapeDtypeStruct((b_, n_latent, MLA_WIDTH), F32),
        scratch_shapes=_mla_scratch(tq, wq, tk),
        compiler_params=_cparams(("parallel", "parallel", "arbitrary")),
        name="mla_attn_latent",
    )(q_t, kc, v_t)
    cb = n_latent // n_ctx
    o_ctx = pl.pallas_call(
        functools.partial(_mla_attn_t_kernel, tq=n_ctx, wq=n_ctx, tk=n_ctx, nk=1, v_lane0=tk - n_ctx),
        grid=(b_, n_pairs),
        in_specs=[
            pl.BlockSpec((1, 2 * LANES, n_ctx), lambda b, p: (b, p, cb)),
            pl.BlockSpec((1, n_ctx, 2 * LANES), lambda b, p: (b, cb, p)),
            pl.BlockSpec((1, 1, 1, V_EXT_ROWS, tk), lambda b, p: (b, p, nk - 1, 0, 0)),
        ],
        out_specs=pl.BlockSpec((1, n_ctx, LANES), lambda b, p: (b, 0, p)),
        out_shape=jax.ShapeDtypeStruct((b_, n_ctx, MLA_WIDTH), F32),
        scratch_shapes=_mla_scratch(n_ctx, n_ctx, n_ctx),
        compiler_params=_cparams(("parallel", "parallel")),
        name="mla_attn_ctx",
    )(q_t, kc, v_t)
    return jnp.concatenate([o_lat, o_ctx], axis=1)


def _lru_kernel(x_ref, cw_ref, cb_ref, wg_ref, bg_ref, sp_ref, r_ref, hf_ref, hb_ref,
                *, tc, n_latent, n_ctx):
    nl = n_latent // tc
    ncx = n_ctx // tc
    row = lax.broadcasted_iota(jnp.int32, (tc, 1), 0)

    def conv_chunk(c0, seg_lo, seg_hi):
        xc = x_ref[0, pl.ds(c0, tc), :]
        has_prev = c0 > seg_lo
        has_next = c0 + tc < seg_hi
        p0 = pl.multiple_of(jnp.where(has_prev, c0 - 8, c0), 8)
        n0 = pl.multiple_of(jnp.where(has_next, c0 + tc, c0), 8)
        prev = jnp.where(has_prev, x_ref[0, pl.ds(p0, 8), :], 0.0)
        nxt = jnp.where(has_next, x_ref[0, pl.ds(n0, 8), :], 0.0)
        xm1 = jnp.where(row >= 1, pltpu.roll(xc, 1, 0), prev[7:8])
        xm2 = jnp.where(row >= 2, pltpu.roll(xc, 2, 0), jnp.where(row == 1, prev[7:8], prev[6:7]))
        xp1 = jnp.where(row <= tc - 2, pltpu.roll(xc, tc - 1, 0), nxt[0:1])
        return (cw_ref[0:1] * xm2 + cw_ref[1:2] * xm1 + cw_ref[2:3] * xc + cw_ref[3:4] * xp1
                + cb_ref[...])

    def coeffs(u, d):
        z = jnp.dot(u.astype(BF16), wg_ref[d], preferred_element_type=F32) + bg_ref[d]
        r = jax.nn.sigmoid(z[:, :LANES])
        i = jax.nn.sigmoid(z[:, LANES:])
        log_a = -LRU_C * r * sp_ref[d]
        a = jnp.exp(log_a)
        t = jnp.tanh(log_a)
        uu = jnp.sqrt(-2.0 * t / (1.0 - t)) * (i * u)
        return a, uu

    def scan_chunk(a, u, h_in, reverse):
        s = 1
        while s < tc:
            if reverse:
                keep = row < tc - s
                a_s = jnp.where(keep, pltpu.roll(a, tc - s, 0), 1.0)
                u_s = jnp.where(keep, pltpu.roll(u, tc - s, 0), 0.0)
            else:
                keep = row >= s
                a_s = jnp.where(keep, pltpu.roll(a, s, 0), 1.0)
                u_s = jnp.where(keep, pltpu.roll(u, s, 0), 0.0)
            u = a * u_s + u
            a = a * a_s
            s *= 2
        return a * h_in + u

    def chunk_start(i):
        return pl.multiple_of(jnp.where(i < ncx, n_latent + i * tc, (i - ncx) * tc), tc)

    def seg_bounds(i):
        lo = jnp.where(i < ncx, n_latent, 0)
        hi = jnp.where(i < ncx, n_latent + n_ctx, n_latent)
        return lo, hi

    hf_ref[...] = jnp.zeros_like(hf_ref)
    hb_ref[...] = jnp.zeros_like(hb_ref)

    def fwd(i, _):
        c0 = chunk_start(i)
        lo, hi = seg_bounds(i)
        a, uu = coeffs(conv_chunk(c0, lo, hi), 0)
        h = scan_chunk(a, uu, hf_ref[0:1], False)
        hf_ref[...] = jnp.broadcast_to(h[tc - 1:tc], hf_ref.shape)
        r_ref[0, pl.ds(c0, tc), :] = h
        return 0

    lax.fori_loop(0, ncx + nl, fwd, 0)

    def bwd(i, _):
        ii = jnp.where(i < ncx, ncx - 1 - i, ncx + (nl - 1 - (i - ncx)))
        c0 = chunk_start(ii)
        lo, hi = seg_bounds(ii)
        a, uu = coeffs(conv_chunk(c0, lo, hi), 1)
        h = scan_chunk(a, uu, hb_ref[0:1], True)
        hb_ref[...] = jnp.broadcast_to(h[0:1], hb_ref.shape)
        r_ref[0, pl.ds(c0, tc), :] = r_ref[0, pl.ds(c0, tc), :] + h
        return 0

    lax.fori_loop(0, ncx + nl, bwd, 0)


def _lru(x_lru, wts, n_latent):
    b_, t_, w = x_lru.shape
    ng = w // LANES
    kern = functools.partial(_lru_kernel, tc=LRU_CHUNK, n_latent=n_latent, n_ctx=t_ - n_latent)
    return pl.pallas_call(
        kern,
        grid=(b_, ng),
        in_specs=[
            pl.BlockSpec((1, t_, LANES), lambda b, g: (b, 0, g)),
            pl.BlockSpec((LRU_CONV, LANES), lambda b, g: (0, g)),
            pl.BlockSpec((1, LANES), lambda b, g: (0, g)),
            pl.BlockSpec((2, None, LANES, 2 * LANES), lambda b, g: (0, g, 0, 0)),
            pl.BlockSpec((2, None, 1, 2 * LANES), lambda b, g: (0, g, 0, 0)),
            pl.BlockSpec((2, None, 1, LANES), lambda b, g: (0, g, 0, 0)),
        ],
        out_specs=pl.BlockSpec((1, t_, LANES), lambda b, g: (b, 0, g)),
        out_shape=jax.ShapeDtypeStruct((b_, t_, w), F32),
        scratch_shapes=[pltpu.VMEM((8, LANES), F32), pltpu.VMEM((8, LANES), F32)],
        compiler_params=_cparams(("parallel", "parallel")),
        name="rglru",
    )(x_lru, wts["conv_w"], wts["conv_b"], wts["w_gate"], wts["b_gate"], wts["softplus"])


def _residual(x_ref, y, modb_ref, modc_ref, is_ctx, d):
    gate = jnp.where(is_ctx, modc_ref[0, :, 2 * d:3 * d], modb_ref[0, :, 2 * d:3 * d])
    return x_ref[0] + gate * y


def _even_post_kernel(x_ref, modb_ref, modc_ref, o_ref, gm_ref, r_ref, gl_ref, wo_ref, out_ref,
                      *, tm, n_latent, d):
    is_ctx = _row_is_ctx(tm, n_latent)
    a = (o_ref[0] * gm_ref[0]).astype(BF16)
    bb = (r_ref[0] * gl_ref[0]).astype(BF16)
    y = (jnp.dot(a, wo_ref[0:MLA_WIDTH, :], preferred_element_type=F32)
         + jnp.dot(bb, wo_ref[MLA_WIDTH:, :], preferred_element_type=F32))
    out_ref[0] = _residual(x_ref, y, modb_ref, modc_ref, is_ctx, d)


def _even_post(xc, modb, modc, o, gm, r, gl, w_out, n_latent):
    b_, t_, d = xc.shape
    tm = ROW_TILE
    tok = lambda w: pl.BlockSpec((1, tm, w), lambda b, t: (b, t, 0))
    return pl.pallas_call(
        functools.partial(_even_post_kernel, tm=tm, n_latent=n_latent, d=d),
        grid=(b_, t_ // tm),
        in_specs=[
            tok(d),
            pl.BlockSpec((1, 1, 3 * d), lambda b, t: (b, 0, 0)),
            _const_spec((1, 1, 3 * d)),
            tok(MLA_WIDTH), tok(MLA_WIDTH), tok(LRU_WIDTH), tok(LRU_WIDTH),
            _const_spec(w_out.shape),
        ],
        out_specs=tok(d),
        out_shape=jax.ShapeDtypeStruct((b_, t_, d), F32),
        compiler_params=_cparams(("parallel", "parallel")),
        name="even_post",
    )(xc, modb, modc, o, gm, r, gl, w_out)


def _odd_post_kernel(x_ref, modb_ref, modc_ref, o_ref, g_ref, wo_ref, fg_ref, out_ref,
                     *, tm, n_latent, d, final):
    is_ctx = _row_is_ctx(tm, n_latent)
    a = (o_ref[0] * g_ref[0]).astype(BF16)
    y = jnp.dot(a, wo_ref[...], preferred_element_type=F32)
    xn = _residual(x_ref, y, modb_ref, modc_ref, is_ctx, d)
    if final:
        xn = _rms(xn, fg_ref[...])
    out_ref[0] = xn


def _odd_post(xc, modb, modc, o, g, w_out, final_g, n_latent, final):
    b_, t_, d = xc.shape
    tm = FINAL_ROW_TILE if final else ROW_TILE
    n_rows = n_latent if final else t_
    tok = lambda w: pl.BlockSpec((1, tm, w), lambda b, t: (b, t, 0))
    return pl.pallas_call(
        functools.partial(_odd_post_kernel, tm=tm, n_latent=n_latent, d=d, final=final),
        grid=(b_, n_rows // tm),
        in_specs=[
            tok(d),
            pl.BlockSpec((1, 1, 3 * d), lambda b, t: (b, 0, 0)),
            _const_spec((1, 1, 3 * d)),
            tok(NA_WIDTH), tok(NA_WIDTH),
            _const_spec(w_out.shape),
            _const_spec((1, d)),
        ],
        out_specs=tok(d),
        out_shape=jax.ShapeDtypeStruct((b_, n_rows, d), F32),
        compiler_params=_cparams(("parallel", "parallel")),
        name="odd_post_final" if final else "odd_post",
    )(xc, modb, modc, o, g, w_out, final_g)


def _odd_pre_kernel(x_ref, modb_ref, modc_ref, g_ref, win_ref, q_out, k_out, v_out, g_out,
                    *, tm, n_latent, d):
    is_ctx = _row_is_ctx(tm, n_latent)
    h = _modulated_norm(x_ref[0], g_ref[...], modb_ref, modc_ref, is_ctx, d).astype(BF16)
    w = NA_WIDTH
    hd = NA_HEAD_DIM
    q_t = (jnp.dot(h, win_ref[:, 0:w], preferred_element_type=F32) * (hd ** -0.5 * LOG2E)).T
    zeros = jnp.zeros((hd, tm), BF16)
    for hh in range(NA_HEADS):
        qh = q_t[hh * hd:(hh + 1) * hd].astype(BF16)
        q_out[0, hh * LANES:(hh + 1) * LANES, :] = jnp.concatenate(
            [qh, zeros] if hh % 2 == 0 else [zeros, qh], axis=0)
    k_out[0] = jnp.dot(h, win_ref[:, w:2 * w], preferred_element_type=F32).astype(BF16)
    v_t = jnp.dot(h, win_ref[:, 2 * w:3 * w], preferred_element_type=F32).T
    ones_rows = jnp.ones((V_EXT_ROWS - LANES, NA_VCHUNK), BF16)
    for p in range(NA_HEADS // 2):
        for cc in range(tm // NA_VCHUNK):
            v_out[0, p, cc, 0:LANES, :] = v_t[p * LANES:(p + 1) * LANES,
                                              cc * NA_VCHUNK:(cc + 1) * NA_VCHUNK].astype(BF16)
            v_out[0, p, cc, LANES:V_EXT_ROWS, :] = ones_rows
    g_out[0] = _silu(jnp.dot(h, win_ref[:, 3 * w:4 * w], preferred_element_type=F32))


def _odd_pre_call(xc, modb, modc, g, w_in, n_latent):
    b_, t_, d = xc.shape
    tm = ROW_TILE
    tok = lambda w: pl.BlockSpec((1, tm, w), lambda b, t: (b, t, 0))
    return pl.pallas_call(
        functools.partial(_odd_pre_kernel, tm=tm, n_latent=n_latent, d=d),
        grid=(b_, t_ // tm),
        in_specs=[
            tok(d),
            pl.BlockSpec((1, 1, 3 * d), lambda b, t: (b, 0, 0)),
            _const_spec((1, 1, 3 * d)),
            _const_spec((1, d)),
            _const_spec(w_in.shape),
        ],
        out_specs=[
            pl.BlockSpec((1, NA_HEADS * LANES, tm), lambda b, t: (b, 0, t)),
            tok(NA_WIDTH),
            pl.BlockSpec((1, NA_HEADS // 2, tm // NA_VCHUNK, V_EXT_ROWS, NA_VCHUNK),
                         lambda b, t: (b, 0, t, 0, 0)),
            tok(NA_WIDTH),
        ],
        out_shape=[
            jax.ShapeDtypeStruct((b_, NA_HEADS * LANES, t_), BF16),
            jax.ShapeDtypeStruct((b_, t_, NA_WIDTH), BF16),
            jax.ShapeDtypeStruct((b_, NA_HEADS // 2, t_ // NA_VCHUNK, V_EXT_ROWS, NA_VCHUNK), BF16),
            jax.ShapeDtypeStruct((b_, t_, NA_WIDTH), F32),
        ],
        compiler_params=_cparams(("parallel", "parallel")),
        name="odd_pre",
    )(xc, modb, modc, g, w_in)


def _na_kernel(q_ref, k_ref, v_ref, *rest, nq, nkl, n_ctx, rows, k_ctx0, v_ctx0):
    local = nkl > 0
    if local:
        bias_ref, o_ref = rest[0:2]
        scratch = rest[2:]
        k_row0 = jnp.clip(pl.program_id(2) * NA_QROWS - NA_WIN_R // 2, 0, rows - NA_KROWS)
        kstart = pl.multiple_of(k_row0 * GRID_W, NA_VCHUNK)
        v_loc0 = k_row0 * GRID_W // NA_VCHUNK
    else:
        o_ref = rest[0]
        scratch = rest[1:]
    s_buf, p_buf, m_ref = scratch[0:2], scratch[2:4], scratch[4:6]

    def score(j):
        q_t = q_ref[0, j * LANES:(j + 1) * LANES, :]
        s_c = jnp.dot(k_ref[0, k_ctx0:k_ctx0 + n_ctx, :], q_t, preferred_element_type=F32)
        s_buf[j][nkl:nkl + n_ctx, :] = s_c
        m = jnp.max(s_c, axis=0, keepdims=True)
        if local:
            s_l = jnp.dot(k_ref[0, pl.ds(kstart, nkl), :], q_t, preferred_element_type=F32)
            s_l = s_l + bias_ref[0, 0, j]
            s_buf[j][0:nkl, :] = s_l
            m = jnp.maximum(m, jnp.max(s_l, axis=0, keepdims=True))
        m_ref[j][...] = m

    def prob(j):
        m = jnp.broadcast_to(m_ref[j][...], (PROB_ROWS, nq))
        for r in range(0, nkl + n_ctx, PROB_ROWS):
            p_buf[j][r:r + PROB_ROWS, :] = jnp.exp2(s_buf[j][r:r + PROB_ROWS, :] - m).astype(BF16)

    def value(j):
        acc = None
        for cc in range((nkl + n_ctx) // NA_VCHUNK):
            in_window = cc < nkl // NA_VCHUNK
            v_t = v_ref[0, 0, v_loc0 + cc] if in_window else v_ref[0, 0, v_ctx0 + cc - nkl // NA_VCHUNK]
            pv = jnp.dot(v_t, p_buf[j][cc * NA_VCHUNK:(cc + 1) * NA_VCHUNK, :],
                         preferred_element_type=F32)
            acc = pv if acc is None else acc + pv
        return acc

    score(0)
    score(1)
    prob(0)
    a0 = value(0)
    prob(1)
    a1 = value(1)
    o0 = a0[0:NA_HEAD_DIM] / a0[LANES:LANES + 1]
    o1 = a1[NA_HEAD_DIM:LANES] / a1[LANES:LANES + 1]
    o_ref[0] = jnp.concatenate([o0, o1], axis=0).T


def _na_bias_tables(rpb, rows):
    nq = NA_QROWS * GRID_W
    nkl = NA_KROWS * GRID_W
    n_heads = rpb.shape[0]
    pad = GRID_W - NA_WIN_C
    rp = jnp.pad(rpb * LOG2E, ((0, 0), (0, 0), (pad, pad)))
    toep = jnp.stack([rp[:, :, GRID_W - 1 - qc:2 * GRID_W - 1 - qc] for qc in range(GRID_W)], axis=3)
    col = np.arange(GRID_W)
    cs = np.clip(col - NA_WIN_C // 2, 0, GRID_W - NA_WIN_C)
    valid_c = (col[:, None] >= cs[None, :]) & (col[:, None] < cs[None, :] + NA_WIN_C)
    toep = jnp.where(valid_c, toep, NEG_BIG)
    n_dr = 2 * NA_WIN_R - 1
    slabs = jnp.concatenate([toep, jnp.full((n_heads, 1, GRID_W, GRID_W), NEG_BIG, F32)], axis=1)
    idx = np.full((3, NA_KROWS, NA_QROWS), n_dr, np.int32)
    for typ, r0 in enumerate((0, 2 * NA_QROWS, rows - NA_QROWS)):
        ks = int(np.clip(r0 - NA_WIN_R // 2, 0, rows - NA_KROWS))
        for kr in range(NA_KROWS):
            for qr in range(NA_QROWS):
                r = r0 + qr
                rs = int(np.clip(r - NA_WIN_R // 2, 0, rows - NA_WIN_R))
                if rs <= ks + kr < rs + NA_WIN_R:
                    idx[typ, kr, qr] = ks + kr - r + NA_WIN_R - 1
    t = jnp.take(slabs, jnp.asarray(idx), axis=1)
    t = t.reshape(n_heads // 2, 2, 3, NA_KROWS, NA_QROWS, GRID_W, GRID_W)
    return t.transpose(0, 2, 1, 3, 5, 4, 6).reshape(n_heads // 2, 3, 2, nkl, nq)


def _na_scratch(n_keys, nq):
    return ([pltpu.VMEM((n_keys, nq), F32)] * 2 + [pltpu.VMEM((n_keys, nq), BF16)] * 2
            + [pltpu.VMEM((1, nq), F32)] * 2)


def _na_attention(q_t, k, v_t, bias, n_latent, need_ctx):
    b_, t_, _ = k.shape
    n_ctx = t_ - n_latent
    rows = n_latent // GRID_W
    nq = NA_QROWS * GRID_W
    nkl = NA_KROWS * GRID_W
    n_blocks = rows // NA_QROWS
    n_pairs = NA_HEADS // 2
    n_vc = t_ // NA_VCHUNK

    def bias_idx(p, b, i):
        return (p, jnp.where(i == 0, 0, jnp.where(i == n_blocks - 1, 2, 1)), 0, 0, 0)

    o_lat = pl.pallas_call(
        functools.partial(_na_kernel, nq=nq, nkl=nkl, n_ctx=n_ctx, rows=rows, k_ctx0=n_latent,
                          v_ctx0=n_latent // NA_VCHUNK),
        grid=(n_pairs, b_, n_blocks),
        in_specs=[
            pl.BlockSpec((1, 2 * LANES, nq), lambda p, b, i: (b, p, i)),
            pl.BlockSpec((1, t_, LANES), lambda p, b, i: (b, 0, p)),
            pl.BlockSpec((1, 1, n_vc, V_EXT_ROWS, NA_VCHUNK), lambda p, b, i: (b, p, 0, 0, 0)),
            pl.BlockSpec((1, 1, 2, nkl, nq), bias_idx),
        ],
        out_specs=pl.BlockSpec((1, nq, LANES), lambda p, b, i: (b, i, p)),
        out_shape=jax.ShapeDtypeStruct((b_, n_latent, NA_WIDTH), F32),
        scratch_shapes=_na_scratch(nkl + n_ctx, nq),
        compiler_params=_cparams(("parallel", "parallel", "arbitrary")),
        name="na_attn",
    )(q_t, k, v_t, bias)
    if not need_ctx:
        return o_lat
    cb = n_latent // n_ctx
    o_ctx = pl.pallas_call(
        functools.partial(_na_kernel, nq=n_ctx, nkl=0, n_ctx=n_ctx, rows=rows, k_ctx0=0, v_ctx0=0),
        grid=(n_pairs, b_),
        in_specs=[
            pl.BlockSpec((1, 2 * LANES, n_ctx), lambda p, b: (b, p, cb)),
            pl.BlockSpec((1, n_ctx, LANES), lambda p, b: (b, cb, p)),
            pl.BlockSpec((1, 1, n_ctx // NA_VCHUNK, V_EXT_ROWS, NA_VCHUNK), lambda p, b: (b, p, cb, 0, 0)),
        ],
        out_specs=pl.BlockSpec((1, n_ctx, LANES), lambda p, b: (b, 0, p)),
        out_shape=jax.ShapeDtypeStruct((b_, n_ctx, NA_WIDTH), F32),
        scratch_shapes=_na_scratch(n_ctx, n_ctx),
        compiler_params=_cparams(("parallel", "parallel")),
        name="na_attn_ctx",
    )(q_t, k, v_t)
    return jnp.concatenate([o_lat, o_ctx], axis=1)


def _rope_swap_index():
    half = MLA_ROPE // 4
    idx = np.arange(MLA_ROPE)
    return np.where((idx // half) % 2 == 0, idx + half, idx - half)


def _rope_tables(n_latent, n_ctx):
    n_freq = MLA_ROPE // 4
    inv = ROPE_THETA ** (-jnp.arange(n_freq, dtype=F32) / n_freq)
    t = jnp.arange(n_latent, dtype=jnp.int32)
    ang_r = (t // GRID_W).astype(F32)[:, None] * inv
    ang_c = (t % GRID_W).astype(F32)[:, None] * inv
    cr, sr, cc, sc = jnp.cos(ang_r), jnp.sin(ang_r), jnp.cos(ang_c), jnp.sin(ang_c)
    cos32 = jnp.concatenate([cr, cr, cc, cc], axis=-1)
    sin32 = jnp.concatenate([-sr, sr, -sc, sc], axis=-1)
    ones = jnp.ones((n_latent, MLA_NOPE), F32)
    zeros = jnp.zeros((n_latent, LANES - MLA_NOPE - MLA_ROPE), F32)
    cos_l = jnp.concatenate([ones, cos32, zeros], axis=-1)
    sin_l = jnp.concatenate([0 * ones, sin32, zeros], axis=-1)
    cos_c = jnp.concatenate([jnp.ones((n_ctx, MLA_NOPE + MLA_ROPE), F32),
                             jnp.zeros((n_ctx, LANES - MLA_NOPE - MLA_ROPE), F32)], axis=-1)
    sin_c = jnp.zeros((n_ctx, LANES), F32)
    return jnp.concatenate([cos_l, cos_c], axis=0), jnp.concatenate([sin_l, sin_c], axis=0)


def _even_weights(i, ev_w_in, mla_q_norm, mla_w_uq, mla_kv_norm, mla_w_ukv, lru_conv_w, lru_conv_b,
                  lru_wa, lru_ba, lru_wx, lru_bx, lru_lambda, ev_w_out):
    sw = _rope_swap_index()
    d = ev_w_in.shape[1]
    w_in = ev_w_in[i]
    o_cq, o_ckv, o_kr = MLA_Q_RANK, MLA_Q_RANK + MLA_KV_RANK, MLA_Q_RANK + MLA_KV_RANK + MLA_ROPE
    kr = w_in[:, o_ckv:o_kr]
    pad_lo = jnp.zeros((d, MLA_NOPE), F32)
    pad_hi = jnp.zeros((d, LANES - MLA_NOPE - MLA_ROPE), F32)
    w_in_ext = jnp.concatenate(
        [w_in[:, :o_ckv], pad_lo, kr, pad_hi, pad_lo, kr[:, sw], pad_hi, w_in[:, o_kr:]], axis=1)
    wq = mla_w_uq[i].reshape(MLA_Q_RANK, MLA_HEADS, MLA_NOPE + MLA_ROPE)
    zq = jnp.zeros((MLA_Q_RANK, MLA_HEADS, LANES - MLA_NOPE - MLA_ROPE), F32)
    wqa = jnp.concatenate([wq, zq], axis=-1).reshape(MLA_Q_RANK, MLA_HEADS * LANES)
    wqb = jnp.concatenate([jnp.zeros((MLA_Q_RANK, MLA_HEADS, MLA_NOPE), F32),
                           wq[:, :, MLA_NOPE:][:, :, sw], zq], axis=-1).reshape(MLA_Q_RANK, MLA_HEADS * LANES)
    wkv = mla_w_ukv[i].reshape(MLA_KV_RANK, MLA_HEADS, MLA_NOPE + MLA_V)
    wk = jnp.concatenate([wkv[:, :, :MLA_NOPE], jnp.zeros((MLA_KV_RANK, MLA_HEADS, LANES - MLA_NOPE), F32)],
                         axis=-1).reshape(MLA_KV_RANK, MLA_HEADS * LANES)
    wv = wkv[:, :, MLA_NOPE:].reshape(MLA_KV_RANK, MLA_WIDTH)
    ng = LRU_WIDTH // LANES
    per = LANES // LRU_BLOCK

    def blockdiag(w):
        w = w.reshape(2, ng, per, LRU_BLOCK, LRU_BLOCK)
        eye = jnp.eye(per, dtype=F32)
        return jnp.einsum('dgpkj,pq->dgpkqj', w, eye).reshape(2, ng, LANES, LANES)

    w_gate = jnp.concatenate([blockdiag(lru_wa[i]), blockdiag(lru_wx[i])], axis=-1).astype(BF16)
    b_gate = jnp.concatenate([lru_ba[i].reshape(2, ng, 1, LANES), lru_bx[i].reshape(2, ng, 1, LANES)], axis=-1)
    return dict(
        w_in=w_in_ext.astype(BF16), q_norm=mla_q_norm[i][None], wqa=wqa.astype(BF16), wqb=wqb.astype(BF16),
        kv_norm=mla_kv_norm[i][None], wk=wk.astype(BF16), wv=wv.astype(BF16),
        conv_w=lru_conv_w[i], conv_b=lru_conv_b[i][None], w_gate=w_gate, b_gate=b_gate,
        softplus=jax.nn.softplus(-lru_lambda[i]).reshape(2, ng, 1, LANES),
        w_out=ev_w_out[i].astype(BF16))


def kernel(x, c, ctx, c_ctx, ada_w, ada_b, norm_g, ev_w_in, mla_q_norm, mla_w_uq, mla_kv_norm, mla_w_ukv,
           lru_conv_w, lru_conv_b, lru_wa, lru_ba, lru_wx, lru_bx, lru_lambda, ev_w_out, od_w_in, na_rpb,
           od_w_out, final_norm_g):
    b_, s_, d = x.shape
    n_ctx = ctx.shape[1]
    depth = ada_w.shape[0]
    rows = s_ // GRID_W
    assert b_ <= 7 and s_ % GRID_W == 0 and rows % NA_QROWS == 0 and rows >= NA_KROWS
    assert (s_ + n_ctx) % ROW_TILE == 0 and s_ % FINAL_ROW_TILE == 0 and s_ % MLA_TQ == 0
    assert ROW_TILE == MLA_TK and n_ctx <= MLA_TK and s_ % n_ctx == 0 and n_ctx % LRU_CHUNK == 0 and s_ % LRU_CHUNK == 0
    assert n_ctx % NA_VCHUNK == 0 and ROW_TILE % NA_VCHUNK == 0 and rows >= 3 * NA_QROWS
    assert (NA_QROWS // 2 * GRID_W) % NA_VCHUNK == 0 and depth % 2 == 0

    cond = jnp.zeros((8, d), F32).at[:b_].set(c).at[b_].set(c_ctx)
    mod = _adaln(cond, ada_w, ada_b)
    xc = jnp.concatenate([x, ctx], axis=1)
    cos_t, sin_t = _rope_tables(s_, n_ctx)

    for layer in range(depth):
        need_ctx = layer < depth - 1
        i = layer // 2
        modb = mod[layer, :b_][:, None, :]
        modc = mod[layer, b_][None, None, :]
        g = norm_g[layer][None]
        if layer % 2 == 0:
            wts = _even_weights(i, ev_w_in, mla_q_norm, mla_w_uq, mla_kv_norm, mla_w_ukv, lru_conv_w,
                                lru_conv_b, lru_wa, lru_ba, lru_wx, lru_bx, lru_lambda, ev_w_out)
            q_t, kc, v_t, gm, xl, gl = _even_pre(xc, modb, modc, g, wts, cos_t, sin_t, s_)
            o = _mla_attention(q_t, kc, v_t, s_)
            r = _lru(xl, wts, s_)
            xc = _even_post(xc, modb, modc, o, gm, r, gl, wts["w_out"], s_)
        else:
            q, k, v, gg = _odd_pre_call(xc, modb, modc, g, od_w_in[i].astype(BF16), s_)
            bias = _na_bias_tables(na_rpb[i], rows)
            o = _na_attention(q, k, v, bias, s_, need_ctx)
            xc = _odd_post(xc, modb, modc, o, gg, od_w_out[i].astype(BF16), final_norm_g[None], s_,
                           final=not need_ctx)
    return xc
```

```python
import functools
import math

import numpy as np
import jax
import jax.numpy as jnp
from jax import lax
from jax.experimental import pallas as pl
from jax.experimental.pallas import tpu as pltpu

F32 = jnp.float32
BF16 = jnp.bfloat16

GRID_W = 64
RMS_EPS = 1e-6
ROPE_THETA = 10000.0
MLA_HEADS = 8
MLA_NOPE = 64
MLA_ROPE = 32
MLA_V = 64
MLA_Q_RANK = 256
MLA_KV_RANK = 128
MLA_WIDTH = MLA_HEADS * MLA_V
MLA_SCALE = (MLA_NOPE + MLA_ROPE) ** -0.5
LRU_WIDTH = 512
LRU_BLOCKS = 8
LRU_BLOCK = LRU_WIDTH // LRU_BLOCKS
LRU_CONV = 4
LRU_C = 8.0
NA_HEADS = 16
NA_HEAD_DIM = 64
NA_WIDTH = NA_HEADS * NA_HEAD_DIM
NA_WIN_R = 8
NA_WIN_C = 16

LANES = 128
LOG2E = 1.4426950408889634
NEG_BIG = -1e30
VMEM_LIMIT = 56 * 1024 * 1024

ROW_TILE = 768
FINAL_ROW_TILE = 512
MLA_TQ = 1024
MLA_S_SLOTS = 4
MLA_WQ = 512
MLA_TK = 768
MLA_UNROLL = 2
PROB_ROWS = 16
V_EXT_ROWS = LANES + 16
NA_QROWS = 8
NA_KROWS = 16
NA_VCHUNK = 256
LRU_CHUNK = 256


def _cparams(sem, flags=None):
    return pltpu.CompilerParams(dimension_semantics=sem, vmem_limit_bytes=VMEM_LIMIT, flags=flags)


def _silu(v):
    return v * jax.nn.sigmoid(v)


def _adaln_kernel(cond_ref, w_ref, b_ref, o_ref):
    a = _silu(cond_ref[...])
    o_ref[0] = jnp.dot(a, w_ref[0], preferred_element_type=F32) + b_ref[0]


def _adaln(cond, ada_w, ada_b):
    depth, d, d3 = ada_w.shape
    tn = 1024
    return pl.pallas_call(
        _adaln_kernel,
        grid=(depth, d3 // tn),
        in_specs=[
            pl.BlockSpec((8, d), lambda l, n: (0, 0)),
            pl.BlockSpec((1, d, tn), lambda l, n: (l, 0, n)),
            pl.BlockSpec((1, 1, tn), lambda l, n: (l, 0, n)),
        ],
        out_specs=pl.BlockSpec((1, 8, tn), lambda l, n: (l, 0, n)),
        out_shape=jax.ShapeDtypeStruct((depth, 8, d3), F32),
        compiler_params=_cparams(("arbitrary", "arbitrary")),
        name="adaln",
    )(cond, ada_w, ada_b.reshape(depth, 1, d3))


def _modulated_norm(x, g, modb_ref, modc_ref, is_ctx, d):
    ms = jnp.mean(x * x, axis=-1, keepdims=True)
    y = x * lax.rsqrt(ms + RMS_EPS) * g
    shift = jnp.where(is_ctx, modc_ref[0, :, 0:d], modb_ref[0, :, 0:d])
    scale = jnp.where(is_ctx, modc_ref[0, :, d:2 * d], modb_ref[0, :, d:2 * d])
    return y * (1.0 + scale) + shift


def _row_is_ctx(tm, n_latent):
    row = pl.program_id(1) * tm + lax.broadcasted_iota(jnp.int32, (tm, 1), 0)
    return row >= n_latent


def _rms(v, g):
    ms = jnp.mean(v * v, axis=-1, keepdims=True)
    return v * lax.rsqrt(ms + RMS_EPS) * g


def _even_pre_kernel(x_ref, modb_ref, modc_ref, g_ref, win_ref, qn_ref, wqa_ref, wqb_ref,
                     kvn_ref, wk_ref, wv_ref, cos_ref, sin_ref,
                     q_out, k_out, v_out, gm_out, xl_out, gl_out, *, tm, n_latent, d):
    is_ctx = _row_is_ctx(tm, n_latent)
    h = _modulated_norm(x_ref[0], g_ref[...], modb_ref, modc_ref, is_ctx, d).astype(BF16)

    def proj(lo, hi):
        return jnp.dot(h, win_ref[:, lo:hi], preferred_element_type=F32)

    cos = cos_ref[...]
    sin = sin_ref[...]
    cq = _rms(proj(0, 256), qn_ref[...]).astype(BF16)
    qa = jnp.dot(cq, wqa_ref[...], preferred_element_type=F32)
    qb = jnp.dot(cq, wqb_ref[...], preferred_element_type=F32)
    ckv = _rms(proj(256, 384), kvn_ref[...]).astype(BF16)
    kk = jnp.dot(ckv, wk_ref[...], preferred_element_type=F32)
    k_rope = proj(384, 512) * cos + proj(512, 640) * sin
    v_t = jnp.dot(ckv, wv_ref[...], preferred_element_type=F32).T
    ones_rows = jnp.ones((V_EXT_ROWS - LANES, tm), BF16)
    for p in range(MLA_HEADS // 2):
        v_out[0, p, 0, 0:LANES, :] = v_t[p * LANES:(p + 1) * LANES].astype(BF16)
        v_out[0, p, 0, LANES:V_EXT_ROWS, :] = ones_rows
    for hh in range(MLA_HEADS):
        sl = slice(hh * LANES, (hh + 1) * LANES)
        q = (qa[:, sl] * cos + qb[:, sl] * sin) * (MLA_SCALE * LOG2E)
        q_out[0, sl, :] = q.T.astype(BF16)
        k_out[0, :, sl] = (kk[:, sl] + k_rope).astype(BF16)
    gm_out[0] = _silu(proj(640, 1152))
    xl_out[0] = proj(1152, 1664)
    gl_out[0] = _silu(proj(1664, 2176))


def _const_spec(shape):
    nd = len(shape)
    return pl.BlockSpec(shape, lambda b, t: (0,) * nd)


def _even_pre(xc, modb, modc, g, wts, cos_t, sin_t, n_latent):
    b_, t_, d = xc.shape
    tm = ROW_TILE
    kern = functools.partial(_even_pre_kernel, tm=tm, n_latent=n_latent, d=d)
    tok = lambda w: pl.BlockSpec((1, tm, w), lambda b, t: (b, t, 0))
    hw = MLA_HEADS * LANES
    return pl.pallas_call(
        kern,
        grid=(b_, t_ // tm),
        in_specs=[
            tok(d),
            pl.BlockSpec((1, 1, 3 * d), lambda b, t: (b, 0, 0)),
            _const_spec((1, 1, 3 * d)),
            _const_spec((1, d)),
            _const_spec(wts["w_in"].shape),
            _const_spec((1, MLA_Q_RANK)),
            _const_spec(wts["wqa"].shape),
            _const_spec(wts["wqb"].shape),
            _const_spec((1, MLA_KV_RANK)),
            _const_spec(wts["wk"].shape),
            _const_spec(wts["wv"].shape),
            pl.BlockSpec((tm, LANES), lambda b, t: (t, 0)),
            pl.BlockSpec((tm, LANES), lambda b, t: (t, 0)),
        ],
        out_specs=[pl.BlockSpec((1, hw, tm), lambda b, t: (b, 0, t)), tok(hw),
                   pl.BlockSpec((1, MLA_HEADS // 2, 1, V_EXT_ROWS, tm), lambda b, t: (b, 0, t, 0, 0)),
                   tok(MLA_WIDTH), tok(LRU_WIDTH), tok(LRU_WIDTH)],
        out_shape=[
            jax.ShapeDtypeStruct((b_, hw, t_), BF16),
            jax.ShapeDtypeStruct((b_, t_, hw), BF16),
            jax.ShapeDtypeStruct((b_, MLA_HEADS // 2, t_ // tm, V_EXT_ROWS, tm), BF16),
            jax.ShapeDtypeStruct((b_, t_, MLA_WIDTH), F32),
            jax.ShapeDtypeStruct((b_, t_, LRU_WIDTH), F32),
            jax.ShapeDtypeStruct((b_, t_, LRU_WIDTH), F32),
        ],
        compiler_params=_cparams(("parallel", "parallel")),
        name="even_pre",
    )(xc, modb, modc, g, wts["w_in"], wts["q_norm"], wts["wqa"], wts["wqb"], wts["kv_norm"],
      wts["wk"], wts["wv"], cos_t, sin_t)


def _mla_attn_t_kernel(q_ref, k_ref, v_ref, o_ref, *scratch, tq, wq, tk, nk, v_lane0):
    nh = tq // wq
    n_sub = 2 * nh
    n_total = n_sub * nk
    ns = min(n_sub, MLA_S_SLOTS)
    s_buf, p_buf = scratch[0:ns], scratch[ns:ns + 2]
    acc_ref, m_ref, al_ref = (scratch[ns + 2 + r * n_sub:ns + 2 + (r + 1) * n_sub] for r in range(3))

    def score(c, i):
        j, h = divmod(i, nh)
        start = c * tk if isinstance(c, int) else pl.multiple_of(c * tk, LANES)
        k = k_ref[0, pl.ds(start, tk), j * LANES:(j + 1) * LANES]
        q_t = q_ref[0, j * LANES:(j + 1) * LANES, h * wq:(h + 1) * wq]
        s = jnp.dot(k, q_t, preferred_element_type=F32)
        m = m_ref[i][...]
        m_new = jnp.maximum(m, jnp.max(s, axis=0, keepdims=True))
        al_ref[i][...] = jnp.exp2(m - m_new)
        m_ref[i][...] = m_new
        s_buf[i % ns][...] = s

    def prob(c, i):
        m = jnp.broadcast_to(m_ref[i][...], (PROB_ROWS, wq))
        for r in range(0, tk, PROB_ROWS):
            p = jnp.exp2(s_buf[i % ns][r:r + PROB_ROWS, :] - m)
            p_buf[i % 2][r:r + PROB_ROWS, :] = p.astype(BF16)

    def value(c, i):
        v_t = v_ref[0, 0, c, :, v_lane0:v_lane0 + tk]
        pv = jnp.dot(v_t, p_buf[i % 2][...], preferred_element_type=F32)
        acc_ref[i][...] = al_ref[i][...] * acc_ref[i][...] + pv

    for i in range(n_sub):
        m_ref[i][...] = jnp.full(m_ref[i].shape, -jnp.inf, F32)
        acc_ref[i][...] = jnp.zeros(acc_ref[i].shape, F32)

    def group(c, r):
        for stage, lag in ((value, 3), (prob, 2), (score, 0)):
            dc, i = divmod(r - lag, n_sub)
            if isinstance(c, int) and not 0 <= (c + dc) * n_sub + i < n_total:
                continue
            stage(c + dc, i)

    for g in range(3):
        group(0, g)

    def body(it, _):
        for r in range(3, 3 + MLA_UNROLL * n_sub):
            group(it * MLA_UNROLL, r)
        return 0

    trips = (nk - 1) // MLA_UNROLL
    lax.fori_loop(0, trips, body, 0)
    for g in range(n_sub * MLA_UNROLL * trips + 3, n_total + 3):
        group(0, g)

    for h in range(nh):
        a0 = acc_ref[h][...]
        a1 = acc_ref[nh + h][...]
        o0 = a0[0:MLA_V] / a0[LANES:LANES + 1]
        o1 = a1[MLA_V:LANES] / a1[LANES:LANES + 1]
        o_ref[0, h * wq:(h + 1) * wq, :] = jnp.concatenate([o0, o1], axis=0).T


def _mla_scratch(tq, wq, tk):
    n_sub = 2 * (tq // wq)
    return ([pltpu.VMEM((tk, wq), F32)] * min(n_sub, MLA_S_SLOTS) + [pltpu.VMEM((tk, wq), BF16)] * 2
            + [pltpu.VMEM((V_EXT_ROWS, wq), F32)] * n_sub + [pltpu.VMEM((1, wq), F32)] * (2 * n_sub))


def _mla_attention(q_t, kc, v_t, n_latent):
    b_, t_, _ = kc.shape
    n_pairs = MLA_HEADS // 2
    tq, wq, tk = MLA_TQ, MLA_WQ, MLA_TK
    n_ctx = t_ - n_latent
    nk = t_ // tk
    o_lat = pl.pallas_call(
        functools.partial(_mla_attn_t_kernel, tq=tq, wq=wq, tk=tk, nk=nk, v_lane0=0),
        grid=(b_, n_pairs, n_latent // tq),
        in_specs=[
            pl.BlockSpec((1, 2 * LANES, tq), lambda b, p, i: (b, p, i)),
            pl.BlockSpec((1, t_, 2 * LANES), lambda b, p, i: (b, 0, p)),
            pl.BlockSpec((1, 1, nk, V_EXT_ROWS, tk), lambda b, p, i: (b, p, 0, 0, 0)),
        ],
        out_specs=pl.BlockSpec((1, tq, LANES), lambda b, p, i: (b, i, p)),
        out_shape=jax.ShapeDtypeStruct((b_, n_latent, MLA_WIDTH), F32),
        scratch_shapes=_mla_scratch(tq, wq, tk),
        compiler_params=_cparams(("parallel", "parallel", "arbitrary")),
        name="mla_attn_latent",
    )(q_t, kc, v_t)
    cb = n_latent // n_ctx
    o_ctx = pl.pallas_call(
        functools.partial(_mla_attn_t_kernel, tq=n_ctx, wq=n_ctx, tk=n_ctx, nk=1, v_lane0=tk - n_ctx),
        grid=(b_, n_pairs),
        in_specs=[
            pl.BlockSpec((1, 2 * LANES, n_ctx), lambda b, p: (b, p, cb)),
            pl.BlockSpec((1, n_ctx, 2 * LANES), lambda b, p: (b, cb, p)),
            pl.BlockSpec((1, 1, 1, V_EXT_ROWS, tk), lambda b, p: (b, p, nk - 1, 0, 0)),
        ],
        out_specs=pl.BlockSpec((1, n_ctx, LANES), lambda b, p: (b, 0, p)),
        out_shape=jax.ShapeDtypeStruct((b_, n_ctx, MLA_WIDTH), F32),
        scratch_shapes=_mla_scratch(n_ctx, n_ctx, n_ctx),
        compiler_params=_cparams(("parallel", "parallel")),
        name="mla_attn_ctx",
    )(q_t, kc, v_t)
    return jnp.concatenate([o_lat, o_ctx], axis=1)


def _lru_kernel(x_ref, cw_ref, cb_ref, wg_ref, bg_ref, sp_ref, r_ref, hf_ref, hb_ref,
                *, tc, n_latent, n_ctx):
    nl = n_latent // tc
    ncx = n_ctx // tc
    row = lax.broadcasted_iota(jnp.int32, (tc, 1), 0)

    def conv_chunk(c0, seg_lo, seg_hi):
        xc = x_ref[0, pl.ds(c0, tc), :]
        has_prev = c0 > seg_lo
        has_next = c0 + tc < seg_hi
        p0 = pl.multiple_of(jnp.where(has_prev, c0 - 8, c0), 8)
        n0 = pl.multiple_of(jnp.where(has_next, c0 + tc, c0), 8)
        prev = jnp.where(has_prev, x_ref[0, pl.ds(p0, 8), :], 0.0)
        nxt = jnp.where(has_next, x_ref[0, pl.ds(n0, 8), :], 0.0)
        xm1 = jnp.where(row >= 1, pltpu.roll(xc, 1, 0), prev[7:8])
        xm2 = jnp.where(row >= 2, pltpu.roll(xc, 2, 0), jnp.where(row == 1, prev[7:8], prev[6:7]))
        xp1 = jnp.where(row <= tc - 2, pltpu.roll(xc, tc - 1, 0), nxt[0:1])
        return (cw_ref[0:1] * xm2 + cw_ref[1:2] * xm1 + cw_ref[2:3] * xc + cw_ref[3:4] * xp1
                + cb_ref[...])

    def coeffs(u, d):
        z = jnp.dot(u.astype(BF16), wg_ref[d], preferred_element_type=F32) + bg_ref[d]
        r = jax.nn.sigmoid(z[:, :LANES])
        i = jax.nn.sigmoid(z[:, LANES:])
        log_a = -LRU_C * r * sp_ref[d]
        a = jnp.exp(log_a)
        t = jnp.tanh(log_a)
        uu = jnp.sqrt(-2.0 * t / (1.0 - t)) * (i * u)
        return a, uu

    def scan_chunk(a, u, h_in, reverse):
        s = 1
        while s < tc:
            if reverse:
                keep = row < tc - s
                a_s = jnp.where(keep, pltpu.roll(a, tc - s, 0), 1.0)
                u_s = jnp.where(keep, pltpu.roll(u, tc - s, 0), 0.0)
            else:
                keep = row >= s
                a_s = jnp.where(keep, pltpu.roll(a, s, 0), 1.0)
                u_s = jnp.where(keep, pltpu.roll(u, s, 0), 0.0)
            u = a * u_s + u
            a = a * a_s
            s *= 2
        return a * h_in + u

    def chunk_start(i):
        return pl.multiple_of(jnp.where(i < ncx, n_latent + i * tc, (i - ncx) * tc), tc)

    def seg_bounds(i):
        lo = jnp.where(i < ncx, n_latent, 0)
        hi = jnp.where(i < ncx, n_latent + n_ctx, n_latent)
        return lo, hi

    hf_ref[...] = jnp.zeros_like(hf_ref)
    hb_ref[...] = jnp.zeros_like(hb_ref)

    def fwd(i, _):
        c0 = chunk_start(i)
        lo, hi = seg_bounds(i)
        a, uu = coeffs(conv_chunk(c0, lo, hi), 0)
        h = scan_chunk(a, uu, hf_ref[0:1], False)
        hf_ref[...] = jnp.broadcast_to(h[tc - 1:tc], hf_ref.shape)
        r_ref[0, pl.ds(c0, tc), :] = h
        return 0

    lax.fori_loop(0, ncx + nl, fwd, 0)

    def bwd(i, _):
        ii = jnp.where(i < ncx, ncx - 1 - i, ncx + (nl - 1 - (i - ncx)))
        c0 = chunk_start(ii)
        lo, hi = seg_bounds(ii)
        a, uu = coeffs(conv_chunk(c0, lo, hi), 1)
        h = scan_chunk(a, uu, hb_ref[0:1], True)
        hb_ref[...] = jnp.broadcast_to(h[0:1], hb_ref.shape)
        r_ref[0, pl.ds(c0, tc), :] = r_ref[0, pl.ds(c0, tc), :] + h
        return 0

    lax.fori_loop(0, ncx + nl, bwd, 0)


def _lru(x_lru, wts, n_latent):
    b_, t_, w = x_lru.shape
    ng = w // LANES
    kern = functools.partial(_lru_kernel, tc=LRU_CHUNK, n_latent=n_latent, n_ctx=t_ - n_latent)
    return pl.pallas_call(
        kern,
        grid=(b_, ng),
        in_specs=[
            pl.BlockSpec((1, t_, LANES), lambda b, g: (b, 0, g)),
            pl.BlockSpec((LRU_CONV, LANES), lambda b, g: (0, g)),
            pl.BlockSpec((1, LANES), lambda b, g: (0, g)),
            pl.BlockSpec((2, None, LANES, 2 * LANES), lambda b, g: (0, g, 0, 0)),
            pl.BlockSpec((2, None, 1, 2 * LANES), lambda b, g: (0, g, 0, 0)),
            pl.BlockSpec((2, None, 1, LANES), lambda b, g: (0, g, 0, 0)),
        ],
        out_specs=pl.BlockSpec((1, t_, LANES), lambda b, g: (b, 0, g)),
        out_shape=jax.ShapeDtypeStruct((b_, t_, w), F32),
        scratch_shapes=[pltpu.VMEM((8, LANES), F32), pltpu.VMEM((8, LANES), F32)],
        compiler_params=_cparams(("parallel", "parallel")),
        name="rglru",
    )(x_lru, wts["conv_w"], wts["conv_b"], wts["w_gate"], wts["b_gate"], wts["softplus"])


def _residual(x_ref, y, modb_ref, modc_ref, is_ctx, d):
    gate = jnp.where(is_ctx, modc_ref[0, :, 2 * d:3 * d], modb_ref[0, :, 2 * d:3 * d])
    return x_ref[0] + gate * y


def _even_post_kernel(x_ref, modb_ref, modc_ref, o_ref, gm_ref, r_ref, gl_ref, wo_ref, out_ref,
                      *, tm, n_latent, d):
    is_ctx = _row_is_ctx(tm, n_latent)
    a = (o_ref[0] * gm_ref[0]).astype(BF16)
    bb = (r_ref[0] * gl_ref[0]).astype(BF16)
    y = (jnp.dot(a, wo_ref[0:MLA_WIDTH, :], preferred_element_type=F32)
         + jnp.dot(bb, wo_ref[MLA_WIDTH:, :], preferred_element_type=F32))
    out_ref[0] = _residual(x_ref, y, modb_ref, modc_ref, is_ctx, d)


def _even_post(xc, modb, modc, o, gm, r, gl, w_out, n_latent):
    b_, t_, d = xc.shape
    tm = ROW_TILE
    tok = lambda w: pl.BlockSpec((1, tm, w), lambda b, t: (b, t, 0))
    return pl.pallas_call(
        functools.partial(_even_post_kernel, tm=tm, n_latent=n_latent, d=d),
        grid=(b_, t_ // tm),
        in_specs=[
            tok(d),
            pl.BlockSpec((1, 1, 3 * d), lambda b, t: (b, 0, 0)),
            _const_spec((1, 1, 3 * d)),
            tok(MLA_WIDTH), tok(MLA_WIDTH), tok(LRU_WIDTH), tok(LRU_WIDTH),
            _const_spec(w_out.shape),
        ],
        out_specs=tok(d),
        out_shape=jax.ShapeDtypeStruct((b_, t_, d), F32),
        compiler_params=_cparams(("parallel", "parallel")),
        name="even_post",
    )(xc, modb, modc, o, gm, r, gl, w_out)


def _odd_post_kernel(x_ref, modb_ref, modc_ref, o_ref, g_ref, wo_ref, fg_ref, out_ref,
                     *, tm, n_latent, d, final):
    is_ctx = _row_is_ctx(tm, n_latent)
    a = (o_ref[0] * g_ref[0]).astype(BF16)
    y = jnp.dot(a, wo_ref[...], preferred_element_type=F32)
    xn = _residual(x_ref, y, modb_ref, modc_ref, is_ctx, d)
    if final:
        xn = _rms(xn, fg_ref[...])
    out_ref[0] = xn


def _odd_post(xc, modb, modc, o, g, w_out, final_g, n_latent, final):
    b_, t_, d = xc.shape
    tm = FINAL_ROW_TILE if final else ROW_TILE
    n_rows = n_latent if final else t_
    tok = lambda w: pl.BlockSpec((1, tm, w), lambda b, t: (b, t, 0))
    return pl.pallas_call(
        functools.partial(_odd_post_kernel, tm=tm, n_latent=n_latent, d=d, final=final),
        grid=(b_, n_rows // tm),
        in_specs=[
            tok(d),
            pl.BlockSpec((1, 1, 3 * d), lambda b, t: (b, 0, 0)),
            _const_spec((1, 1, 3 * d)),
            tok(NA_WIDTH), tok(NA_WIDTH),
            _const_spec(w_out.shape),
            _const_spec((1, d)),
        ],
        out_specs=tok(d),
        out_shape=jax.ShapeDtypeStruct((b_, n_rows, d), F32),
        compiler_params=_cparams(("parallel", "parallel")),
        name="odd_post_final" if final else "odd_post",
    )(xc, modb, modc, o, g, w_out, final_g)


def _odd_pre_kernel(x_ref, modb_ref, modc_ref, g_ref, win_ref, q_out, k_out, v_out, g_out,
                    *, tm, n_latent, d):
    is_ctx = _row_is_ctx(tm, n_latent)
    h = _modulated_norm(x_ref[0], g_ref[...], modb_ref, modc_ref, is_ctx, d).astype(BF16)
    w = NA_WIDTH
    hd = NA_HEAD_DIM
    q_t = (jnp.dot(h, win_ref[:, 0:w], preferred_element_type=F32) * (hd ** -0.5 * LOG2E)).T
    zeros = jnp.zeros((hd, tm), BF16)
    for hh in range(NA_HEADS):
        qh = q_t[hh * hd:(hh + 1) * hd].astype(BF16)
        q_out[0, hh * LANES:(hh + 1) * LANES, :] = jnp.concatenate(
            [qh, zeros] if hh % 2 == 0 else [zeros, qh], axis=0)
    k_out[0] = jnp.dot(h, win_ref[:, w:2 * w], preferred_element_type=F32).astype(BF16)
    v_t = jnp.dot(h, win_ref[:, 2 * w:3 * w], preferred_element_type=F32).T
    ones_rows = jnp.ones((V_EXT_ROWS - LANES, NA_VCHUNK), BF16)
    for p in range(NA_HEADS // 2):
        for cc in range(tm // NA_VCHUNK):
            v_out[0, p, cc, 0:LANES, :] = v_t[p * LANES:(p + 1) * LANES,
                                              cc * NA_VCHUNK:(cc + 1) * NA_VCHUNK].astype(BF16)
            v_out[0, p, cc, LANES:V_EXT_ROWS, :] = ones_rows
    g_out[0] = _silu(jnp.dot(h, win_ref[:, 3 * w:4 * w], preferred_element_type=F32))


def _odd_pre_call(xc, modb, modc, g, w_in, n_latent):
    b_, t_, d = xc.shape
    tm = ROW_TILE
    tok = lambda w: pl.BlockSpec((1, tm, w), lambda b, t: (b, t, 0))
    return pl.pallas_call(
        functools.partial(_odd_pre_kernel, tm=tm, n_latent=n_latent, d=d),
        grid=(b_, t_ // tm),
        in_specs=[
            tok(d),
            pl.BlockSpec((1, 1, 3 * d), lambda b, t: (b, 0, 0)),
            _const_spec((1, 1, 3 * d)),
            _const_spec((1, d)),
            _const_spec(w_in.shape),
        ],
        out_specs=[
            pl.BlockSpec((1, NA_HEADS * LANES, tm), lambda b, t: (b, 0, t)),
            tok(NA_WIDTH),
            pl.BlockSpec((1, NA_HEADS // 2, tm // NA_VCHUNK, V_EXT_ROWS, NA_VCHUNK),
                         lambda b, t: (b, 0, t, 0, 0)),
            tok(NA_WIDTH),
        ],
        out_shape=[
            jax.ShapeDtypeStruct((b_, NA_HEADS * LANES, t_), BF16),
            jax.ShapeDtypeStruct((b_, t_, NA_WIDTH), BF16),
            jax.ShapeDtypeStruct((b_, NA_HEADS // 2, t_ // NA_VCHUNK, V_EXT_ROWS, NA_VCHUNK), BF16),
            jax.ShapeDtypeStruct((b_, t_, NA_WIDTH), F32),
        ],
        compiler_params=_cparams(("parallel", "parallel")),
        name="odd_pre",
    )(xc, modb, modc, g, w_in)


def _na_kernel(q_ref, k_ref, v_ref, *rest, nq, nkl, n_ctx, rows, k_ctx0, v_ctx0):
    local = nkl > 0
    if local:
        slab_ref, idx_ref, o_ref = rest[0:3]
        scratch = rest[3:]
        blk = pl.program_id(2)
        k_row0 = jnp.clip(blk * NA_QROWS - NA_WIN_R // 2, 0, rows - NA_KROWS)
        kstart = pl.multiple_of(k_row0 * GRID_W, NA_VCHUNK)
        v_loc0 = k_row0 * GRID_W // NA_VCHUNK
        typ = jnp.where(blk == 0, 0, jnp.where(blk == rows // NA_QROWS - 1, 2, 1))
    else:
        o_ref = rest[0]
        scratch = rest[1:]
    s_buf, p_buf, m_ref = scratch[0:2], scratch[2:4], scratch[4:6]

    def score(j):
        q_t = q_ref[0, j * LANES:(j + 1) * LANES, :]
        s_c = jnp.dot(k_ref[0, k_ctx0:k_ctx0 + n_ctx, :], q_t, preferred_element_type=F32)
        s_buf[j][nkl:nkl + n_ctx, :] = s_c
        m = jnp.max(s_c, axis=0, keepdims=True)
        if local:
            s_buf[j][0:nkl, :] = jnp.dot(k_ref[0, pl.ds(kstart, nkl), :], q_t, preferred_element_type=F32)
            n_pair = nq // LANES
            parts = []
            for a in range(n_pair):
                part = jnp.full((GRID_W, LANES), -jnp.inf, F32)
                for kr in range(NA_KROWS):
                    rows_kr = slice(kr * GRID_W, (kr + 1) * GRID_W)
                    cols_a = slice(a * LANES, (a + 1) * LANES)
                    t = s_buf[j][rows_kr, cols_a] + slab_ref[0, j, idx_ref[typ, kr * n_pair + a]]
                    s_buf[j][rows_kr, cols_a] = t
                    part = jnp.maximum(part, t)
                parts.append(jnp.max(part, axis=0, keepdims=True))
            m = jnp.maximum(m, jnp.concatenate(parts, axis=1))
        m_ref[j][...] = m

    def prob(j):
        m = jnp.broadcast_to(m_ref[j][...], (PROB_ROWS, nq))
        for r in range(0, nkl + n_ctx, PROB_ROWS):
            p_buf[j][r:r + PROB_ROWS, :] = jnp.exp2(s_buf[j][r:r + PROB_ROWS, :] - m).astype(BF16)

    def value(j):
        acc = None
        for cc in range((nkl + n_ctx) // NA_VCHUNK):
            in_window = cc < nkl // NA_VCHUNK
            v_t = v_ref[0, 0, v_loc0 + cc] if in_window else v_ref[0, 0, v_ctx0 + cc - nkl // NA_VCHUNK]
            pv = jnp.dot(v_t, p_buf[j][cc * NA_VCHUNK:(cc + 1) * NA_VCHUNK, :],
                         preferred_element_type=F32)
            acc = pv if acc is None else acc + pv
        return acc

    score(0)
    score(1)
    prob(0)
    a0 = value(0)
    prob(1)
    a1 = value(1)
    o0 = a0[0:NA_HEAD_DIM] / a0[LANES:LANES + 1]
    o1 = a1[NA_HEAD_DIM:LANES] / a1[LANES:LANES + 1]
    o_ref[0] = jnp.concatenate([o0, o1], axis=0).T


def _na_bias_tables(rpb, rows):
    n_heads = rpb.shape[0]
    pad = GRID_W - NA_WIN_C
    rp = jnp.pad(rpb * LOG2E, ((0, 0), (0, 0), (pad, pad)))
    toep = jnp.stack([rp[:, :, GRID_W - 1 - qc:2 * GRID_W - 1 - qc] for qc in range(GRID_W)], axis=3)
    col = np.arange(GRID_W)
    cs = np.clip(col - NA_WIN_C // 2, 0, GRID_W - NA_WIN_C)
    valid_c = (col[:, None] >= cs[None, :]) & (col[:, None] < cs[None, :] + NA_WIN_C)
    toep = jnp.where(valid_c, toep, NEG_BIG)
    n_dr = 2 * NA_WIN_R - 1
    slabs = jnp.concatenate([toep, jnp.full((n_heads, 1, GRID_W, GRID_W), NEG_BIG, F32)], axis=1)
    idx = np.full((3, NA_KROWS, NA_QROWS), n_dr, np.int32)
    for typ, r0 in enumerate((0, 2 * NA_QROWS, rows - NA_QROWS)):
        ks = int(np.clip(r0 - NA_WIN_R // 2, 0, rows - NA_KROWS))
        for kr in range(NA_KROWS):
            for qr in range(NA_QROWS):
                r = r0 + qr
                rs = int(np.clip(r - NA_WIN_R // 2, 0, rows - NA_WIN_R))
                if rs <= ks + kr < rs + NA_WIN_R:
                    idx[typ, kr, qr] = ks + kr - r + NA_WIN_R - 1
    pairs = idx.reshape(3, NA_KROWS, NA_QROWS // 2, 2)
    uniq, inverse = np.unique(pairs.reshape(-1, 2), axis=0, return_inverse=True)
    table = jnp.concatenate([jnp.take(slabs, jnp.asarray(uniq[:, 0]), axis=1),
                             jnp.take(slabs, jnp.asarray(uniq[:, 1]), axis=1)], axis=-1)
    table = table.reshape(n_heads // 2, 2, uniq.shape[0], GRID_W, 2 * GRID_W)
    return table, jnp.asarray(inverse.reshape(3, NA_KROWS * (NA_QROWS // 2)).astype(np.int32))


def _na_scratch(n_keys, nq):
    return ([pltpu.VMEM((n_keys, nq), F32)] * 2 + [pltpu.VMEM((n_keys, nq), BF16)] * 2
            + [pltpu.VMEM((1, nq), F32)] * 2)


def _na_attention(q_t, k, v_t, bias, n_latent, need_ctx):
    b_, t_, _ = k.shape
    n_ctx = t_ - n_latent
    rows = n_latent // GRID_W
    nq = NA_QROWS * GRID_W
    nkl = NA_KROWS * GRID_W
    n_blocks = rows // NA_QROWS
    n_pairs = NA_HEADS // 2
    n_vc = t_ // NA_VCHUNK
    slab_table, slab_idx = bias

    o_lat = pl.pallas_call(
        functools.partial(_na_kernel, nq=nq, nkl=nkl, n_ctx=n_ctx, rows=rows, k_ctx0=n_latent,
                          v_ctx0=n_latent // NA_VCHUNK),
        grid=(n_pairs, b_, n_blocks),
        in_specs=[
            pl.BlockSpec((1, 2 * LANES, nq), lambda p, b, i: (b, p, i)),
            pl.BlockSpec((1, t_, LANES), lambda p, b, i: (b, 0, p)),
            pl.BlockSpec((1, 1, n_vc, V_EXT_ROWS, NA_VCHUNK), lambda p, b, i: (b, p, 0, 0, 0)),
            pl.BlockSpec((1,) + slab_table.shape[1:], lambda p, b, i: (p, 0, 0, 0, 0)),
            pl.BlockSpec(memory_space=pltpu.SMEM),
        ],
        out_specs=pl.BlockSpec((1, nq, LANES), lambda p, b, i: (b, i, p)),
        out_shape=jax.ShapeDtypeStruct((b_, n_latent, NA_WIDTH), F32),
        scratch_shapes=_na_scratch(nkl + n_ctx, nq),
        compiler_params=_cparams(("parallel", "parallel", "arbitrary")),
        name="na_attn",
    )(q_t, k, v_t, slab_table, slab_idx)
    if not need_ctx:
        return o_lat
    cb = n_latent // n_ctx
    o_ctx = pl.pallas_call(
        functools.partial(_na_kernel, nq=n_ctx, nkl=0, n_ctx=n_ctx, rows=rows, k_ctx0=0, v_ctx0=0),
        grid=(n_pairs, b_),
        in_specs=[
            pl.BlockSpec((1, 2 * LANES, n_ctx), lambda p, b: (b, p, cb)),
            pl.BlockSpec((1, n_ctx, LANES), lambda p, b: (b, cb, p)),
            pl.BlockSpec((1, 1, n_ctx // NA_VCHUNK, V_EXT_ROWS, NA_VCHUNK), lambda p, b: (b, p, cb, 0, 0)),
        ],
        out_specs=pl.BlockSpec((1, n_ctx, LANES), lambda p, b: (b, 0, p)),
        out_shape=jax.ShapeDtypeStruct((b_, n_ctx, NA_WIDTH), F32),
        scratch_shapes=_na_scratch(n_ctx, n_ctx),
        compiler_params=_cparams(("parallel", "parallel")),
        name="na_attn_ctx",
    )(q_t, k, v_t)
    return jnp.concatenate([o_lat, o_ctx], axis=1)


def _rope_swap_index():
    half = MLA_ROPE // 4
    idx = np.arange(MLA_ROPE)
    return np.where((idx // half) % 2 == 0, idx + half, idx - half)


def _rope_tables(n_latent, n_ctx):
    n_freq = MLA_ROPE // 4
    inv = ROPE_THETA ** (-jnp.arange(n_freq, dtype=F32) / n_freq)
    t = jnp.arange(n_latent, dtype=jnp.int32)
    ang_r = (t // GRID_W).astype(F32)[:, None] * inv
    ang_c = (t % GRID_W).astype(F32)[:, None] * inv
    cr, sr, cc, sc = jnp.cos(ang_r), jnp.sin(ang_r), jnp.cos(ang_c), jnp.sin(ang_c)
    cos32 = jnp.concatenate([cr, cr, cc, cc], axis=-1)
    sin32 = jnp.concatenate([-sr, sr, -sc, sc], axis=-1)
    ones = jnp.ones((n_latent, MLA_NOPE), F32)
    zeros = jnp.zeros((n_latent, LANES - MLA_NOPE - MLA_ROPE), F32)
    cos_l = jnp.concatenate([ones, cos32, zeros], axis=-1)
    sin_l = jnp.concatenate([0 * ones, sin32, zeros], axis=-1)
    cos_c = jnp.concatenate([jnp.ones((n_ctx, MLA_NOPE + MLA_ROPE), F32),
                             jnp.zeros((n_ctx, LANES - MLA_NOPE - MLA_ROPE), F32)], axis=-1)
    sin_c = jnp.zeros((n_ctx, LANES), F32)
    return jnp.concatenate([cos_l, cos_c], axis=0), jnp.concatenate([sin_l, sin_c], axis=0)


def _even_weights(i, ev_w_in, mla_q_norm, mla_w_uq, mla_kv_norm, mla_w_ukv, lru_conv_w, lru_conv_b,
                  lru_wa, lru_ba, lru_wx, lru_bx, lru_lambda, ev_w_out):
    sw = _rope_swap_index()
    d = ev_w_in.shape[1]
    w_in = ev_w_in[i]
    o_cq, o_ckv, o_kr = MLA_Q_RANK, MLA_Q_RANK + MLA_KV_RANK, MLA_Q_RANK + MLA_KV_RANK + MLA_ROPE
    kr = w_in[:, o_ckv:o_kr]
    pad_lo = jnp.zeros((d, MLA_NOPE), F32)
    pad_hi = jnp.zeros((d, LANES - MLA_NOPE - MLA_ROPE), F32)
    w_in_ext = jnp.concatenate(
        [w_in[:, :o_ckv], pad_lo, kr, pad_hi, pad_lo, kr[:, sw], pad_hi, w_in[:, o_kr:]], axis=1)
    wq = mla_w_uq[i].reshape(MLA_Q_RANK, MLA_HEADS, MLA_NOPE + MLA_ROPE)
    zq = jnp.zeros((MLA_Q_RANK, MLA_HEADS, LANES - MLA_NOPE - MLA_ROPE), F32)
    wqa = jnp.concatenate([wq, zq], axis=-1).reshape(MLA_Q_RANK, MLA_HEADS * LANES)
    wqb = jnp.concatenate([jnp.zeros((MLA_Q_RANK, MLA_HEADS, MLA_NOPE), F32),
                           wq[:, :, MLA_NOPE:][:, :, sw], zq], axis=-1).reshape(MLA_Q_RANK, MLA_HEADS * LANES)
    wkv = mla_w_ukv[i].reshape(MLA_KV_RANK, MLA_HEADS, MLA_NOPE + MLA_V)
    wk = jnp.concatenate([wkv[:, :, :MLA_NOPE], jnp.zeros((MLA_KV_RANK, MLA_HEADS, LANES - MLA_NOPE), F32)],
                         axis=-1).reshape(MLA_KV_RANK, MLA_HEADS * LANES)
    wv = wkv[:, :, MLA_NOPE:].reshape(MLA_KV_RANK, MLA_WIDTH)
    ng = LRU_WIDTH // LANES
    per = LANES // LRU_BLOCK

    def blockdiag(w):
        w = w.reshape(2, ng, per, LRU_BLOCK, LRU_BLOCK)
        eye = jnp.eye(per, dtype=F32)
        return jnp.einsum('dgpkj,pq->dgpkqj', w, eye).reshape(2, ng, LANES, LANES)

    w_gate = jnp.concatenate([blockdiag(lru_wa[i]), blockdiag(lru_wx[i])], axis=-1).astype(BF16)
    b_gate = jnp.concatenate([lru_ba[i].reshape(2, ng, 1, LANES), lru_bx[i].reshape(2, ng, 1, LANES)], axis=-1)
    return dict(
        w_in=w_in_ext.astype(BF16), q_norm=mla_q_norm[i][None], wqa=wqa.astype(BF16), wqb=wqb.astype(BF16),
        kv_norm=mla_kv_norm[i][None], wk=wk.astype(BF16), wv=wv.astype(BF16),
        conv_w=lru_conv_w[i], conv_b=lru_conv_b[i][None], w_gate=w_gate, b_gate=b_gate,
        softplus=jax.nn.softplus(-lru_lambda[i]).reshape(2, ng, 1, LANES),
        w_out=ev_w_out[i].astype(BF16))


def kernel(x, c, ctx, c_ctx, ada_w, ada_b, norm_g, ev_w_in, mla_q_norm, mla_w_uq, mla_kv_norm, mla_w_ukv,
           lru_conv_w, lru_conv_b, lru_wa, lru_ba, lru_wx, lru_bx, lru_lambda, ev_w_out, od_w_in, na_rpb,
           od_w_out, final_norm_g):
    b_, s_, d = x.shape
    n_ctx = ctx.shape[1]
    depth = ada_w.shape[0]
    rows = s_ // GRID_W
    assert b_ <= 7 and s_ % GRID_W == 0 and rows % NA_QROWS == 0 and rows >= NA_KROWS
    assert (s_ + n_ctx) % ROW_TILE == 0 and s_ % FINAL_ROW_TILE == 0 and s_ % MLA_TQ == 0
    assert ROW_TILE == MLA_TK and n_ctx <= MLA_TK and s_ % n_ctx == 0 and n_ctx % LRU_CHUNK == 0 and s_ % LRU_CHUNK == 0
    assert n_ctx % NA_VCHUNK == 0 and ROW_TILE % NA_VCHUNK == 0 and rows >= 3 * NA_QROWS
    assert (NA_QROWS // 2 * GRID_W) % NA_VCHUNK == 0 and depth % 2 == 0

    cond = jnp.zeros((8, d), F32).at[:b_].set(c).at[b_].set(c_ctx)
    mod = _adaln(cond, ada_w, ada_b)
    xc = jnp.concatenate([x, ctx], axis=1)
    cos_t, sin_t = _rope_tables(s_, n_ctx)

    for layer in range(depth):
        need_ctx = layer < depth - 1
        i = layer // 2
        modb = mod[layer, :b_][:, None, :]
        modc = mod[layer, b_][None, None, :]
        g = norm_g[layer][None]
        if layer % 2 == 0:
            wts = _even_weights(i, ev_w_in, mla_q_norm, mla_w_uq, mla_kv_norm, mla_w_ukv, lru_conv_w,
                                lru_conv_b, lru_wa, lru_ba, lru_wx, lru_bx, lru_lambda, ev_w_out)
            q_t, kc, v_t, gm, xl, gl = _even_pre(xc, modb, modc, g, wts, cos_t, sin_t, s_)
            o = _mla_attention(q_t, kc, v_t, s_)
            r = _lru(xl, wts, s_)
            xc = _even_post(xc, modb, modc, o, gm, r, gl, wts["w_out"], s_)
        else:
            q, k, v, gg = _odd_pre_call(xc, modb, modc, g, od_w_in[i].astype(BF16), s_)
            bias = _na_bias_tables(na_rpb[i], rows)
            o = _na_attention(q, k, v, bias, s_, need_ctx)
            xc = _odd_post(xc, modb, modc, o, gg, od_w_out[i].astype(BF16), final_norm_g[None], s_,
                           final=not need_ctx)
    return xc
```

```python
import functools
import math

import numpy as np
import jax
import jax.numpy as jnp
from jax import lax
from jax.experimental import pallas as pl
from jax.experimental.pallas import tpu as pltpu

F32 = jnp.float32
BF16 = jnp.bfloat16

GRID_W = 64
RMS_EPS = 1e-6
ROPE_THETA = 10000.0
MLA_HEADS = 8
MLA_NOPE = 64
MLA_ROPE = 32
MLA_V = 64
MLA_Q_RANK = 256
MLA_KV_RANK = 128
MLA_WIDTH = MLA_HEADS * MLA_V
MLA_SCALE = (MLA_NOPE + MLA_ROPE) ** -0.5
LRU_WIDTH = 512
LRU_BLOCKS = 8
LRU_BLOCK = LRU_WIDTH // LRU_BLOCKS
LRU_CONV = 4
LRU_C = 8.0
NA_HEADS = 16
NA_HEAD_DIM = 64
NA_WIDTH = NA_HEADS * NA_HEAD_DIM
NA_WIN_R = 8
NA_WIN_C = 16

LANES = 128
LOG2E = 1.4426950408889634
NEG_BIG = -1e30
VMEM_LIMIT = 56 * 1024 * 1024

ROW_TILE = 768
FINAL_ROW_TILE = 512
MLA_TQ = 1024
MLA_S_SLOTS = 4
MLA_WQ = 512
MLA_TK = 768
MLA_UNROLL = 2
PROB_ROWS = 16
V_EXT_ROWS = LANES + 16
NA_QROWS = 8
NA_KROWS = 16
NA_VCHUNK = 256
LRU_CHUNK = 256


def _cparams(sem, flags=None):
    return pltpu.CompilerParams(dimension_semantics=sem, vmem_limit_bytes=VMEM_LIMIT, flags=flags)


def _silu(v):
    return v * jax.nn.sigmoid(v)


def _adaln_kernel(cond_ref, w_ref, b_ref, o_ref):
    a = _silu(cond_ref[...])
    o_ref[0] = jnp.dot(a, w_ref[0], preferred_element_type=F32) + b_ref[0]


def _adaln(cond, ada_w, ada_b):
    depth, d, d3 = ada_w.shape
    tn = 1024
    return pl.pallas_call(
        _adaln_kernel,
        grid=(depth, d3 // tn),
        in_specs=[
            pl.BlockSpec((8, d), lambda l, n: (0, 0)),
            pl.BlockSpec((1, d, tn), lambda l, n: (l, 0, n)),
            pl.BlockSpec((1, 1, tn), lambda l, n: (l, 0, n)),
        ],
        out_specs=pl.BlockSpec((1, 8, tn), lambda l, n: (l, 0, n)),
        out_shape=jax.ShapeDtypeStruct((depth, 8, d3), F32),
        compiler_params=_cparams(("arbitrary", "arbitrary")),
        name="adaln",
    )(cond, ada_w, ada_b.reshape(depth, 1, d3))


def _modulated_norm(x, g, modb_ref, modc_ref, is_ctx, d):
    ms = jnp.mean(x * x, axis=-1, keepdims=True)
    y = x * lax.rsqrt(ms + RMS_EPS) * g
    shift = jnp.where(is_ctx, modc_ref[0, :, 0:d], modb_ref[0, :, 0:d])
    scale = jnp.where(is_ctx, modc_ref[0, :, d:2 * d], modb_ref[0, :, d:2 * d])
    return y * (1.0 + scale) + shift


def _row_is_ctx(tm, n_latent):
    row = pl.program_id(1) * tm + lax.broadcasted_iota(jnp.int32, (tm, 1), 0)
    return row >= n_latent


def _rms(v, g):
    ms = jnp.mean(v * v, axis=-1, keepdims=True)
    return v * lax.rsqrt(ms + RMS_EPS) * g


def _even_pre_kernel(x_ref, modb_ref, modc_ref, g_ref, win_ref, qn_ref, wqa_ref, wqb_ref,
                     kvn_ref, wk_ref, wv_ref, cos_ref, sin_ref,
                     q_out, k_out, v_out, gm_out, xl_out, gl_out, *, tm, n_latent, d):
    is_ctx = _row_is_ctx(tm, n_latent)
    h = _modulated_norm(x_ref[0], g_ref[...], modb_ref, modc_ref, is_ctx, d).astype(BF16)

    def proj(lo, hi):
        return jnp.dot(h, win_ref[:, lo:hi], preferred_element_type=F32)

    cos = cos_ref[...]
    sin = sin_ref[...]
    cq = _rms(proj(0, 256), qn_ref[...]).astype(BF16)
    qa = jnp.dot(cq, wqa_ref[...], preferred_element_type=F32)
    qb = jnp.dot(cq, wqb_ref[...], preferred_element_type=F32)
    ckv = _rms(proj(256, 384), kvn_ref[...]).astype(BF16)
    kk = jnp.dot(ckv, wk_ref[...], preferred_element_type=F32)
    k_rope = proj(384, 512) * cos + proj(512, 640) * sin
    v_t = jnp.dot(ckv, wv_ref[...], preferred_element_type=F32).T
    ones_rows = jnp.ones((V_EXT_ROWS - LANES, tm), BF16)
    for p in range(MLA_HEADS // 2):
        v_out[0, p, 0, 0:LANES, :] = v_t[p * LANES:(p + 1) * LANES].astype(BF16)
        v_out[0, p, 0, LANES:V_EXT_ROWS, :] = ones_rows
    for hh in range(MLA_HEADS):
        sl = slice(hh * LANES, (hh + 1) * LANES)
        q = (qa[:, sl] * cos + qb[:, sl] * sin) * (MLA_SCALE * LOG2E)
        q_out[0, sl, :] = q.T.astype(BF16)
        k_out[0, :, sl] = (kk[:, sl] + k_rope).astype(BF16)
    gm_out[0] = _silu(proj(640, 1152)).astype(BF16)
    xl_out[0] = proj(1152, 1664)
    gl_out[0] = _silu(proj(1664, 2176)).astype(BF16)


def _const_spec(shape):
    nd = len(shape)
    return pl.BlockSpec(shape, lambda b, t: (0,) * nd)


def _even_pre(xc, modb, modc, g, wts, cos_t, sin_t, n_latent):
    b_, t_, d = xc.shape
    tm = ROW_TILE
    kern = functools.partial(_even_pre_kernel, tm=tm, n_latent=n_latent, d=d)
    tok = lambda w: pl.BlockSpec((1, tm, w), lambda b, t: (b, t, 0))
    hw = MLA_HEADS * LANES
    return pl.pallas_call(
        kern,
        grid=(b_, t_ // tm),
        in_specs=[
            tok(d),
            pl.BlockSpec((1, 1, 3 * d), lambda b, t: (b, 0, 0)),
            _const_spec((1, 1, 3 * d)),
            _const_spec((1, d)),
            _const_spec(wts["w_in"].shape),
            _const_spec((1, MLA_Q_RANK)),
            _const_spec(wts["wqa"].shape),
            _const_spec(wts["wqb"].shape),
            _const_spec((1, MLA_KV_RANK)),
            _const_spec(wts["wk"].shape),
            _const_spec(wts["wv"].shape),
            pl.BlockSpec((tm, LANES), lambda b, t: (t, 0)),
            pl.BlockSpec((tm, LANES), lambda b, t: (t, 0)),
        ],
        out_specs=[pl.BlockSpec((1, hw, tm), lambda b, t: (b, 0, t)), tok(hw),
                   pl.BlockSpec((1, MLA_HEADS // 2, 1, V_EXT_ROWS, tm), lambda b, t: (b, 0, t, 0, 0)),
                   tok(MLA_WIDTH), tok(LRU_WIDTH), tok(LRU_WIDTH)],
        out_shape=[
            jax.ShapeDtypeStruct((b_, hw, t_), BF16),
            jax.ShapeDtypeStruct((b_, t_, hw), BF16),
            jax.ShapeDtypeStruct((b_, MLA_HEADS // 2, t_ // tm, V_EXT_ROWS, tm), BF16),
            jax.ShapeDtypeStruct((b_, t_, MLA_WIDTH), BF16),
            jax.ShapeDtypeStruct((b_, t_, LRU_WIDTH), F32),
            jax.ShapeDtypeStruct((b_, t_, LRU_WIDTH), BF16),
        ],
        compiler_params=_cparams(("parallel", "parallel")),
        name="even_pre",
    )(xc, modb, modc, g, wts["w_in"], wts["q_norm"], wts["wqa"], wts["wqb"], wts["kv_norm"],
      wts["wk"], wts["wv"], cos_t, sin_t)


def _mla_attn_t_kernel(q_ref, k_ref, v_ref, o_ref, *scratch, tq, wq, tk, nk, v_lane0):
    nh = tq // wq
    n_sub = 2 * nh
    n_total = n_sub * nk
    ns = min(n_sub, MLA_S_SLOTS)
    s_buf, p_buf = scratch[0:ns], scratch[ns:ns + 2]
    acc_ref, m_ref, al_ref = (scratch[ns + 2 + r * n_sub:ns + 2 + (r + 1) * n_sub] for r in range(3))

    def score(c, i):
        j, h = divmod(i, nh)
        start = c * tk if isinstance(c, int) else pl.multiple_of(c * tk, LANES)
        k = k_ref[0, pl.ds(start, tk), j * LANES:(j + 1) * LANES]
        q_t = q_ref[0, j * LANES:(j + 1) * LANES, h * wq:(h + 1) * wq]
        s = jnp.dot(k, q_t, preferred_element_type=F32)
        m = m_ref[i][...]
        m_new = jnp.maximum(m, jnp.max(s, axis=0, keepdims=True))
        al_ref[i][...] = jnp.exp2(m - m_new)
        m_ref[i][...] = m_new
        s_buf[i % ns][...] = s

    def prob(c, i):
        m = jnp.broadcast_to(m_ref[i][...], (PROB_ROWS, wq))
        for r in range(0, tk, PROB_ROWS):
            p = jnp.exp2(s_buf[i % ns][r:r + PROB_ROWS, :] - m)
            p_buf[i % 2][r:r + PROB_ROWS, :] = p.astype(BF16)

    def value(c, i):
        v_t = v_ref[0, 0, c, :, v_lane0:v_lane0 + tk]
        pv = jnp.dot(v_t, p_buf[i % 2][...], preferred_element_type=F32)
        acc_ref[i][...] = al_ref[i][...] * acc_ref[i][...] + pv

    for i in range(n_sub):
        m_ref[i][...] = jnp.full(m_ref[i].shape, -jnp.inf, F32)
        acc_ref[i][...] = jnp.zeros(acc_ref[i].shape, F32)

    def group(c, r):
        for stage, lag in ((value, 3), (prob, 2), (score, 0)):
            dc, i = divmod(r - lag, n_sub)
            if isinstance(c, int) and not 0 <= (c + dc) * n_sub + i < n_total:
                continue
            stage(c + dc, i)

    for g in range(3):
        group(0, g)

    def body(it, _):
        for r in range(3, 3 + MLA_UNROLL * n_sub):
            group(it * MLA_UNROLL, r)
        return 0

    trips = (nk - 1) // MLA_UNROLL
    lax.fori_loop(0, trips, body, 0)
    for g in range(n_sub * MLA_UNROLL * trips + 3, n_total + 3):
        group(0, g)

    for h in range(nh):
        a0 = acc_ref[h][...]
        a1 = acc_ref[nh + h][...]
        o0 = a0[0:MLA_V] / a0[LANES:LANES + 1]
        o1 = a1[MLA_V:LANES] / a1[LANES:LANES + 1]
        o_ref[0, h * wq:(h + 1) * wq, :] = jnp.concatenate([o0, o1], axis=0).T.astype(BF16)


def _mla_scratch(tq, wq, tk):
    n_sub = 2 * (tq // wq)
    return ([pltpu.VMEM((tk, wq), F32)] * min(n_sub, MLA_S_SLOTS) + [pltpu.VMEM((tk, wq), BF16)] * 2
            + [pltpu.VMEM((V_EXT_ROWS, wq), F32)] * n_sub + [pltpu.VMEM((1, wq), F32)] * (2 * n_sub))


def _mla_attention(q_t, kc, v_t, n_latent):
    b_, t_, _ = kc.shape
    n_pairs = MLA_HEADS // 2
    tq, wq, tk = MLA_TQ, MLA_WQ, MLA_TK
    n_ctx = t_ - n_latent
    nk = t_ // tk
    o_lat = pl.pallas_call(
        functools.partial(_mla_attn_t_kernel, tq=tq, wq=wq, tk=tk, nk=nk, v_lane0=0),
        grid=(b_, n_pairs, n_latent // tq),
        in_specs=[
            pl.BlockSpec((1, 2 * LANES, tq), lambda b, p, i: (b, p, i)),
            pl.BlockSpec((1, t_, 2 * LANES), lambda b, p, i: (b, 0, p)),
            pl.BlockSpec((1, 1, nk, V_EXT_ROWS, tk), lambda b, p, i: (b, p, 0, 0, 0)),
        ],
        out_specs=pl.BlockSpec((1, tq, LANES), lambda b, p, i: (b, i, p)),
        out_shape=jax.ShapeDtypeStruct((b_, n_latent, MLA_WIDTH), BF16),
        scratch_shapes=_mla_scratch(tq, wq, tk),
        compiler_params=_cparams(("parallel", "parallel", "arbitrary")),
        name="mla_attn_latent",
    )(q_t, kc, v_t)
    cb = n_latent // n_ctx
    o_ctx = pl.pallas_call(
        functools.partial(_mla_attn_t_kernel, tq=n_ctx, wq=n_ctx, tk=n_ctx, nk=1, v_lane0=tk - n_ctx),
        grid=(b_, n_pairs),
        in_specs=[
            pl.BlockSpec((1, 2 * LANES, n_ctx), lambda b, p: (b, p, cb)),
            pl.BlockSpec((1, n_ctx, 2 * LANES), lambda b, p: (b, cb, p)),
            pl.BlockSpec((1, 1, 1, V_EXT_ROWS, tk), lambda b, p: (b, p, nk - 1, 0, 0)),
        ],
        out_specs=pl.BlockSpec((1, n_ctx, LANES), lambda b, p: (b, 0, p)),
        out_shape=jax.ShapeDtypeStruct((b_, n_ctx, MLA_WIDTH), BF16),
        scratch_shapes=_mla_scratch(n_ctx, n_ctx, n_ctx),
        compiler_params=_cparams(("parallel", "parallel")),
        name="mla_attn_ctx",
    )(q_t, kc, v_t)
    return jnp.concatenate([o_lat, o_ctx], axis=1)


def _lru_kernel(x_ref, cw_ref, cb_ref, wg_ref, bg_ref, sp_ref, r_ref, hf_ref, hb_ref,
                *, tc, n_latent, n_ctx):
    nl = n_latent // tc
    ncx = n_ctx // tc
    row = lax.broadcasted_iota(jnp.int32, (tc, 1), 0)

    def conv_chunk(c0, seg_lo, seg_hi):
        xc = x_ref[0, pl.ds(c0, tc), :]
        has_prev = c0 > seg_lo
        has_next = c0 + tc < seg_hi
        p0 = pl.multiple_of(jnp.where(has_prev, c0 - 8, c0), 8)
        n0 = pl.multiple_of(jnp.where(has_next, c0 + tc, c0), 8)
        prev = jnp.where(has_prev, x_ref[0, pl.ds(p0, 8), :], 0.0)
        nxt = jnp.where(has_next, x_ref[0, pl.ds(n0, 8), :], 0.0)
        xm1 = jnp.where(row >= 1, pltpu.roll(xc, 1, 0), prev[7:8])
        xm2 = jnp.where(row >= 2, pltpu.roll(xc, 2, 0), jnp.where(row == 1, prev[7:8], prev[6:7]))
        xp1 = jnp.where(row <= tc - 2, pltpu.roll(xc, tc - 1, 0), nxt[0:1])
        return (cw_ref[0:1] * xm2 + cw_ref[1:2] * xm1 + cw_ref[2:3] * xc + cw_ref[3:4] * xp1
                + cb_ref[...])

    def coeffs(u, d):
        z = jnp.dot(u.astype(BF16), wg_ref[d], preferred_element_type=F32) + bg_ref[d]
        r = jax.nn.sigmoid(z[:, :LANES])
        i = jax.nn.sigmoid(z[:, LANES:])
        log_a = -LRU_C * r * sp_ref[d]
        a = jnp.exp(log_a)
        t = jnp.tanh(log_a)
        uu = jnp.sqrt(-2.0 * t / (1.0 - t)) * (i * u)
        return a, uu

    def scan_chunk(a, u, h_in, reverse):
        s = 1
        while s < tc:
            if s % 8 == 0:
                one, zero = jnp.ones((s, LANES), F32), jnp.zeros((s, LANES), F32)
                if reverse:
                    a_s = jnp.concatenate([a[s:], one], axis=0)
                    u_s = jnp.concatenate([u[s:], zero], axis=0)
                else:
                    a_s = jnp.concatenate([one, a[:tc - s]], axis=0)
                    u_s = jnp.concatenate([zero, u[:tc - s]], axis=0)
            elif reverse:
                keep = row < tc - s
                a_s = jnp.where(keep, pltpu.roll(a, tc - s, 0), 1.0)
                u_s = jnp.where(keep, pltpu.roll(u, tc - s, 0), 0.0)
            else:
                keep = row >= s
                a_s = jnp.where(keep, pltpu.roll(a, s, 0), 1.0)
                u_s = jnp.where(keep, pltpu.roll(u, s, 0), 0.0)
            u = a * u_s + u
            a = a * a_s
            s *= 2
        return a * h_in + u

    def chunk_start(i):
        return pl.multiple_of(jnp.where(i < ncx, n_latent + i * tc, (i - ncx) * tc), tc)

    def seg_bounds(i):
        lo = jnp.where(i < ncx, n_latent, 0)
        hi = jnp.where(i < ncx, n_latent + n_ctx, n_latent)
        return lo, hi

    hf_ref[...] = jnp.zeros_like(hf_ref)
    hb_ref[...] = jnp.zeros_like(hb_ref)
    r_ref[...] = jnp.zeros_like(r_ref)

    def step(i, _):
        c0 = chunk_start(i)
        lo, hi = seg_bounds(i)
        a, uu = coeffs(conv_chunk(c0, lo, hi), 0)
        h = scan_chunk(a, uu, hf_ref[0:1], False)
        hf_ref[...] = jnp.broadcast_to(h[tc - 1:tc], hf_ref.shape)
        r_ref[0, pl.ds(c0, tc), :] = r_ref[0, pl.ds(c0, tc), :] + h
        ii = jnp.where(i < ncx, ncx - 1 - i, ncx + (nl - 1 - (i - ncx)))
        c0 = chunk_start(ii)
        lo, hi = seg_bounds(ii)
        a, uu = coeffs(conv_chunk(c0, lo, hi), 1)
        h = scan_chunk(a, uu, hb_ref[0:1], True)
        hb_ref[...] = jnp.broadcast_to(h[0:1], hb_ref.shape)
        r_ref[0, pl.ds(c0, tc), :] = r_ref[0, pl.ds(c0, tc), :] + h
        return 0

    lax.fori_loop(0, ncx + nl, step, 0)


def _lru(x_lru, wts, n_latent):
    b_, t_, w = x_lru.shape
    ng = w // LANES
    kern = functools.partial(_lru_kernel, tc=LRU_CHUNK, n_latent=n_latent, n_ctx=t_ - n_latent)
    return pl.pallas_call(
        kern,
        grid=(b_, ng),
        in_specs=[
            pl.BlockSpec((1, t_, LANES), lambda b, g: (b, 0, g)),
            pl.BlockSpec((LRU_CONV, LANES), lambda b, g: (0, g)),
            pl.BlockSpec((1, LANES), lambda b, g: (0, g)),
            pl.BlockSpec((2, None, LANES, 2 * LANES), lambda b, g: (0, g, 0, 0)),
            pl.BlockSpec((2, None, 1, 2 * LANES), lambda b, g: (0, g, 0, 0)),
            pl.BlockSpec((2, None, 1, LANES), lambda b, g: (0, g, 0, 0)),
        ],
        out_specs=pl.BlockSpec((1, t_, LANES), lambda b, g: (b, 0, g)),
        out_shape=jax.ShapeDtypeStruct((b_, t_, w), F32),
        scratch_shapes=[pltpu.VMEM((8, LANES), F32), pltpu.VMEM((8, LANES), F32)],
        compiler_params=_cparams(("parallel", "parallel")),
        name="rglru",
    )(x_lru, wts["conv_w"], wts["conv_b"], wts["w_gate"], wts["b_gate"], wts["softplus"])


def _residual(x_ref, y, modb_ref, modc_ref, is_ctx, d):
    gate = jnp.where(is_ctx, modc_ref[0, :, 2 * d:3 * d], modb_ref[0, :, 2 * d:3 * d])
    return x_ref[0] + gate * y


def _even_post_kernel(x_ref, modb_ref, modc_ref, o_ref, gm_ref, r_ref, gl_ref, wo_ref, out_ref,
                      *, tm, n_latent, d):
    is_ctx = _row_is_ctx(tm, n_latent)
    a = o_ref[0] * gm_ref[0]
    bb = (r_ref[0] * gl_ref[0].astype(F32)).astype(BF16)
    y = (jnp.dot(a, wo_ref[0:MLA_WIDTH, :], preferred_element_type=F32)
         + jnp.dot(bb, wo_ref[MLA_WIDTH:, :], preferred_element_type=F32))
    out_ref[0] = _residual(x_ref, y, modb_ref, modc_ref, is_ctx, d)


def _even_post(xc, modb, modc, o, gm, r, gl, w_out, n_latent):
    b_, t_, d = xc.shape
    tm = ROW_TILE
    tok = lambda w: pl.BlockSpec((1, tm, w), lambda b, t: (b, t, 0))
    return pl.pallas_call(
        functools.partial(_even_post_kernel, tm=tm, n_latent=n_latent, d=d),
        grid=(b_, t_ // tm),
        in_specs=[
            tok(d),
            pl.BlockSpec((1, 1, 3 * d), lambda b, t: (b, 0, 0)),
            _const_spec((1, 1, 3 * d)),
            tok(MLA_WIDTH), tok(MLA_WIDTH), tok(LRU_WIDTH), tok(LRU_WIDTH),
            _const_spec(w_out.shape),
        ],
        out_specs=tok(d),
        out_shape=jax.ShapeDtypeStruct((b_, t_, d), F32),
        compiler_params=_cparams(("parallel", "parallel")),
        name="even_post",
    )(xc, modb, modc, o, gm, r, gl, w_out)


def _odd_post_kernel(x_ref, modb_ref, modc_ref, o_ref, g_ref, wo_ref, fg_ref, out_ref,
                     *, tm, n_latent, d, final):
    is_ctx = _row_is_ctx(tm, n_latent)
    a = o_ref[0] * g_ref[0]
    y = jnp.dot(a, wo_ref[...], preferred_element_type=F32)
    xn = _residual(x_ref, y, modb_ref, modc_ref, is_ctx, d)
    if final:
        xn = _rms(xn, fg_ref[...])
    out_ref[0] = xn


def _odd_post(xc, modb, modc, o, g, w_out, final_g, n_latent, final):
    b_, t_, d = xc.shape
    tm = FINAL_ROW_TILE if final else ROW_TILE
    n_rows = n_latent if final else t_
    tok = lambda w: pl.BlockSpec((1, tm, w), lambda b, t: (b, t, 0))
    return pl.pallas_call(
        functools.partial(_odd_post_kernel, tm=tm, n_latent=n_latent, d=d, final=final),
        grid=(b_, n_rows // tm),
        in_specs=[
            tok(d),
            pl.BlockSpec((1, 1, 3 * d), lambda b, t: (b, 0, 0)),
            _const_spec((1, 1, 3 * d)),
            tok(NA_WIDTH), tok(NA_WIDTH),
            _const_spec(w_out.shape),
            _const_spec((1, d)),
        ],
        out_specs=tok(d),
        out_shape=jax.ShapeDtypeStruct((b_, n_rows, d), F32),
        compiler_params=_cparams(("parallel", "parallel")),
        name="odd_post_final" if final else "odd_post",
    )(xc, modb, modc, o, g, w_out, final_g)


def _odd_pre_kernel(x_ref, modb_ref, modc_ref, g_ref, win_ref, q_out, k_out, v_out, g_out,
                    *, tm, n_latent, d):
    is_ctx = _row_is_ctx(tm, n_latent)
    h = _modulated_norm(x_ref[0], g_ref[...], modb_ref, modc_ref, is_ctx, d).astype(BF16)
    w = NA_WIDTH
    hd = NA_HEAD_DIM
    q_t = (jnp.dot(h, win_ref[:, 0:w], preferred_element_type=F32) * (hd ** -0.5 * LOG2E)).T
    zeros = jnp.zeros((hd, tm), BF16)
    for hh in range(NA_HEADS):
        qh = q_t[hh * hd:(hh + 1) * hd].astype(BF16)
        q_out[0, hh * LANES:(hh + 1) * LANES, :] = jnp.concatenate(
            [qh, zeros] if hh % 2 == 0 else [zeros, qh], axis=0)
    k_out[0] = jnp.dot(h, win_ref[:, w:2 * w], preferred_element_type=F32).astype(BF16)
    v_t = jnp.dot(h, win_ref[:, 2 * w:3 * w], preferred_element_type=F32).T
    ones_rows = jnp.ones((V_EXT_ROWS - LANES, NA_VCHUNK), BF16)
    for p in range(NA_HEADS // 2):
        for cc in range(tm // NA_VCHUNK):
            v_out[0, p, cc, 0:LANES, :] = v_t[p * LANES:(p + 1) * LANES,
                                              cc * NA_VCHUNK:(cc + 1) * NA_VCHUNK].astype(BF16)
            v_out[0, p, cc, LANES:V_EXT_ROWS, :] = ones_rows
    g_out[0] = _silu(jnp.dot(h, win_ref[:, 3 * w:4 * w], preferred_element_type=F32)).astype(BF16)


def _odd_pre_call(xc, modb, modc, g, w_in, n_latent):
    b_, t_, d = xc.shape
    tm = ROW_TILE
    tok = lambda w: pl.BlockSpec((1, tm, w), lambda b, t: (b, t, 0))
    return pl.pallas_call(
        functools.partial(_odd_pre_kernel, tm=tm, n_latent=n_latent, d=d),
        grid=(b_, t_ // tm),
        in_specs=[
            tok(d),
            pl.BlockSpec((1, 1, 3 * d), lambda b, t: (b, 0, 0)),
            _const_spec((1, 1, 3 * d)),
            _const_spec((1, d)),
            _const_spec(w_in.shape),
        ],
        out_specs=[
            pl.BlockSpec((1, NA_HEADS * LANES, tm), lambda b, t: (b, 0, t)),
            tok(NA_WIDTH),
            pl.BlockSpec((1, NA_HEADS // 2, tm // NA_VCHUNK, V_EXT_ROWS, NA_VCHUNK),
                         lambda b, t: (b, 0, t, 0, 0)),
            tok(NA_WIDTH),
        ],
        out_shape=[
            jax.ShapeDtypeStruct((b_, NA_HEADS * LANES, t_), BF16),
            jax.ShapeDtypeStruct((b_, t_, NA_WIDTH), BF16),
            jax.ShapeDtypeStruct((b_, NA_HEADS // 2, t_ // NA_VCHUNK, V_EXT_ROWS, NA_VCHUNK), BF16),
            jax.ShapeDtypeStruct((b_, t_, NA_WIDTH), BF16),
        ],
        compiler_params=_cparams(("parallel", "parallel")),
        name="odd_pre",
    )(xc, modb, modc, g, w_in)


def _na_kernel(q_ref, k_ref, v_ref, *rest, nq, nkl, n_ctx, rows, k_ctx0, v_ctx0):
    local = nkl > 0
    if local:
        slab_ref, idx_ref, o_ref = rest[0:3]
        scratch = rest[3:]
        blk = pl.program_id(2)
        k_row0 = jnp.clip(blk * NA_QROWS - NA_WIN_R // 2, 0, rows - NA_KROWS)
        kstart = pl.multiple_of(k_row0 * GRID_W, NA_VCHUNK)
        v_loc0 = k_row0 * GRID_W // NA_VCHUNK
        typ = jnp.where(blk == 0, 0, jnp.where(blk == rows // NA_QROWS - 1, 2, 1))
    else:
        o_ref = rest[0]
        scratch = rest[1:]
    s_buf, p_buf, m_ref = scratch[0:2], scratch[2:4], scratch[4:6]

    def score(j):
        q_t = q_ref[0, j * LANES:(j + 1) * LANES, :]
        s_c = jnp.dot(k_ref[0, k_ctx0:k_ctx0 + n_ctx, :], q_t, preferred_element_type=F32)
        s_buf[j][nkl:nkl + n_ctx, :] = s_c
        m = jnp.max(s_c, axis=0, keepdims=True)
        if local:
            s_buf[j][0:nkl, :] = jnp.dot(k_ref[0, pl.ds(kstart, nkl), :], q_t, preferred_element_type=F32)
            n_pair = nq // LANES
            parts = []
            for a in range(n_pair):
                part = jnp.full((GRID_W, LANES), -jnp.inf, F32)
                for kr in range(NA_KROWS):
                    rows_kr = slice(kr * GRID_W, (kr + 1) * GRID_W)
                    cols_a = slice(a * LANES, (a + 1) * LANES)
                    t = s_buf[j][rows_kr, cols_a] + slab_ref[0, j, idx_ref[typ, kr * n_pair + a]]
                    s_buf[j][rows_kr, cols_a] = t
                    part = jnp.maximum(part, t)
                parts.append(jnp.max(part, axis=0, keepdims=True))
            m = jnp.maximum(m, jnp.concatenate(parts, axis=1))
        m_ref[j][...] = m

    def prob(j):
        m = jnp.broadcast_to(m_ref[j][...], (PROB_ROWS, nq))
        for r in range(0, nkl + n_ctx, PROB_ROWS):
            p_buf[j][r:r + PROB_ROWS, :] = jnp.exp2(s_buf[j][r:r + PROB_ROWS, :] - m).astype(BF16)

    def value(j):
        acc = None
        for cc in range((nkl + n_ctx) // NA_VCHUNK):
            in_window = cc < nkl // NA_VCHUNK
            v_t = v_ref[0, 0, v_loc0 + cc] if in_window else v_ref[0, 0, v_ctx0 + cc - nkl // NA_VCHUNK]
            pv = jnp.dot(v_t, p_buf[j][cc * NA_VCHUNK:(cc + 1) * NA_VCHUNK, :],
                         preferred_element_type=F32)
            acc = pv if acc is None else acc + pv
        return acc

    score(0)
    score(1)
    prob(0)
    a0 = value(0)
    prob(1)
    a1 = value(1)
    o0 = a0[0:NA_HEAD_DIM] / a0[LANES:LANES + 1]
    o1 = a1[NA_HEAD_DIM:LANES] / a1[LANES:LANES + 1]
    o_ref[0] = jnp.concatenate([o0, o1], axis=0).T.astype(BF16)


def _na_bias_tables(rpb, rows):
    n_heads = rpb.shape[0]
    pad = GRID_W - NA_WIN_C
    rp = jnp.pad(rpb * LOG2E, ((0, 0), (0, 0), (pad, pad)))
    toep = jnp.stack([rp[:, :, GRID_W - 1 - qc:2 * GRID_W - 1 - qc] for qc in range(GRID_W)], axis=3)
    col = np.arange(GRID_W)
    cs = np.clip(col - NA_WIN_C // 2, 0, GRID_W - NA_WIN_C)
    valid_c = (col[:, None] >= cs[None, :]) & (col[:, None] < cs[None, :] + NA_WIN_C)
    toep = jnp.where(valid_c, toep, NEG_BIG)
    n_dr = 2 * NA_WIN_R - 1
    slabs = jnp.concatenate([toep, jnp.full((n_heads, 1, GRID_W, GRID_W), NEG_BIG, F32)], axis=1)
    idx = np.full((3, NA_KROWS, NA_QROWS), n_dr, np.int32)
    for typ, r0 in enumerate((0, 2 * NA_QROWS, rows - NA_QROWS)):
        ks = int(np.clip(r0 - NA_WIN_R // 2, 0, rows - NA_KROWS))
        for kr in range(NA_KROWS):
            for qr in range(NA_QROWS):
                r = r0 + qr
                rs = int(np.clip(r - NA_WIN_R // 2, 0, rows - NA_WIN_R))
                if rs <= ks + kr < rs + NA_WIN_R:
                    idx[typ, kr, qr] = ks + kr - r + NA_WIN_R - 1
    pairs = idx.reshape(3, NA_KROWS, NA_QROWS // 2, 2)
    uniq, inverse = np.unique(pairs.reshape(-1, 2), axis=0, return_inverse=True)
    table = jnp.concatenate([jnp.take(slabs, jnp.asarray(uniq[:, 0]), axis=1),
                             jnp.take(slabs, jnp.asarray(uniq[:, 1]), axis=1)], axis=-1)
    table = table.reshape(n_heads // 2, 2, uniq.shape[0], GRID_W, 2 * GRID_W)
    return table, jnp.asarray(inverse.reshape(3, NA_KROWS * (NA_QROWS // 2)).astype(np.int32))


def _na_scratch(n_keys, nq):
    return ([pltpu.VMEM((n_keys, nq), F32)] * 2 + [pltpu.VMEM((n_keys, nq), BF16)] * 2
            + [pltpu.VMEM((1, nq), F32)] * 2)


def _na_attention(q_t, k, v_t, bias, n_latent, need_ctx):
    b_, t_, _ = k.shape
    n_ctx = t_ - n_latent
    rows = n_latent // GRID_W
    nq = NA_QROWS * GRID_W
    nkl = NA_KROWS * GRID_W
    n_blocks = rows // NA_QROWS
    n_pairs = NA_HEADS // 2
    n_vc = t_ // NA_VCHUNK
    slab_table, slab_idx = bias

    o_lat = pl.pallas_call(
        functools.partial(_na_kernel, nq=nq, nkl=nkl, n_ctx=n_ctx, rows=rows, k_ctx0=n_latent,
                          v_ctx0=n_latent // NA_VCHUNK),
        grid=(n_pairs, b_, n_blocks),
        in_specs=[
            pl.BlockSpec((1, 2 * LANES, nq), lambda p, b, i: (b, p, i)),
            pl.BlockSpec((1, t_, LANES), lambda p, b, i: (b, 0, p)),
            pl.BlockSpec((1, 1, n_vc, V_EXT_ROWS, NA_VCHUNK), lambda p, b, i: (b, p, 0, 0, 0)),
            pl.BlockSpec((1,) + slab_table.shape[1:], lambda p, b, i: (p, 0, 0, 0, 0)),
            pl.BlockSpec(memory_space=pltpu.SMEM),
        ],
        out_specs=pl.BlockSpec((1, nq, LANES), lambda p, b, i: (b, i, p)),
        out_shape=jax.ShapeDtypeStruct((b_, n_latent, NA_WIDTH), BF16),
        scratch_shapes=_na_scratch(nkl + n_ctx, nq),
        compiler_params=_cparams(("parallel", "parallel", "arbitrary")),
        name="na_attn",
    )(q_t, k, v_t, slab_table, slab_idx)
    if not need_ctx:
        return o_lat
    cb = n_latent // n_ctx
    o_ctx = pl.pallas_call(
        functools.partial(_na_kernel, nq=n_ctx, nkl=0, n_ctx=n_ctx, rows=rows, k_ctx0=0, v_ctx0=0),
        grid=(n_pairs, b_),
        in_specs=[
            pl.BlockSpec((1, 2 * LANES, n_ctx), lambda p, b: (b, p, cb)),
            pl.BlockSpec((1, n_ctx, LANES), lambda p, b: (b, cb, p)),
            pl.BlockSpec((1, 1, n_ctx // NA_VCHUNK, V_EXT_ROWS, NA_VCHUNK), lambda p, b: (b, p, cb, 0, 0)),
        ],
        out_specs=pl.BlockSpec((1, n_ctx, LANES), lambda p, b: (b, 0, p)),
        out_shape=jax.ShapeDtypeStruct((b_, n_ctx, NA_WIDTH), BF16),
        scratch_shapes=_na_scratch(n_ctx, n_ctx),
        compiler_params=_cparams(("parallel", "parallel")),
        name="na_attn_ctx",
    )(q_t, k, v_t)
    return jnp.concatenate([o_lat, o_ctx], axis=1)


def _rope_swap_index():
    half = MLA_ROPE // 4
    idx = np.arange(MLA_ROPE)
    return np.where((idx // half) % 2 == 0, idx + half, idx - half)


def _rope_tables(n_latent, n_ctx):
    n_freq = MLA_ROPE // 4
    inv = ROPE_THETA ** (-jnp.arange(n_freq, dtype=F32) / n_freq)
    t = jnp.arange(n_latent, dtype=jnp.int32)
    ang_r = (t // GRID_W).astype(F32)[:, None] * inv
    ang_c = (t % GRID_W).astype(F32)[:, None] * inv
    cr, sr, cc, sc = jnp.cos(ang_r), jnp.sin(ang_r), jnp.cos(ang_c), jnp.sin(ang_c)
    cos32 = jnp.concatenate([cr, cr, cc, cc], axis=-1)
    sin32 = jnp.concatenate([-sr, sr, -sc, sc], axis=-1)
    ones = jnp.ones((n_latent, MLA_NOPE), F32)
    zeros = jnp.zeros((n_latent, LANES - MLA_NOPE - MLA_ROPE), F32)
    cos_l = jnp.concatenate([ones, cos32, zeros], axis=-1)
    sin_l = jnp.concatenate([0 * ones, sin32, zeros], axis=-1)
    cos_c = jnp.concatenate([jnp.ones((n_ctx, MLA_NOPE + MLA_ROPE), F32),
                             jnp.zeros((n_ctx, LANES - MLA_NOPE - MLA_ROPE), F32)], axis=-1)
    sin_c = jnp.zeros((n_ctx, LANES), F32)
    return jnp.concatenate([cos_l, cos_c], axis=0), jnp.concatenate([sin_l, sin_c], axis=0)


def _even_weights(i, ev_w_in, mla_q_norm, mla_w_uq, mla_kv_norm, mla_w_ukv, lru_conv_w, lru_conv_b,
                  lru_wa, lru_ba, lru_wx, lru_bx, lru_lambda, ev_w_out):
    sw = _rope_swap_index()
    d = ev_w_in.shape[1]
    w_in = ev_w_in[i]
    o_cq, o_ckv, o_kr = MLA_Q_RANK, MLA_Q_RANK + MLA_KV_RANK, MLA_Q_RANK + MLA_KV_RANK + MLA_ROPE
    kr = w_in[:, o_ckv:o_kr]
    pad_lo = jnp.zeros((d, MLA_NOPE), F32)
    pad_hi = jnp.zeros((d, LANES - MLA_NOPE - MLA_ROPE), F32)
    w_in_ext = jnp.concatenate(
        [w_in[:, :o_ckv], pad_lo, kr, pad_hi, pad_lo, kr[:, sw], pad_hi, w_in[:, o_kr:]], axis=1)
    wq = mla_w_uq[i].reshape(MLA_Q_RANK, MLA_HEADS, MLA_NOPE + MLA_ROPE)
    zq = jnp.zeros((MLA_Q_RANK, MLA_HEADS, LANES - MLA_NOPE - MLA_ROPE), F32)
    wqa = jnp.concatenate([wq, zq], axis=-1).reshape(MLA_Q_RANK, MLA_HEADS * LANES)
    wqb = jnp.concatenate([jnp.zeros((MLA_Q_RANK, MLA_HEADS, MLA_NOPE), F32),
                           wq[:, :, MLA_NOPE:][:, :, sw], zq], axis=-1).reshape(MLA_Q_RANK, MLA_HEADS * LANES)
    wkv = mla_w_ukv[i].reshape(MLA_KV_RANK, MLA_HEADS, MLA_NOPE + MLA_V)
    wk = jnp.concatenate([wkv[:, :, :MLA_NOPE], jnp.zeros((MLA_KV_RANK, MLA_HEADS, LANES - MLA_NOPE), F32)],
                         axis=-1).reshape(MLA_KV_RANK, MLA_HEADS * LANES)
    wv = wkv[:, :, MLA_NOPE:].reshape(MLA_KV_RANK, MLA_WIDTH)
    ng = LRU_WIDTH // LANES
    per = LANES // LRU_BLOCK

    def blockdiag(w):
        w = w.reshape(2, ng, per, LRU_BLOCK, LRU_BLOCK)
        eye = jnp.eye(per, dtype=F32)
        return jnp.einsum('dgpkj,pq->dgpkqj', w, eye).reshape(2, ng, LANES, LANES)

    w_gate = jnp.concatenate([blockdiag(lru_wa[i]), blockdiag(lru_wx[i])], axis=-1).astype(BF16)
    b_gate = jnp.concatenate([lru_ba[i].reshape(2, ng, 1, LANES), lru_bx[i].reshape(2, ng, 1, LANES)], axis=-1)
    return dict(
        w_in=w_in_ext.astype(BF16), q_norm=mla_q_norm[i][None], wqa=wqa.astype(BF16), wqb=wqb.astype(BF16),
        kv_norm=mla_kv_norm[i][None], wk=wk.astype(BF16), wv=wv.astype(BF16),
        conv_w=lru_conv_w[i], conv_b=lru_conv_b[i][None], w_gate=w_gate, b_gate=b_gate,
        softplus=jax.nn.softplus(-lru_lambda[i]).reshape(2, ng, 1, LANES),
        w_out=ev_w_out[i].astype(BF16))


def kernel(x, c, ctx, c_ctx, ada_w, ada_b, norm_g, ev_w_in, mla_q_norm, mla_w_uq, mla_kv_norm, mla_w_ukv,
           lru_conv_w, lru_conv_b, lru_wa, lru_ba, lru_wx, lru_bx, lru_lambda, ev_w_out, od_w_in, na_rpb,
           od_w_out, final_norm_g):
    b_, s_, d = x.shape
    n_ctx = ctx.shape[1]
    depth = ada_w.shape[0]
    rows = s_ // GRID_W
    assert b_ <= 7 and s_ % GRID_W == 0 and rows % NA_QROWS == 0 and rows >= NA_KROWS
    assert (s_ + n_ctx) % ROW_TILE == 0 and s_ % FINAL_ROW_TILE == 0 and s_ % MLA_TQ == 0
    assert ROW_TILE == MLA_TK and n_ctx <= MLA_TK and s_ % n_ctx == 0 and n_ctx % LRU_CHUNK == 0 and s_ % LRU_CHUNK == 0
    assert n_ctx % NA_VCHUNK == 0 and ROW_TILE % NA_VCHUNK == 0 and rows >= 3 * NA_QROWS
    assert (NA_QROWS // 2 * GRID_W) % NA_VCHUNK == 0 and depth % 2 == 0

    cond = jnp.zeros((8, d), F32).at[:b_].set(c).at[b_].set(c_ctx)
    mod = _adaln(cond, ada_w, ada_b)
    xc = jnp.concatenate([x, ctx], axis=1)
    cos_t, sin_t = _rope_tables(s_, n_ctx)

    for layer in range(depth):
        need_ctx = layer < depth - 1
        i = layer // 2
        modb = mod[layer, :b_][:, None, :]
        modc = mod[layer, b_][None, None, :]
        g = norm_g[layer][None]
        if layer % 2 == 0:
            wts = _even_weights(i, ev_w_in, mla_q_norm, mla_w_uq, mla_kv_norm, mla_w_ukv, lru_conv_w,
                                lru_conv_b, lru_wa, lru_ba, lru_wx, lru_bx, lru_lambda, ev_w_out)
            q_t, kc, v_t, gm, xl, gl = _even_pre(xc, modb, modc, g, wts, cos_t, sin_t, s_)
            o = _mla_attention(q_t, kc, v_t, s_)
            r = _lru(xl, wts, s_)
            xc = _even_post(xc, modb, modc, o, gm, r, gl, wts["w_out"], s_)
        else:
            q, k, v, gg = _odd_pre_call(xc, modb, modc, g, od_w_in[i].astype(BF16), s_)
            bias = _na_bias_tables(na_rpb[i], rows)
            o = _na_attention(q, k, v, bias, s_, need_ctx)
            xc = _odd_post(xc, modb, modc, o, gg, od_w_out[i].astype(BF16), final_norm_g[None], s_,
                           final=not need_ctx)
    return xc
```

```python
import functools
import math

import numpy as np
import jax
import jax.numpy as jnp
from jax import lax
from jax.experimental import pallas as pl
from jax.experimental.pallas import tpu as pltpu

F32 = jnp.float32
BF16 = jnp.bfloat16

GRID_W = 64
RMS_EPS = 1e-6
ROPE_THETA = 10000.0
MLA_HEADS = 8
MLA_NOPE = 64
MLA_ROPE = 32
MLA_V = 64
MLA_Q_RANK = 256
MLA_KV_RANK = 128
MLA_WIDTH = MLA_HEADS * MLA_V
MLA_SCALE = (MLA_NOPE + MLA_ROPE) ** -0.5
LRU_WIDTH = 512
LRU_BLOCKS = 8
LRU_BLOCK = LRU_WIDTH // LRU_BLOCKS
LRU_CONV = 4
LRU_C = 8.0
NA_HEADS = 16
NA_HEAD_DIM = 64
NA_WIDTH = NA_HEADS * NA_HEAD_DIM
NA_WIN_R = 8
NA_WIN_C = 16

LANES = 128
LOG2E = 1.4426950408889634
NEG_BIG = -1e30
VMEM_LIMIT = 56 * 1024 * 1024

ROW_TILE = 768
FINAL_ROW_TILE = 512
MLA_TQ = 1024
MLA_S_SLOTS = 4
MLA_WQ = 512
MLA_TK = 768
MLA_UNROLL = 2
PROB_ROWS = 16
V_EXT_ROWS = LANES + 16
NA_QROWS = 8
NA_KROWS = 16
NA_BLOCKS_PER_STEP = 2
NA_VCHUNK = 256
LRU_CHUNK = 256


def _cparams(sem, flags=None):
    return pltpu.CompilerParams(dimension_semantics=sem, vmem_limit_bytes=VMEM_LIMIT, flags=flags)


def _silu(v):
    return v * jax.nn.sigmoid(v)


def _adaln_kernel(cond_ref, w_ref, b_ref, o_ref):
    a = _silu(cond_ref[...])
    o_ref[0] = jnp.dot(a, w_ref[0], preferred_element_type=F32) + b_ref[0]


def _adaln(cond, ada_w, ada_b):
    depth, d, d3 = ada_w.shape
    tn = 1024
    return pl.pallas_call(
        _adaln_kernel,
        grid=(depth, d3 // tn),
        in_specs=[
            pl.BlockSpec((8, d), lambda l, n: (0, 0)),
            pl.BlockSpec((1, d, tn), lambda l, n: (l, 0, n)),
            pl.BlockSpec((1, 1, tn), lambda l, n: (l, 0, n)),
        ],
        out_specs=pl.BlockSpec((1, 8, tn), lambda l, n: (l, 0, n)),
        out_shape=jax.ShapeDtypeStruct((depth, 8, d3), F32),
        compiler_params=_cparams(("arbitrary", "arbitrary")),
        name="adaln",
    )(cond, ada_w, ada_b.reshape(depth, 1, d3))


def _modulated_norm(x, g, modb_ref, modc_ref, is_ctx, d):
    ms = jnp.mean(x * x, axis=-1, keepdims=True)
    y = x * lax.rsqrt(ms + RMS_EPS) * g
    shift = jnp.where(is_ctx, modc_ref[0, :, 0:d], modb_ref[0, :, 0:d])
    scale = jnp.where(is_ctx, modc_ref[0, :, d:2 * d], modb_ref[0, :, d:2 * d])
    return y * (1.0 + scale) + shift


def _row_is_ctx(tm, n_latent):
    row = pl.program_id(1) * tm + lax.broadcasted_iota(jnp.int32, (tm, 1), 0)
    return row >= n_latent


def _rms(v, g):
    ms = jnp.mean(v * v, axis=-1, keepdims=True)
    return v * lax.rsqrt(ms + RMS_EPS) * g


def _even_pre_kernel(x_ref, modb_ref, modc_ref, g_ref, win_ref, qn_ref, wqa_ref, wqb_ref,
                     kvn_ref, wk_ref, wv_ref, cos_ref, sin_ref,
                     q_out, k_out, v_out, gm_out, xl_out, gl_out, *, tm, n_latent, d):
    is_ctx = _row_is_ctx(tm, n_latent)
    h = _modulated_norm(x_ref[0], g_ref[...], modb_ref, modc_ref, is_ctx, d).astype(BF16)

    def proj(lo, hi):
        return jnp.dot(h, win_ref[:, lo:hi], preferred_element_type=F32)

    cos = cos_ref[...]
    sin = sin_ref[...]
    cq = _rms(proj(0, 256), qn_ref[...]).astype(BF16)
    qa = jnp.dot(cq, wqa_ref[...], preferred_element_type=F32)
    qb = jnp.dot(cq, wqb_ref[...], preferred_element_type=F32)
    ckv = _rms(proj(256, 384), kvn_ref[...]).astype(BF16)
    kk = jnp.dot(ckv, wk_ref[...], preferred_element_type=F32)
    k_rope = proj(384, 512) * cos + proj(512, 640) * sin
    v_t = jnp.dot(ckv, wv_ref[...], preferred_element_type=F32).T
    ones_rows = jnp.ones((V_EXT_ROWS - LANES, tm), BF16)
    for p in range(MLA_HEADS // 2):
        v_out[0, p, 0, 0:LANES, :] = v_t[p * LANES:(p + 1) * LANES].astype(BF16)
        v_out[0, p, 0, LANES:V_EXT_ROWS, :] = ones_rows
    for hh in range(MLA_HEADS):
        sl = slice(hh * LANES, (hh + 1) * LANES)
        q = (qa[:, sl] * cos + qb[:, sl] * sin) * (MLA_SCALE * LOG2E)
        q_out[0, sl, :] = q.T.astype(BF16)
        k_out[0, :, sl] = (kk[:, sl] + k_rope).astype(BF16)
    gm_out[0] = _silu(proj(640, 1152)).astype(BF16)
    xl_out[0] = proj(1152, 1664)
    gl_out[0] = _silu(proj(1664, 2176)).astype(BF16)


def _const_spec(shape):
    nd = len(shape)
    return pl.BlockSpec(shape, lambda b, t: (0,) * nd)


def _even_pre(xc, modb, modc, g, wts, cos_t, sin_t, n_latent):
    b_, t_, d = xc.shape
    tm = ROW_TILE
    kern = functools.partial(_even_pre_kernel, tm=tm, n_latent=n_latent, d=d)
    tok = lambda w: pl.BlockSpec((1, tm, w), lambda b, t: (b, t, 0))
    hw = MLA_HEADS * LANES
    return pl.pallas_call(
        kern,
        grid=(b_, t_ // tm),
        in_specs=[
            tok(d),
            pl.BlockSpec((1, 1, 3 * d), lambda b, t: (b, 0, 0)),
            _const_spec((1, 1, 3 * d)),
            _const_spec((1, d)),
            _const_spec(wts["w_in"].shape),
            _const_spec((1, MLA_Q_RANK)),
            _const_spec(wts["wqa"].shape),
            _const_spec(wts["wqb"].shape),
            _const_spec((1, MLA_KV_RANK)),
            _const_spec(wts["wk"].shape),
            _const_spec(wts["wv"].shape),
            pl.BlockSpec((tm, LANES), lambda b, t: (t, 0)),
            pl.BlockSpec((tm, LANES), lambda b, t: (t, 0)),
        ],
        out_specs=[pl.BlockSpec((1, hw, tm), lambda b, t: (b, 0, t)), tok(hw),
                   pl.BlockSpec((1, MLA_HEADS // 2, 1, V_EXT_ROWS, tm), lambda b, t: (b, 0, t, 0, 0)),
                   tok(MLA_WIDTH), tok(LRU_WIDTH), tok(LRU_WIDTH)],
        out_shape=[
            jax.ShapeDtypeStruct((b_, hw, t_), BF16),
            jax.ShapeDtypeStruct((b_, t_, hw), BF16),
            jax.ShapeDtypeStruct((b_, MLA_HEADS // 2, t_ // tm, V_EXT_ROWS, tm), BF16),
            jax.ShapeDtypeStruct((b_, t_, MLA_WIDTH), BF16),
            jax.ShapeDtypeStruct((b_, t_, LRU_WIDTH), F32),
            jax.ShapeDtypeStruct((b_, t_, LRU_WIDTH), BF16),
        ],
        compiler_params=_cparams(("parallel", "parallel")),
        name="even_pre",
    )(xc, modb, modc, g, wts["w_in"], wts["q_norm"], wts["wqa"], wts["wqb"], wts["kv_norm"],
      wts["wk"], wts["wv"], cos_t, sin_t)


def _mla_attn_t_kernel(q_ref, k_ref, v_ref, o_ref, *scratch, tq, wq, tk, nk, v_lane0):
    nh = tq // wq
    n_sub = 2 * nh
    n_total = n_sub * nk
    ns = min(n_sub, MLA_S_SLOTS)
    s_buf, p_buf = scratch[0:ns], scratch[ns:ns + 2]
    acc_ref, m_ref, al_ref = (scratch[ns + 2 + r * n_sub:ns + 2 + (r + 1) * n_sub] for r in range(3))

    def score(c, i):
        j, h = divmod(i, nh)
        start = c * tk if isinstance(c, int) else pl.multiple_of(c * tk, LANES)
        k = k_ref[0, pl.ds(start, tk), j * LANES:(j + 1) * LANES]
        q_t = q_ref[0, j * LANES:(j + 1) * LANES, h * wq:(h + 1) * wq]
        s = jnp.dot(k, q_t, preferred_element_type=F32)
        m = m_ref[i][...]
        m_new = jnp.maximum(m, jnp.max(s, axis=0, keepdims=True))
        al_ref[i][...] = jnp.exp2(m - m_new)
        m_ref[i][...] = m_new
        s_buf[i % ns][...] = s

    def prob(c, i):
        m = jnp.broadcast_to(m_ref[i][...], (PROB_ROWS, wq))
        for r in range(0, tk, PROB_ROWS):
            p = jnp.exp2(s_buf[i % ns][r:r + PROB_ROWS, :] - m)
            p_buf[i % 2][r:r + PROB_ROWS, :] = p.astype(BF16)

    def value(c, i):
        v_t = v_ref[0, 0, c, :, v_lane0:v_lane0 + tk]
        pv = jnp.dot(v_t, p_buf[i % 2][...], preferred_element_type=F32)
        acc_ref[i][...] = al_ref[i][...] * acc_ref[i][...] + pv

    for i in range(n_sub):
        m_ref[i][...] = jnp.full(m_ref[i].shape, -jnp.inf, F32)
        acc_ref[i][...] = jnp.zeros(acc_ref[i].shape, F32)

    def group(c, r):
        for stage, lag in ((value, 3), (prob, 2), (score, 0)):
            dc, i = divmod(r - lag, n_sub)
            if isinstance(c, int) and not 0 <= (c + dc) * n_sub + i < n_total:
                continue
            stage(c + dc, i)

    for g in range(3):
        group(0, g)

    def body(it, _):
        for r in range(3, 3 + MLA_UNROLL * n_sub):
            group(it * MLA_UNROLL, r)
        return 0

    trips = (nk - 1) // MLA_UNROLL
    lax.fori_loop(0, trips, body, 0)
    for g in range(n_sub * MLA_UNROLL * trips + 3, n_total + 3):
        group(0, g)

    for h in range(nh):
        a0 = acc_ref[h][...]
        a1 = acc_ref[nh + h][...]
        o0 = a0[0:MLA_V] / a0[LANES:LANES + 1]
        o1 = a1[MLA_V:LANES] / a1[LANES:LANES + 1]
        o_ref[0, h * wq:(h + 1) * wq, :] = jnp.concatenate([o0, o1], axis=0).T.astype(BF16)


def _mla_scratch(tq, wq, tk):
    n_sub = 2 * (tq // wq)
    return ([pltpu.VMEM((tk, wq), F32)] * min(n_sub, MLA_S_SLOTS) + [pltpu.VMEM((tk, wq), BF16)] * 2
            + [pltpu.VMEM((V_EXT_ROWS, wq), F32)] * n_sub + [pltpu.VMEM((1, wq), F32)] * (2 * n_sub))


def _mla_attention(q_t, kc, v_t, n_latent):
    b_, t_, _ = kc.shape
    n_pairs = MLA_HEADS // 2
    tq, wq, tk = MLA_TQ, MLA_WQ, MLA_TK
    n_ctx = t_ - n_latent
    nk = t_ // tk
    o_lat = pl.pallas_call(
        functools.partial(_mla_attn_t_kernel, tq=tq, wq=wq, tk=tk, nk=nk, v_lane0=0),
        grid=(b_, n_pairs, n_latent // tq),
        in_specs=[
            pl.BlockSpec((1, 2 * LANES, tq), lambda b, p, i: (b, p, i)),
            pl.BlockSpec((1, t_, 2 * LANES), lambda b, p, i: (b, 0, p)),
            pl.BlockSpec((1, 1, nk, V_EXT_ROWS, tk), lambda b, p, i: (b, p, 0, 0, 0)),
        ],
        out_specs=pl.BlockSpec((1, tq, LANES), lambda b, p, i: (b, i, p)),
        out_shape=jax.ShapeDtypeStruct((b_, n_latent, MLA_WIDTH), BF16),
        scratch_shapes=_mla_scratch(tq, wq, tk),
        compiler_params=_cparams(("parallel", "parallel", "arbitrary")),
        name="mla_attn_latent",
    )(q_t, kc, v_t)
    cb = n_latent // n_ctx
    o_ctx = pl.pallas_call(
        functools.partial(_mla_attn_t_kernel, tq=n_ctx, wq=n_ctx, tk=n_ctx, nk=1, v_lane0=tk - n_ctx),
        grid=(b_, n_pairs),
        in_specs=[
            pl.BlockSpec((1, 2 * LANES, n_ctx), lambda b, p: (b, p, cb)),
            pl.BlockSpec((1, n_ctx, 2 * LANES), lambda b, p: (b, cb, p)),
            pl.BlockSpec((1, 1, 1, V_EXT_ROWS, tk), lambda b, p: (b, p, nk - 1, 0, 0)),
        ],
        out_specs=pl.BlockSpec((1, n_ctx, LANES), lambda b, p: (b, 0, p)),
        out_shape=jax.ShapeDtypeStruct((b_, n_ctx, MLA_WIDTH), BF16),
        scratch_shapes=_mla_scratch(n_ctx, n_ctx, n_ctx),
        compiler_params=_cparams(("parallel", "parallel")),
        name="mla_attn_ctx",
    )(q_t, kc, v_t)
    return jnp.concatenate([o_lat, o_ctx], axis=1)


def _lru_kernel(x_ref, cw_ref, cb_ref, wg_ref, bg_ref, sp_ref, r_ref, hf_ref, hb_ref,
                *, tc, n_latent, n_ctx):
    nl = n_latent // tc
    ncx = n_ctx // tc
    row = lax.broadcasted_iota(jnp.int32, (tc, 1), 0)

    def conv_chunk(c0, seg_lo, seg_hi):
        xc = x_ref[0, pl.ds(c0, tc), :]
        has_prev = c0 > seg_lo
        has_next = c0 + tc < seg_hi
        p0 = pl.multiple_of(jnp.where(has_prev, c0 - 8, c0), 8)
        n0 = pl.multiple_of(jnp.where(has_next, c0 + tc, c0), 8)
        prev = jnp.where(has_prev, x_ref[0, pl.ds(p0, 8), :], 0.0)
        nxt = jnp.where(has_next, x_ref[0, pl.ds(n0, 8), :], 0.0)
        xm1 = jnp.where(row >= 1, pltpu.roll(xc, 1, 0), prev[7:8])
        xm2 = jnp.where(row >= 2, pltpu.roll(xc, 2, 0), jnp.where(row == 1, prev[7:8], prev[6:7]))
        xp1 = jnp.where(row <= tc - 2, pltpu.roll(xc, tc - 1, 0), nxt[0:1])
        return (cw_ref[0:1] * xm2 + cw_ref[1:2] * xm1 + cw_ref[2:3] * xc + cw_ref[3:4] * xp1
                + cb_ref[...])

    def coeffs(u, d):
        z = jnp.dot(u.astype(BF16), wg_ref[d], preferred_element_type=F32) + bg_ref[d]
        r = jax.nn.sigmoid(z[:, :LANES])
        i = jax.nn.sigmoid(z[:, LANES:])
        log_a = -LRU_C * r * sp_ref[d]
        a = jnp.exp(log_a)
        t = jnp.tanh(log_a)
        uu = jnp.sqrt(-2.0 * t / (1.0 - t)) * (i * u)
        return a, uu

    def scan_chunk(a, u, h_in, reverse):
        s = 1
        while s < tc:
            if s % 8 == 0:
                one, zero = jnp.ones((s, LANES), F32), jnp.zeros((s, LANES), F32)
                if reverse:
                    a_s = jnp.concatenate([a[s:], one], axis=0)
                    u_s = jnp.concatenate([u[s:], zero], axis=0)
                else:
                    a_s = jnp.concatenate([one, a[:tc - s]], axis=0)
                    u_s = jnp.concatenate([zero, u[:tc - s]], axis=0)
            elif reverse:
                keep = row < tc - s
                a_s = jnp.where(keep, pltpu.roll(a, tc - s, 0), 1.0)
                u_s = jnp.where(keep, pltpu.roll(u, tc - s, 0), 0.0)
            else:
                keep = row >= s
                a_s = jnp.where(keep, pltpu.roll(a, s, 0), 1.0)
                u_s = jnp.where(keep, pltpu.roll(u, s, 0), 0.0)
            u = a * u_s + u
            a = a * a_s
            s *= 2
        return a * h_in + u

    def chunk_start(i):
        return pl.multiple_of(jnp.where(i < ncx, n_latent + i * tc, (i - ncx) * tc), tc)

    def seg_bounds(i):
        lo = jnp.where(i < ncx, n_latent, 0)
        hi = jnp.where(i < ncx, n_latent + n_ctx, n_latent)
        return lo, hi

    hf_ref[...] = jnp.zeros_like(hf_ref)
    hb_ref[...] = jnp.zeros_like(hb_ref)
    r_ref[...] = jnp.zeros_like(r_ref)

    def step(i, _):
        c0 = chunk_start(i)
        lo, hi = seg_bounds(i)
        a, uu = coeffs(conv_chunk(c0, lo, hi), 0)
        h = scan_chunk(a, uu, hf_ref[0:1], False)
        hf_ref[...] = jnp.broadcast_to(h[tc - 1:tc], hf_ref.shape)
        r_ref[0, pl.ds(c0, tc), :] = r_ref[0, pl.ds(c0, tc), :] + h
        ii = jnp.where(i < ncx, ncx - 1 - i, ncx + (nl - 1 - (i - ncx)))
        c0 = chunk_start(ii)
        lo, hi = seg_bounds(ii)
        a, uu = coeffs(conv_chunk(c0, lo, hi), 1)
        h = scan_chunk(a, uu, hb_ref[0:1], True)
        hb_ref[...] = jnp.broadcast_to(h[0:1], hb_ref.shape)
        r_ref[0, pl.ds(c0, tc), :] = r_ref[0, pl.ds(c0, tc), :] + h
        return 0

    lax.fori_loop(0, ncx + nl, step, 0)


def _lru(x_lru, wts, n_latent):
    b_, t_, w = x_lru.shape
    ng = w // LANES
    kern = functools.partial(_lru_kernel, tc=LRU_CHUNK, n_latent=n_latent, n_ctx=t_ - n_latent)
    return pl.pallas_call(
        kern,
        grid=(b_, ng),
        in_specs=[
            pl.BlockSpec((1, t_, LANES), lambda b, g: (b, 0, g)),
            pl.BlockSpec((LRU_CONV, LANES), lambda b, g: (0, g)),
            pl.BlockSpec((1, LANES), lambda b, g: (0, g)),
            pl.BlockSpec((2, None, LANES, 2 * LANES), lambda b, g: (0, g, 0, 0)),
            pl.BlockSpec((2, None, 1, 2 * LANES), lambda b, g: (0, g, 0, 0)),
            pl.BlockSpec((2, None, 1, LANES), lambda b, g: (0, g, 0, 0)),
        ],
        out_specs=pl.BlockSpec((1, t_, LANES), lambda b, g: (b, 0, g)),
        out_shape=jax.ShapeDtypeStruct((b_, t_, w), F32),
        scratch_shapes=[pltpu.VMEM((8, LANES), F32), pltpu.VMEM((8, LANES), F32)],
        compiler_params=_cparams(("parallel", "parallel")),
        name="rglru",
    )(x_lru, wts["conv_w"], wts["conv_b"], wts["w_gate"], wts["b_gate"], wts["softplus"])


def _residual(x_ref, y, modb_ref, modc_ref, is_ctx, d):
    gate = jnp.where(is_ctx, modc_ref[0, :, 2 * d:3 * d], modb_ref[0, :, 2 * d:3 * d])
    return x_ref[0] + gate * y


def _even_post_kernel(x_ref, modb_ref, modc_ref, o_ref, gm_ref, r_ref, gl_ref, wo_ref, out_ref,
                      *, tm, n_latent, d):
    is_ctx = _row_is_ctx(tm, n_latent)
    a = o_ref[0] * gm_ref[0]
    bb = (r_ref[0] * gl_ref[0].astype(F32)).astype(BF16)
    y = (jnp.dot(a, wo_ref[0:MLA_WIDTH, :], preferred_element_type=F32)
         + jnp.dot(bb, wo_ref[MLA_WIDTH:, :], preferred_element_type=F32))
    out_ref[0] = _residual(x_ref, y, modb_ref, modc_ref, is_ctx, d)


def _even_post(xc, modb, modc, o, gm, r, gl, w_out, n_latent):
    b_, t_, d = xc.shape
    tm = ROW_TILE
    tok = lambda w: pl.BlockSpec((1, tm, w), lambda b, t: (b, t, 0))
    return pl.pallas_call(
        functools.partial(_even_post_kernel, tm=tm, n_latent=n_latent, d=d),
        grid=(b_, t_ // tm),
        in_specs=[
            tok(d),
            pl.BlockSpec((1, 1, 3 * d), lambda b, t: (b, 0, 0)),
            _const_spec((1, 1, 3 * d)),
            tok(MLA_WIDTH), tok(MLA_WIDTH), tok(LRU_WIDTH), tok(LRU_WIDTH),
            _const_spec(w_out.shape),
        ],
        out_specs=tok(d),
        out_shape=jax.ShapeDtypeStruct((b_, t_, d), F32),
        compiler_params=_cparams(("parallel", "parallel")),
        name="even_post",
    )(xc, modb, modc, o, gm, r, gl, w_out)


def _odd_post_kernel(x_ref, modb_ref, modc_ref, o_ref, g_ref, wo_ref, fg_ref, out_ref,
                     *, tm, n_latent, d, final):
    is_ctx = _row_is_ctx(tm, n_latent)
    a = o_ref[0] * g_ref[0]
    y = jnp.dot(a, wo_ref[...], preferred_element_type=F32)
    xn = _residual(x_ref, y, modb_ref, modc_ref, is_ctx, d)
    if final:
        xn = _rms(xn, fg_ref[...])
    out_ref[0] = xn


def _odd_post(xc, modb, modc, o, g, w_out, final_g, n_latent, final):
    b_, t_, d = xc.shape
    tm = FINAL_ROW_TILE if final else ROW_TILE
    n_rows = n_latent if final else t_
    tok = lambda w: pl.BlockSpec((1, tm, w), lambda b, t: (b, t, 0))
    return pl.pallas_call(
        functools.partial(_odd_post_kernel, tm=tm, n_latent=n_latent, d=d, final=final),
        grid=(b_, n_rows // tm),
        in_specs=[
            tok(d),
            pl.BlockSpec((1, 1, 3 * d), lambda b, t: (b, 0, 0)),
            _const_spec((1, 1, 3 * d)),
            tok(NA_WIDTH), tok(NA_WIDTH),
            _const_spec(w_out.shape),
            _const_spec((1, d)),
        ],
        out_specs=tok(d),
        out_shape=jax.ShapeDtypeStruct((b_, n_rows, d), F32),
        compiler_params=_cparams(("parallel", "parallel")),
        name="odd_post_final" if final else "odd_post",
    )(xc, modb, modc, o, g, w_out, final_g)


def _odd_pre_kernel(x_ref, modb_ref, modc_ref, g_ref, win_ref, q_out, k_out, v_out, g_out,
                    *, tm, n_latent, d):
    is_ctx = _row_is_ctx(tm, n_latent)
    h = _modulated_norm(x_ref[0], g_ref[...], modb_ref, modc_ref, is_ctx, d).astype(BF16)
    w = NA_WIDTH
    hd = NA_HEAD_DIM
    q_t = (jnp.dot(h, win_ref[:, 0:w], preferred_element_type=F32) * (hd ** -0.5 * LOG2E)).T
    zeros = jnp.zeros((hd, tm), BF16)
    for hh in range(NA_HEADS):
        qh = q_t[hh * hd:(hh + 1) * hd].astype(BF16)
        q_out[0, hh * LANES:(hh + 1) * LANES, :] = jnp.concatenate(
            [qh, zeros] if hh % 2 == 0 else [zeros, qh], axis=0)
    k_out[0] = jnp.dot(h, win_ref[:, w:2 * w], preferred_element_type=F32).astype(BF16)
    v_t = jnp.dot(h, win_ref[:, 2 * w:3 * w], preferred_element_type=F32).T
    ones_rows = jnp.ones((V_EXT_ROWS - LANES, NA_VCHUNK), BF16)
    for p in range(NA_HEADS // 2):
        for cc in range(tm // NA_VCHUNK):
            v_out[0, p, cc, 0:LANES, :] = v_t[p * LANES:(p + 1) * LANES,
                                              cc * NA_VCHUNK:(cc + 1) * NA_VCHUNK].astype(BF16)
            v_out[0, p, cc, LANES:V_EXT_ROWS, :] = ones_rows
    g_out[0] = _silu(jnp.dot(h, win_ref[:, 3 * w:4 * w], preferred_element_type=F32)).astype(BF16)


def _odd_pre_call(xc, modb, modc, g, w_in, n_latent):
    b_, t_, d = xc.shape
    tm = ROW_TILE
    tok = lambda w: pl.BlockSpec((1, tm, w), lambda b, t: (b, t, 0))
    return pl.pallas_call(
        functools.partial(_odd_pre_kernel, tm=tm, n_latent=n_latent, d=d),
        grid=(b_, t_ // tm),
        in_specs=[
            tok(d),
            pl.BlockSpec((1, 1, 3 * d), lambda b, t: (b, 0, 0)),
            _const_spec((1, 1, 3 * d)),
            _const_spec((1, d)),
            _const_spec(w_in.shape),
        ],
        out_specs=[
            pl.BlockSpec((1, NA_HEADS * LANES, tm), lambda b, t: (b, 0, t)),
            tok(NA_WIDTH),
            pl.BlockSpec((1, NA_HEADS // 2, tm // NA_VCHUNK, V_EXT_ROWS, NA_VCHUNK),
                         lambda b, t: (b, 0, t, 0, 0)),
            tok(NA_WIDTH),
        ],
        out_shape=[
            jax.ShapeDtypeStruct((b_, NA_HEADS * LANES, t_), BF16),
            jax.ShapeDtypeStruct((b_, t_, NA_WIDTH), BF16),
            jax.ShapeDtypeStruct((b_, NA_HEADS // 2, t_ // NA_VCHUNK, V_EXT_ROWS, NA_VCHUNK), BF16),
            jax.ShapeDtypeStruct((b_, t_, NA_WIDTH), BF16),
        ],
        compiler_params=_cparams(("parallel", "parallel")),
        name="odd_pre",
    )(xc, modb, modc, g, w_in)


def _na_kernel(q_ref, k_ref, v_ref, *rest, nq, nb, nkl, n_ctx, rows, k_ctx0, v_ctx0):
    local = nkl > 0
    n_sub = 2 * nb
    if local:
        slab_ref, idx_ref, o_ref = rest[0:3]
        scratch = rest[3:]
        kstart, v_loc0, typ = [], [], []
        for t in range(nb):
            blk = pl.program_id(2) * nb + t
            k_row0 = jnp.clip(blk * NA_QROWS - NA_WIN_R // 2, 0, rows - NA_KROWS)
            kstart.append(pl.multiple_of(k_row0 * GRID_W, NA_VCHUNK))
            v_loc0.append(k_row0 * GRID_W // NA_VCHUNK)
            typ.append(jnp.where(blk == 0, 0, jnp.where(blk == rows // NA_QROWS - 1, 2, 1)))
    else:
        o_ref = rest[0]
        scratch = rest[1:]
    s_buf, p_buf, m_ref = (scratch[r * n_sub:(r + 1) * n_sub] for r in range(3))
    accs = [None] * n_sub

    def score(u):
        t, j = divmod(u, 2)
        q_t = q_ref[0, j * LANES:(j + 1) * LANES, t * nq:(t + 1) * nq]
        s_c = jnp.dot(k_ref[0, k_ctx0:k_ctx0 + n_ctx, :], q_t, preferred_element_type=F32)
        s_buf[u][nkl:nkl + n_ctx, :] = s_c
        m = jnp.max(s_c, axis=0, keepdims=True)
        if local:
            s_buf[u][0:nkl, :] = jnp.dot(k_ref[0, pl.ds(kstart[t], nkl), :], q_t,
                                         preferred_element_type=F32)
            n_pair = nq // LANES
            parts = []
            for a in range(n_pair):
                part = jnp.full((GRID_W, LANES), -jnp.inf, F32)
                for kr in range(NA_KROWS):
                    rows_kr = slice(kr * GRID_W, (kr + 1) * GRID_W)
                    cols_a = slice(a * LANES, (a + 1) * LANES)
                    tt = s_buf[u][rows_kr, cols_a] + slab_ref[0, j, idx_ref[typ[t], kr * n_pair + a]]
                    s_buf[u][rows_kr, cols_a] = tt
                    part = jnp.maximum(part, tt)
                parts.append(jnp.max(part, axis=0, keepdims=True))
            m = jnp.maximum(m, jnp.concatenate(parts, axis=1))
        m_ref[u][...] = m

    def prob(u):
        m = jnp.broadcast_to(m_ref[u][...], (PROB_ROWS, nq))
        for r in range(0, nkl + n_ctx, PROB_ROWS):
            p_buf[u][r:r + PROB_ROWS, :] = jnp.exp2(s_buf[u][r:r + PROB_ROWS, :] - m).astype(BF16)

    def value(u):
        t = u // 2
        acc = None
        for cc in range((nkl + n_ctx) // NA_VCHUNK):
            in_window = cc < nkl // NA_VCHUNK
            v_t = (v_ref[0, 0, v_loc0[t] + cc] if in_window
                   else v_ref[0, 0, v_ctx0 + cc - nkl // NA_VCHUNK])
            pv = jnp.dot(v_t, p_buf[u][cc * NA_VCHUNK:(cc + 1) * NA_VCHUNK, :],
                         preferred_element_type=F32)
            acc = pv if acc is None else acc + pv
        accs[u] = acc

    for g in range(n_sub + 3):
        for stage, lag in ((value, 3), (prob, 2), (score, 0)):
            if 0 <= g - lag < n_sub:
                stage(g - lag)
    for t in range(nb):
        a0, a1 = accs[2 * t], accs[2 * t + 1]
        o0 = a0[0:NA_HEAD_DIM] / a0[LANES:LANES + 1]
        o1 = a1[NA_HEAD_DIM:LANES] / a1[LANES:LANES + 1]
        o_ref[0, t * nq:(t + 1) * nq, :] = jnp.concatenate([o0, o1], axis=0).T.astype(BF16)


def _na_bias_tables(rpb, rows):
    n_heads = rpb.shape[0]
    pad = GRID_W - NA_WIN_C
    rp = jnp.pad(rpb * LOG2E, ((0, 0), (0, 0), (pad, pad)))
    toep = jnp.stack([rp[:, :, GRID_W - 1 - qc:2 * GRID_W - 1 - qc] for qc in range(GRID_W)], axis=3)
    col = np.arange(GRID_W)
    cs = np.clip(col - NA_WIN_C // 2, 0, GRID_W - NA_WIN_C)
    valid_c = (col[:, None] >= cs[None, :]) & (col[:, None] < cs[None, :] + NA_WIN_C)
    toep = jnp.where(valid_c, toep, NEG_BIG)
    n_dr = 2 * NA_WIN_R - 1
    slabs = jnp.concatenate([toep, jnp.full((n_heads, 1, GRID_W, GRID_W), NEG_BIG, F32)], axis=1)
    idx = np.full((3, NA_KROWS, NA_QROWS), n_dr, np.int32)
    for typ, r0 in enumerate((0, 2 * NA_QROWS, rows - NA_QROWS)):
        ks = int(np.clip(r0 - NA_WIN_R // 2, 0, rows - NA_KROWS))
        for kr in range(NA_KROWS):
            for qr in range(NA_QROWS):
                r = r0 + qr
                rs = int(np.clip(r - NA_WIN_R // 2, 0, rows - NA_WIN_R))
                if rs <= ks + kr < rs + NA_WIN_R:
                    idx[typ, kr, qr] = ks + kr - r + NA_WIN_R - 1
    pairs = idx.reshape(3, NA_KROWS, NA_QROWS // 2, 2)
    uniq, inverse = np.unique(pairs.reshape(-1, 2), axis=0, return_inverse=True)
    table = jnp.concatenate([jnp.take(slabs, jnp.asarray(uniq[:, 0]), axis=1),
                             jnp.take(slabs, jnp.asarray(uniq[:, 1]), axis=1)], axis=-1)
    table = table.reshape(n_heads // 2, 2, uniq.shape[0], GRID_W, 2 * GRID_W)
    return table, jnp.asarray(inverse.reshape(3, NA_KROWS * (NA_QROWS // 2)).astype(np.int32))


def _na_scratch(n_keys, nq, nb):
    return ([pltpu.VMEM((n_keys, nq), F32)] * (2 * nb) + [pltpu.VMEM((n_keys, nq), BF16)] * (2 * nb)
            + [pltpu.VMEM((1, nq), F32)] * (2 * nb))


def _na_attention(q_t, k, v_t, bias, n_latent, need_ctx):
    b_, t_, _ = k.shape
    n_ctx = t_ - n_latent
    rows = n_latent // GRID_W
    nq = NA_QROWS * GRID_W
    nkl = NA_KROWS * GRID_W
    n_blocks = rows // NA_QROWS
    n_pairs = NA_HEADS // 2
    n_vc = t_ // NA_VCHUNK
    nb = NA_BLOCKS_PER_STEP
    assert n_blocks % nb == 0
    slab_table, slab_idx = bias

    o_lat = pl.pallas_call(
        functools.partial(_na_kernel, nq=nq, nb=nb, nkl=nkl, n_ctx=n_ctx, rows=rows, k_ctx0=n_latent,
                          v_ctx0=n_latent // NA_VCHUNK),
        grid=(n_pairs, b_, n_blocks // nb),
        in_specs=[
            pl.BlockSpec((1, 2 * LANES, nb * nq), lambda p, b, i: (b, p, i)),
            pl.BlockSpec((1, t_, LANES), lambda p, b, i: (b, 0, p)),
            pl.BlockSpec((1, 1, n_vc, V_EXT_ROWS, NA_VCHUNK), lambda p, b, i: (b, p, 0, 0, 0)),
            pl.BlockSpec((1,) + slab_table.shape[1:], lambda p, b, i: (p, 0, 0, 0, 0)),
            pl.BlockSpec(memory_space=pltpu.SMEM),
        ],
        out_specs=pl.BlockSpec((1, nb * nq, LANES), lambda p, b, i: (b, i, p)),
        out_shape=jax.ShapeDtypeStruct((b_, n_latent, NA_WIDTH), BF16),
        scratch_shapes=_na_scratch(nkl + n_ctx, nq, nb),
        compiler_params=_cparams(("parallel", "parallel", "arbitrary")),
        name="na_attn",
    )(q_t, k, v_t, slab_table, slab_idx)
    if not need_ctx:
        return o_lat
    cb = n_latent // n_ctx
    o_ctx = pl.pallas_call(
        functools.partial(_na_kernel, nq=n_ctx, nb=1, nkl=0, n_ctx=n_ctx, rows=rows, k_ctx0=0, v_ctx0=0),
        grid=(n_pairs, b_),
        in_specs=[
            pl.BlockSpec((1, 2 * LANES, n_ctx), lambda p, b: (b, p, cb)),
            pl.BlockSpec((1, n_ctx, LANES), lambda p, b: (b, cb, p)),
            pl.BlockSpec((1, 1, n_ctx // NA_VCHUNK, V_EXT_ROWS, NA_VCHUNK), lambda p, b: (b, p, cb, 0, 0)),
        ],
        out_specs=pl.BlockSpec((1, n_ctx, LANES), lambda p, b: (b, 0, p)),
        out_shape=jax.ShapeDtypeStruct((b_, n_ctx, NA_WIDTH), BF16),
        scratch_shapes=_na_scratch(n_ctx, n_ctx, 1),
        compiler_params=_cparams(("parallel", "parallel")),
        name="na_attn_ctx",
    )(q_t, k, v_t)
    return jnp.concatenate([o_lat, o_ctx], axis=1)


def _rope_swap_index():
    half = MLA_ROPE // 4
    idx = np.arange(MLA_ROPE)
    return np.where((idx // half) % 2 == 0, idx + half, idx - half)


def _rope_tables(n_latent, n_ctx):
    n_freq = MLA_ROPE // 4
    inv = ROPE_THETA ** (-jnp.arange(n_freq, dtype=F32) / n_freq)
    t = jnp.arange(n_latent, dtype=jnp.int32)
    ang_r = (t // GRID_W).astype(F32)[:, None] * inv
    ang_c = (t % GRID_W).astype(F32)[:, None] * inv
    cr, sr, cc, sc = jnp.cos(ang_r), jnp.sin(ang_r), jnp.cos(ang_c), jnp.sin(ang_c)
    cos32 = jnp.concatenate([cr, cr, cc, cc], axis=-1)
    sin32 = jnp.concatenate([-sr, sr, -sc, sc], axis=-1)
    ones = jnp.ones((n_latent, MLA_NOPE), F32)
    zeros = jnp.zeros((n_latent, LANES - MLA_NOPE - MLA_ROPE), F32)
    cos_l = jnp.concatenate([ones, cos32, zeros], axis=-1)
    sin_l = jnp.concatenate([0 * ones, sin32, zeros], axis=-1)
    cos_c = jnp.concatenate([jnp.ones((n_ctx, MLA_NOPE + MLA_ROPE), F32),
                             jnp.zeros((n_ctx, LANES - MLA_NOPE - MLA_ROPE), F32)], axis=-1)
    sin_c = jnp.zeros((n_ctx, LANES), F32)
    return jnp.concatenate([cos_l, cos_c], axis=0), jnp.concatenate([sin_l, sin_c], axis=0)


def _even_weights(i, ev_w_in, mla_q_norm, mla_w_uq, mla_kv_norm, mla_w_ukv, lru_conv_w, lru_conv_b,
                  lru_wa, lru_ba, lru_wx, lru_bx, lru_lambda, ev_w_out):
    sw = _rope_swap_index()
    d = ev_w_in.shape[1]
    w_in = ev_w_in[i]
    o_cq, o_ckv, o_kr = MLA_Q_RANK, MLA_Q_RANK + MLA_KV_RANK, MLA_Q_RANK + MLA_KV_RANK + MLA_ROPE
    kr = w_in[:, o_ckv:o_kr]
    pad_lo = jnp.zeros((d, MLA_NOPE), F32)
    pad_hi = jnp.zeros((d, LANES - MLA_NOPE - MLA_ROPE), F32)
    w_in_ext = jnp.concatenate(
        [w_in[:, :o_ckv], pad_lo, kr, pad_hi, pad_lo, kr[:, sw], pad_hi, w_in[:, o_kr:]], axis=1)
    wq = mla_w_uq[i].reshape(MLA_Q_RANK, MLA_HEADS, MLA_NOPE + MLA_ROPE)
    zq = jnp.zeros((MLA_Q_RANK, MLA_HEADS, LANES - MLA_NOPE - MLA_ROPE), F32)
    wqa = jnp.concatenate([wq, zq], axis=-1).reshape(MLA_Q_RANK, MLA_HEADS * LANES)
    wqb = jnp.concatenate([jnp.zeros((MLA_Q_RANK, MLA_HEADS, MLA_NOPE), F32),
                           wq[:, :, MLA_NOPE:][:, :, sw], zq], axis=-1).reshape(MLA_Q_RANK, MLA_HEADS * LANES)
    wkv = mla_w_ukv[i].reshape(MLA_KV_RANK, MLA_HEADS, MLA_NOPE + MLA_V)
    wk = jnp.concatenate([wkv[:, :, :MLA_NOPE], jnp.zeros((MLA_KV_RANK, MLA_HEADS, LANES - MLA_NOPE), F32)],
                         axis=-1).reshape(MLA_KV_RANK, MLA_HEADS * LANES)
    wv = wkv[:, :, MLA_NOPE:].reshape(MLA_KV_RANK, MLA_WIDTH)
    ng = LRU_WIDTH // LANES
    per = LANES // LRU_BLOCK

    def blockdiag(w):
        w = w.reshape(2, ng, per, LRU_BLOCK, LRU_BLOCK)
        eye = jnp.eye(per, dtype=F32)
        return jnp.einsum('dgpkj,pq->dgpkqj', w, eye).reshape(2, ng, LANES, LANES)

    w_gate = jnp.concatenate([blockdiag(lru_wa[i]), blockdiag(lru_wx[i])], axis=-1).astype(BF16)
    b_gate = jnp.concatenate([lru_ba[i].reshape(2, ng, 1, LANES), lru_bx[i].reshape(2, ng, 1, LANES)], axis=-1)
    return dict(
        w_in=w_in_ext.astype(BF16), q_norm=mla_q_norm[i][None], wqa=wqa.astype(BF16), wqb=wqb.astype(BF16),
        kv_norm=mla_kv_norm[i][None], wk=wk.astype(BF16), wv=wv.astype(BF16),
        conv_w=lru_conv_w[i], conv_b=lru_conv_b[i][None], w_gate=w_gate, b_gate=b_gate,
        softplus=jax.nn.softplus(-lru_lambda[i]).reshape(2, ng, 1, LANES),
        w_out=ev_w_out[i].astype(BF16))


def kernel(x, c, ctx, c_ctx, ada_w, ada_b, norm_g, ev_w_in, mla_q_norm, mla_w_uq, mla_kv_norm, mla_w_ukv,
           lru_conv_w, lru_conv_b, lru_wa, lru_ba, lru_wx, lru_bx, lru_lambda, ev_w_out, od_w_in, na_rpb,
           od_w_out, final_norm_g):
    b_, s_, d = x.shape
    n_ctx = ctx.shape[1]
    depth = ada_w.shape[0]
    rows = s_ // GRID_W
    assert b_ <= 7 and s_ % GRID_W == 0 and rows % NA_QROWS == 0 and rows >= NA_KROWS
    assert (s_ + n_ctx) % ROW_TILE == 0 and s_ % FINAL_ROW_TILE == 0 and s_ % MLA_TQ == 0
    assert ROW_TILE == MLA_TK and n_ctx <= MLA_TK and s_ % n_ctx == 0 and n_ctx % LRU_CHUNK == 0 and s_ % LRU_CHUNK == 0
    assert n_ctx % NA_VCHUNK == 0 and ROW_TILE % NA_VCHUNK == 0 and rows >= 3 * NA_QROWS
    assert (NA_QROWS // 2 * GRID_W) % NA_VCHUNK == 0 and depth % 2 == 0

    cond = jnp.zeros((8, d), F32).at[:b_].set(c).at[b_].set(c_ctx)
    mod = _adaln(cond, ada_w, ada_b)
    xc = jnp.concatenate([x, ctx], axis=1)
    cos_t, sin_t = _rope_tables(s_, n_ctx)

    for layer in range(depth):
        need_ctx = layer < depth - 1
        i = layer // 2
        modb = mod[layer, :b_][:, None, :]
        modc = mod[layer, b_][None, None, :]
        g = norm_g[layer][None]
        if layer % 2 == 0:
            wts = _even_weights(i, ev_w_in, mla_q_norm, mla_w_uq, mla_kv_norm, mla_w_ukv, lru_conv_w,
                                lru_conv_b, lru_wa, lru_ba, lru_wx, lru_bx, lru_lambda, ev_w_out)
            q_t, kc, v_t, gm, xl, gl = _even_pre(xc, modb, modc, g, wts, cos_t, sin_t, s_)
            o = _mla_attention(q_t, kc, v_t, s_)
            r = _lru(xl, wts, s_)
            xc = _even_post(xc, modb, modc, o, gm, r, gl, wts["w_out"], s_)
        else:
            q, k, v, gg = _odd_pre_call(xc, modb, modc, g, od_w_in[i].astype(BF16), s_)
            bias = _na_bias_tables(na_rpb[i], rows)
            o = _na_attention(q, k, v, bias, s_, need_ctx)
            xc = _odd_post(xc, modb, modc, o, gg, od_w_out[i].astype(BF16), final_norm_g[None], s_,
                           final=not need_ctx)
    return xc
```

```python
import functools
import math

import numpy as np
import jax
import jax.numpy as jnp
from jax import lax
from jax.experimental import pallas as pl
from jax.experimental.pallas import tpu as pltpu

F32 = jnp.float32
BF16 = jnp.bfloat16

GRID_W = 64
RMS_EPS = 1e-6
ROPE_THETA = 10000.0
MLA_HEADS = 8
MLA_NOPE = 64
MLA_ROPE = 32
MLA_V = 64
MLA_Q_RANK = 256
MLA_KV_RANK = 128
MLA_WIDTH = MLA_HEADS * MLA_V
MLA_SCALE = (MLA_NOPE + MLA_ROPE) ** -0.5
LRU_WIDTH = 512
LRU_BLOCKS = 8
LRU_BLOCK = LRU_WIDTH // LRU_BLOCKS
LRU_CONV = 4
LRU_C = 8.0
NA_HEADS = 16
NA_HEAD_DIM = 64
NA_WIDTH = NA_HEADS * NA_HEAD_DIM
NA_WIN_R = 8
NA_WIN_C = 16

LANES = 128
LOG2E = 1.4426950408889634
NEG_BIG = -1e30
VMEM_LIMIT = 56 * 1024 * 1024

ROW_TILE = 768
FINAL_ROW_TILE = 512
MLA_TQ = 1024
MLA_S_SLOTS = 4
MLA_WQ = 512
MLA_TK = 768
MLA_UNROLL = 5
PROB_ROWS = 16
V_EXT_ROWS = LANES + 16
NA_QROWS = 8
NA_KROWS = 16
NA_BLOCKS_PER_STEP = 2
NA_VCHUNK = 256
LRU_CHUNK = 256


def _cparams(sem, flags=None):
    return pltpu.CompilerParams(dimension_semantics=sem, vmem_limit_bytes=VMEM_LIMIT, flags=flags)


def _silu(v):
    return v * jax.nn.sigmoid(v)


def _adaln_kernel(cond_ref, w_ref, b_ref, o_ref):
    a = _silu(cond_ref[...])
    o_ref[0] = jnp.dot(a, w_ref[0], preferred_element_type=F32) + b_ref[0]


def _adaln(cond, ada_w, ada_b):
    depth, d, d3 = ada_w.shape
    tn = 1024
    return pl.pallas_call(
        _adaln_kernel,
        grid=(depth, d3 // tn),
        in_specs=[
            pl.BlockSpec((8, d), lambda l, n: (0, 0)),
            pl.BlockSpec((1, d, tn), lambda l, n: (l, 0, n)),
            pl.BlockSpec((1, 1, tn), lambda l, n: (l, 0, n)),
        ],
        out_specs=pl.BlockSpec((1, 8, tn), lambda l, n: (l, 0, n)),
        out_shape=jax.ShapeDtypeStruct((depth, 8, d3), F32),
        compiler_params=_cparams(("arbitrary", "arbitrary")),
        name="adaln",
    )(cond, ada_w, ada_b.reshape(depth, 1, d3))


def _modulated_norm(x, g, modb_ref, modc_ref, is_ctx, d):
    ms = jnp.mean(x * x, axis=-1, keepdims=True)
    y = x * lax.rsqrt(ms + RMS_EPS) * g
    shift = jnp.where(is_ctx, modc_ref[0, :, 0:d], modb_ref[0, :, 0:d])
    scale = jnp.where(is_ctx, modc_ref[0, :, d:2 * d], modb_ref[0, :, d:2 * d])
    return y * (1.0 + scale) + shift


def _row_is_ctx(tm, n_latent):
    row = pl.program_id(1) * tm + lax.broadcasted_iota(jnp.int32, (tm, 1), 0)
    return row >= n_latent


def _rms(v, g):
    ms = jnp.mean(v * v, axis=-1, keepdims=True)
    return v * lax.rsqrt(ms + RMS_EPS) * g


def _even_pre_kernel(x_ref, modb_ref, modc_ref, g_ref, win_ref, qn_ref, wqa_ref, wqb_ref,
                     kvn_ref, wk_ref, wv_ref, cos_ref, sin_ref,
                     q_out, k_out, v_out, gm_out, xl_out, gl_out, *, tm, n_latent, d):
    is_ctx = _row_is_ctx(tm, n_latent)
    h = _modulated_norm(x_ref[0], g_ref[...], modb_ref, modc_ref, is_ctx, d).astype(BF16)

    def proj(lo, hi):
        return jnp.dot(h, win_ref[:, lo:hi], preferred_element_type=F32)

    cos = cos_ref[...]
    sin = sin_ref[...]
    cq = _rms(proj(0, 256), qn_ref[...]).astype(BF16)
    qa = jnp.dot(cq, wqa_ref[...], preferred_element_type=F32)
    qb = jnp.dot(cq, wqb_ref[...], preferred_element_type=F32)
    ckv = _rms(proj(256, 384), kvn_ref[...]).astype(BF16)
    kk = jnp.dot(ckv, wk_ref[...], preferred_element_type=F32)
    k_rope = proj(384, 512) * cos + proj(512, 640) * sin
    v_t = jnp.dot(ckv, wv_ref[...], preferred_element_type=F32).T
    ones_rows = jnp.ones((V_EXT_ROWS - LANES, tm), BF16)
    for p in range(MLA_HEADS // 2):
        v_out[0, p, 0, 0:LANES, :] = v_t[p * LANES:(p + 1) * LANES].astype(BF16)
        v_out[0, p, 0, LANES:V_EXT_ROWS, :] = ones_rows
    for hh in range(MLA_HEADS):
        sl = slice(hh * LANES, (hh + 1) * LANES)
        q = (qa[:, sl] * cos + qb[:, sl] * sin) * (MLA_SCALE * LOG2E)
        q_out[0, sl, :] = q.T.astype(BF16)
        k_out[0, :, sl] = (kk[:, sl] + k_rope).astype(BF16)
    gm_out[0] = _silu(proj(640, 1152)).astype(BF16)
    xl_out[0] = proj(1152, 1664)
    gl_out[0] = _silu(proj(1664, 2176)).astype(BF16)


def _const_spec(shape):
    nd = len(shape)
    return pl.BlockSpec(shape, lambda b, t: (0,) * nd)


def _even_pre(xc, modb, modc, g, wts, cos_t, sin_t, n_latent):
    b_, t_, d = xc.shape
    tm = ROW_TILE
    kern = functools.partial(_even_pre_kernel, tm=tm, n_latent=n_latent, d=d)
    tok = lambda w: pl.BlockSpec((1, tm, w), lambda b, t: (b, t, 0))
    hw = MLA_HEADS * LANES
    return pl.pallas_call(
        kern,
        grid=(b_, t_ // tm),
        in_specs=[
            tok(d),
            pl.BlockSpec((1, 1, 3 * d), lambda b, t: (b, 0, 0)),
            _const_spec((1, 1, 3 * d)),
            _const_spec((1, d)),
            _const_spec(wts["w_in"].shape),
            _const_spec((1, MLA_Q_RANK)),
            _const_spec(wts["wqa"].shape),
            _const_spec(wts["wqb"].shape),
            _const_spec((1, MLA_KV_RANK)),
            _const_spec(wts["wk"].shape),
            _const_spec(wts["wv"].shape),
            pl.BlockSpec((tm, LANES), lambda b, t: (t, 0)),
            pl.BlockSpec((tm, LANES), lambda b, t: (t, 0)),
        ],
        out_specs=[pl.BlockSpec((1, hw, tm), lambda b, t: (b, 0, t)), tok(hw),
                   pl.BlockSpec((1, MLA_HEADS // 2, 1, V_EXT_ROWS, tm), lambda b, t: (b, 0, t, 0, 0)),
                   tok(MLA_WIDTH), tok(LRU_WIDTH), tok(LRU_WIDTH)],
        out_shape=[
            jax.ShapeDtypeStruct((b_, hw, t_), BF16),
            jax.ShapeDtypeStruct((b_, t_, hw), BF16),
            jax.ShapeDtypeStruct((b_, MLA_HEADS // 2, t_ // tm, V_EXT_ROWS, tm), BF16),
            jax.ShapeDtypeStruct((b_, t_, MLA_WIDTH), BF16),
            jax.ShapeDtypeStruct((b_, t_, LRU_WIDTH), F32),
            jax.ShapeDtypeStruct((b_, t_, LRU_WIDTH), BF16),
        ],
        compiler_params=_cparams(("parallel", "parallel")),
        name="even_pre",
    )(xc, modb, modc, g, wts["w_in"], wts["q_norm"], wts["wqa"], wts["wqb"], wts["kv_norm"],
      wts["wk"], wts["wv"], cos_t, sin_t)


def _mla_attn_t_kernel(q_ref, k_ref, v_ref, o_ref, *scratch, tq, wq, tk, nk, v_lane0):
    nh = tq // wq
    n_sub = 2 * nh
    n_total = n_sub * nk
    ns = min(n_sub, MLA_S_SLOTS)
    s_buf, p_buf = scratch[0:ns], scratch[ns:ns + 2]
    acc_ref, m_ref, al_ref = (scratch[ns + 2 + r * n_sub:ns + 2 + (r + 1) * n_sub] for r in range(3))

    def score(c, i):
        j, h = divmod(i, nh)
        start = c * tk if isinstance(c, int) else pl.multiple_of(c * tk, LANES)
        k = k_ref[0, pl.ds(start, tk), j * LANES:(j + 1) * LANES]
        q_t = q_ref[0, j * LANES:(j + 1) * LANES, h * wq:(h + 1) * wq]
        s = jnp.dot(k, q_t, preferred_element_type=F32)
        m = m_ref[i][...]
        m_new = jnp.maximum(m, jnp.max(s, axis=0, keepdims=True))
        al_ref[i][...] = jnp.exp2(m - m_new)
        m_ref[i][...] = m_new
        s_buf[i % ns][...] = s

    def prob(c, i):
        m = jnp.broadcast_to(m_ref[i][...], (PROB_ROWS, wq))
        for r in range(0, tk, PROB_ROWS):
            p = jnp.exp2(s_buf[i % ns][r:r + PROB_ROWS, :] - m)
            p_buf[i % 2][r:r + PROB_ROWS, :] = p.astype(BF16)

    def value(c, i):
        v_t = v_ref[0, 0, c, :, v_lane0:v_lane0 + tk]
        pv = jnp.dot(v_t, p_buf[i % 2][...], preferred_element_type=F32)
        acc_ref[i][...] = al_ref[i][...] * acc_ref[i][...] + pv

    for i in range(n_sub):
        m_ref[i][...] = jnp.full(m_ref[i].shape, -jnp.inf, F32)
        acc_ref[i][...] = jnp.zeros(acc_ref[i].shape, F32)

    def group(c, r):
        for stage, lag in ((value, 3), (prob, 2), (score, 0)):
            dc, i = divmod(r - lag, n_sub)
            if isinstance(c, int) and not 0 <= (c + dc) * n_sub + i < n_total:
                continue
            stage(c + dc, i)

    for g in range(3):
        group(0, g)

    def body(it, _):
        for r in range(3, 3 + MLA_UNROLL * n_sub):
            group(it * MLA_UNROLL, r)
        return 0

    trips = (nk - 1) // MLA_UNROLL
    lax.fori_loop(0, trips, body, 0)
    for g in range(n_sub * MLA_UNROLL * trips + 3, n_total + 3):
        group(0, g)

    for h in range(nh):
        a0 = acc_ref[h][...]
        a1 = acc_ref[nh + h][...]
        o0 = a0[0:MLA_V] / a0[LANES:LANES + 1]
        o1 = a1[MLA_V:LANES] / a1[LANES:LANES + 1]
        o_ref[0, h * wq:(h + 1) * wq, :] = jnp.concatenate([o0, o1], axis=0).T.astype(BF16)


def _mla_scratch(tq, wq, tk):
    n_sub = 2 * (tq // wq)
    return ([pltpu.VMEM((tk, wq), F32)] * min(n_sub, MLA_S_SLOTS) + [pltpu.VMEM((tk, wq), BF16)] * 2
            + [pltpu.VMEM((V_EXT_ROWS, wq), F32)] * n_sub + [pltpu.VMEM((1, wq), F32)] * (2 * n_sub))


def _mla_attention(q_t, kc, v_t, n_latent):
    b_, t_, _ = kc.shape
    n_pairs = MLA_HEADS // 2
    tq, wq, tk = MLA_TQ, MLA_WQ, MLA_TK
    n_ctx = t_ - n_latent
    nk = t_ // tk
    o_lat = pl.pallas_call(
        functools.partial(_mla_attn_t_kernel, tq=tq, wq=wq, tk=tk, nk=nk, v_lane0=0),
        grid=(b_, n_pairs, n_latent // tq),
        in_specs=[
            pl.BlockSpec((1, 2 * LANES, tq), lambda b, p, i: (b, p, i)),
            pl.BlockSpec((1, t_, 2 * LANES), lambda b, p, i: (b, 0, p)),
            pl.BlockSpec((1, 1, nk, V_EXT_ROWS, tk), lambda b, p, i: (b, p, 0, 0, 0)),
        ],
        out_specs=pl.BlockSpec((1, tq, LANES), lambda b, p, i: (b, i, p)),
        out_shape=jax.ShapeDtypeStruct((b_, n_latent, MLA_WIDTH), BF16),
        scratch_shapes=_mla_scratch(tq, wq, tk),
        compiler_params=_cparams(("parallel", "parallel", "arbitrary")),
        name="mla_attn_latent",
    )(q_t, kc, v_t)
    cb = n_latent // n_ctx
    o_ctx = pl.pallas_call(
        functools.partial(_mla_attn_t_kernel, tq=n_ctx, wq=n_ctx, tk=n_ctx, nk=1, v_lane0=tk - n_ctx),
        grid=(b_, n_pairs),
        in_specs=[
            pl.BlockSpec((1, 2 * LANES, n_ctx), lambda b, p: (b, p, cb)),
            pl.BlockSpec((1, n_ctx, 2 * LANES), lambda b, p: (b, cb, p)),
            pl.BlockSpec((1, 1, 1, V_EXT_ROWS, tk), lambda b, p: (b, p, nk - 1, 0, 0)),
        ],
        out_specs=pl.BlockSpec((1, n_ctx, LANES), lambda b, p: (b, 0, p)),
        out_shape=jax.ShapeDtypeStruct((b_, n_ctx, MLA_WIDTH), BF16),
        scratch_shapes=_mla_scratch(n_ctx, n_ctx, n_ctx),
        compiler_params=_cparams(("parallel", "parallel")),
        name="mla_attn_ctx",
    )(q_t, kc, v_t)
    return jnp.concatenate([o_lat, o_ctx], axis=1)


def _lru_kernel(x_ref, cw_ref, cb_ref, wg_ref, bg_ref, sp_ref, r_ref, hf_ref, hb_ref,
                *, tc, n_latent, n_ctx):
    nl = n_latent // tc
    ncx = n_ctx // tc
    row = lax.broadcasted_iota(jnp.int32, (tc, 1), 0)

    def conv_chunk(c0, seg_lo, seg_hi):
        xc = x_ref[0, pl.ds(c0, tc), :]
        has_prev = c0 > seg_lo
        has_next = c0 + tc < seg_hi
        p0 = pl.multiple_of(jnp.where(has_prev, c0 - 8, c0), 8)
        n0 = pl.multiple_of(jnp.where(has_next, c0 + tc, c0), 8)
        prev = jnp.where(has_prev, x_ref[0, pl.ds(p0, 8), :], 0.0)
        nxt = jnp.where(has_next, x_ref[0, pl.ds(n0, 8), :], 0.0)
        xm1 = jnp.where(row >= 1, pltpu.roll(xc, 1, 0), prev[7:8])
        xm2 = jnp.where(row >= 2, pltpu.roll(xc, 2, 0), jnp.where(row == 1, prev[7:8], prev[6:7]))
        xp1 = jnp.where(row <= tc - 2, pltpu.roll(xc, tc - 1, 0), nxt[0:1])
        return (cw_ref[0:1] * xm2 + cw_ref[1:2] * xm1 + cw_ref[2:3] * xc + cw_ref[3:4] * xp1
                + cb_ref[...])

    def coeffs(u, d):
        z = jnp.dot(u.astype(BF16), wg_ref[d], preferred_element_type=F32) + bg_ref[d]
        r = jax.nn.sigmoid(z[:, :LANES])
        i = jax.nn.sigmoid(z[:, LANES:])
        log_a = -LRU_C * r * sp_ref[d]
        a = jnp.exp(log_a)
        t = jnp.tanh(log_a)
        uu = jnp.sqrt(-2.0 * t / (1.0 - t)) * (i * u)
        return a, uu

    def scan_chunk(a, u, h_in, reverse):
        s = 1
        while s < tc:
            if s % 8 == 0:
                one, zero = jnp.ones((s, LANES), F32), jnp.zeros((s, LANES), F32)
                if reverse:
                    a_s = jnp.concatenate([a[s:], one], axis=0)
                    u_s = jnp.concatenate([u[s:], zero], axis=0)
                else:
                    a_s = jnp.concatenate([one, a[:tc - s]], axis=0)
                    u_s = jnp.concatenate([zero, u[:tc - s]], axis=0)
            elif reverse:
                keep = row < tc - s
                a_s = jnp.where(keep, pltpu.roll(a, tc - s, 0), 1.0)
                u_s = jnp.where(keep, pltpu.roll(u, tc - s, 0), 0.0)
            else:
                keep = row >= s
                a_s = jnp.where(keep, pltpu.roll(a, s, 0), 1.0)
                u_s = jnp.where(keep, pltpu.roll(u, s, 0), 0.0)
            u = a * u_s + u
            a = a * a_s
            s *= 2
        return a * h_in + u

    def chunk_start(i):
        return pl.multiple_of(jnp.where(i < ncx, n_latent + i * tc, (i - ncx) * tc), tc)

    def seg_bounds(i):
        lo = jnp.where(i < ncx, n_latent, 0)
        hi = jnp.where(i < ncx, n_latent + n_ctx, n_latent)
        return lo, hi

    hf_ref[...] = jnp.zeros_like(hf_ref)
    hb_ref[...] = jnp.zeros_like(hb_ref)
    r_ref[...] = jnp.zeros_like(r_ref)

    def step(i, _):
        c0 = chunk_start(i)
        lo, hi = seg_bounds(i)
        a, uu = coeffs(conv_chunk(c0, lo, hi), 0)
        h = scan_chunk(a, uu, hf_ref[0:1], False)
        hf_ref[...] = jnp.broadcast_to(h[tc - 1:tc], hf_ref.shape)
        r_ref[0, pl.ds(c0, tc), :] = r_ref[0, pl.ds(c0, tc), :] + h
        ii = jnp.where(i < ncx, ncx - 1 - i, ncx + (nl - 1 - (i - ncx)))
        c0 = chunk_start(ii)
        lo, hi = seg_bounds(ii)
        a, uu = coeffs(conv_chunk(c0, lo, hi), 1)
        h = scan_chunk(a, uu, hb_ref[0:1], True)
        hb_ref[...] = jnp.broadcast_to(h[0:1], hb_ref.shape)
        r_ref[0, pl.ds(c0, tc), :] = r_ref[0, pl.ds(c0, tc), :] + h
        return 0

    lax.fori_loop(0, ncx + nl, step, 0)


def _lru(x_lru, wts, n_latent):
    b_, t_, w = x_lru.shape
    ng = w // LANES
    kern = functools.partial(_lru_kernel, tc=LRU_CHUNK, n_latent=n_latent, n_ctx=t_ - n_latent)
    return pl.pallas_call(
        kern,
        grid=(b_, ng),
        in_specs=[
            pl.BlockSpec((1, t_, LANES), lambda b, g: (b, 0, g)),
            pl.BlockSpec((LRU_CONV, LANES), lambda b, g: (0, g)),
            pl.BlockSpec((1, LANES), lambda b, g: (0, g)),
            pl.BlockSpec((2, None, LANES, 2 * LANES), lambda b, g: (0, g, 0, 0)),
            pl.BlockSpec((2, None, 1, 2 * LANES), lambda b, g: (0, g, 0, 0)),
            pl.BlockSpec((2, None, 1, LANES), lambda b, g: (0, g, 0, 0)),
        ],
        out_specs=pl.BlockSpec((1, t_, LANES), lambda b, g: (b, 0, g)),
        out_shape=jax.ShapeDtypeStruct((b_, t_, w), F32),
        scratch_shapes=[pltpu.VMEM((8, LANES), F32), pltpu.VMEM((8, LANES), F32)],
        compiler_params=_cparams(("parallel", "parallel")),
        name="rglru",
    )(x_lru, wts["conv_w"], wts["conv_b"], wts["w_gate"], wts["b_gate"], wts["softplus"])


def _residual(x_ref, y, modb_ref, modc_ref, is_ctx, d):
    gate = jnp.where(is_ctx, modc_ref[0, :, 2 * d:3 * d], modb_ref[0, :, 2 * d:3 * d])
    return x_ref[0] + gate * y


def _even_post_kernel(x_ref, modb_ref, modc_ref, o_ref, gm_ref, r_ref, gl_ref, wo_ref, out_ref,
                      *, tm, n_latent, d):
    is_ctx = _row_is_ctx(tm, n_latent)
    a = o_ref[0] * gm_ref[0]
    bb = (r_ref[0] * gl_ref[0].astype(F32)).astype(BF16)
    y = (jnp.dot(a, wo_ref[0:MLA_WIDTH, :], preferred_element_type=F32)
         + jnp.dot(bb, wo_ref[MLA_WIDTH:, :], preferred_element_type=F32))
    out_ref[0] = _residual(x_ref, y, modb_ref, modc_ref, is_ctx, d)


def _even_post(xc, modb, modc, o, gm, r, gl, w_out, n_latent):
    b_, t_, d = xc.shape
    tm = ROW_TILE
    tok = lambda w: pl.BlockSpec((1, tm, w), lambda b, t: (b, t, 0))
    return pl.pallas_call(
        functools.partial(_even_post_kernel, tm=tm, n_latent=n_latent, d=d),
        grid=(b_, t_ // tm),
        in_specs=[
            tok(d),
            pl.BlockSpec((1, 1, 3 * d), lambda b, t: (b, 0, 0)),
            _const_spec((1, 1, 3 * d)),
            tok(MLA_WIDTH), tok(MLA_WIDTH), tok(LRU_WIDTH), tok(LRU_WIDTH),
            _const_spec(w_out.shape),
        ],
        out_specs=tok(d),
        out_shape=jax.ShapeDtypeStruct((b_, t_, d), F32),
        compiler_params=_cparams(("parallel", "parallel")),
        name="even_post",
    )(xc, modb, modc, o, gm, r, gl, w_out)


def _odd_post_kernel(x_ref, modb_ref, modc_ref, o_ref, g_ref, wo_ref, fg_ref, out_ref,
                     *, tm, n_latent, d, final):
    is_ctx = _row_is_ctx(tm, n_latent)
    a = o_ref[0] * g_ref[0]
    y = jnp.dot(a, wo_ref[...], preferred_element_type=F32)
    xn = _residual(x_ref, y, modb_ref, modc_ref, is_ctx, d)
    if final:
        xn = _rms(xn, fg_ref[...])
    out_ref[0] = xn


def _odd_post(xc, modb, modc, o, g, w_out, final_g, n_latent, final):
    b_, t_, d = xc.shape
    tm = FINAL_ROW_TILE if final else ROW_TILE
    n_rows = n_latent if final else t_
    tok = lambda w: pl.BlockSpec((1, tm, w), lambda b, t: (b, t, 0))
    return pl.pallas_call(
        functools.partial(_odd_post_kernel, tm=tm, n_latent=n_latent, d=d, final=final),
        grid=(b_, n_rows // tm),
        in_specs=[
            tok(d),
            pl.BlockSpec((1, 1, 3 * d), lambda b, t: (b, 0, 0)),
            _const_spec((1, 1, 3 * d)),
            tok(NA_WIDTH), tok(NA_WIDTH),
            _const_spec(w_out.shape),
            _const_spec((1, d)),
        ],
        out_specs=tok(d),
        out_shape=jax.ShapeDtypeStruct((b_, n_rows, d), F32),
        compiler_params=_cparams(("parallel", "parallel")),
        name="odd_post_final" if final else "odd_post",
    )(xc, modb, modc, o, g, w_out, final_g)


def _odd_pre_kernel(x_ref, modb_ref, modc_ref, g_ref, win_ref, q_out, k_out, v_out, g_out,
                    *, tm, n_latent, d):
    is_ctx = _row_is_ctx(tm, n_latent)
    h = _modulated_norm(x_ref[0], g_ref[...], modb_ref, modc_ref, is_ctx, d).astype(BF16)
    w = NA_WIDTH
    hd = NA_HEAD_DIM
    q_t = (jnp.dot(h, win_ref[:, 0:w], preferred_element_type=F32) * (hd ** -0.5 * LOG2E)).T
    zeros = jnp.zeros((hd, tm), BF16)
    for hh in range(NA_HEADS):
        qh = q_t[hh * hd:(hh + 1) * hd].astype(BF16)
        q_out[0, hh * LANES:(hh + 1) * LANES, :] = jnp.concatenate(
            [qh, zeros] if hh % 2 == 0 else [zeros, qh], axis=0)
    k_out[0] = jnp.dot(h, win_ref[:, w:2 * w], preferred_element_type=F32).astype(BF16)
    v_t = jnp.dot(h, win_ref[:, 2 * w:3 * w], preferred_element_type=F32).T
    ones_rows = jnp.ones((V_EXT_ROWS - LANES, NA_VCHUNK), BF16)
    for p in range(NA_HEADS // 2):
        for cc in range(tm // NA_VCHUNK):
            v_out[0, p, cc, 0:LANES, :] = v_t[p * LANES:(p + 1) * LANES,
                                              cc * NA_VCHUNK:(cc + 1) * NA_VCHUNK].astype(BF16)
            v_out[0, p, cc, LANES:V_EXT_ROWS, :] = ones_rows
    g_out[0] = _silu(jnp.dot(h, win_ref[:, 3 * w:4 * w], preferred_element_type=F32)).astype(BF16)


def _odd_pre_call(xc, modb, modc, g, w_in, n_latent):
    b_, t_, d = xc.shape
    tm = ROW_TILE
    tok = lambda w: pl.BlockSpec((1, tm, w), lambda b, t: (b, t, 0))
    return pl.pallas_call(
        functools.partial(_odd_pre_kernel, tm=tm, n_latent=n_latent, d=d),
        grid=(b_, t_ // tm),
        in_specs=[
            tok(d),
            pl.BlockSpec((1, 1, 3 * d), lambda b, t: (b, 0, 0)),
            _const_spec((1, 1, 3 * d)),
            _const_spec((1, d)),
            _const_spec(w_in.shape),
        ],
        out_specs=[
            pl.BlockSpec((1, NA_HEADS * LANES, tm), lambda b, t: (b, 0, t)),
            tok(NA_WIDTH),
            pl.BlockSpec((1, NA_HEADS // 2, tm // NA_VCHUNK, V_EXT_ROWS, NA_VCHUNK),
                         lambda b, t: (b, 0, t, 0, 0)),
            tok(NA_WIDTH),
        ],
        out_shape=[
            jax.ShapeDtypeStruct((b_, NA_HEADS * LANES, t_), BF16),
            jax.ShapeDtypeStruct((b_, t_, NA_WIDTH), BF16),
            jax.ShapeDtypeStruct((b_, NA_HEADS // 2, t_ // NA_VCHUNK, V_EXT_ROWS, NA_VCHUNK), BF16),
            jax.ShapeDtypeStruct((b_, t_, NA_WIDTH), BF16),
        ],
        compiler_params=_cparams(("parallel", "parallel")),
        name="odd_pre",
    )(xc, modb, modc, g, w_in)


def _na_kernel(q_ref, k_ref, v_ref, *rest, nq, nb, nkl, n_ctx, rows, k_ctx0, v_ctx0):
    local = nkl > 0
    n_sub = 2 * nb
    if local:
        slab_ref, idx_ref, o_ref = rest[0:3]
        scratch = rest[3:]
        kstart, v_loc0, typ = [], [], []
        for t in range(nb):
            blk = pl.program_id(2) * nb + t
            k_row0 = jnp.clip(blk * NA_QROWS - NA_WIN_R // 2, 0, rows - NA_KROWS)
            kstart.append(pl.multiple_of(k_row0 * GRID_W, NA_VCHUNK))
            v_loc0.append(k_row0 * GRID_W // NA_VCHUNK)
            typ.append(jnp.where(blk == 0, 0, jnp.where(blk == rows // NA_QROWS - 1, 2, 1)))
    else:
        o_ref = rest[0]
        scratch = rest[1:]
    s_buf, p_buf, m_ref = (scratch[r * n_sub:(r + 1) * n_sub] for r in range(3))
    accs = [None] * n_sub

    def score(u):
        t, j = divmod(u, 2)
        q_t = q_ref[0, j * LANES:(j + 1) * LANES, t * nq:(t + 1) * nq]
        s_c = jnp.dot(k_ref[0, k_ctx0:k_ctx0 + n_ctx, :], q_t, preferred_element_type=F32)
        s_buf[u][nkl:nkl + n_ctx, :] = s_c
        m = jnp.max(s_c, axis=0, keepdims=True)
        if local:
            s_buf[u][0:nkl, :] = jnp.dot(k_ref[0, pl.ds(kstart[t], nkl), :], q_t,
                                         preferred_element_type=F32)
            n_pair = nq // LANES
            parts = []
            for a in range(n_pair):
                part = jnp.full((GRID_W, LANES), -jnp.inf, F32)
                for kr in range(NA_KROWS):
                    rows_kr = slice(kr * GRID_W, (kr + 1) * GRID_W)
                    cols_a = slice(a * LANES, (a + 1) * LANES)
                    tt = s_buf[u][rows_kr, cols_a] + slab_ref[0, j, idx_ref[typ[t], kr * n_pair + a]]
                    s_buf[u][rows_kr, cols_a] = tt
                    part = jnp.maximum(part, tt)
                parts.append(jnp.max(part, axis=0, keepdims=True))
            m = jnp.maximum(m, jnp.concatenate(parts, axis=1))
        m_ref[u][...] = m

    def prob(u):
        m = jnp.broadcast_to(m_ref[u][...], (PROB_ROWS, nq))
        for r in range(0, nkl + n_ctx, PROB_ROWS):
            p_buf[u][r:r + PROB_ROWS, :] = jnp.exp2(s_buf[u][r:r + PROB_ROWS, :] - m).astype(BF16)

    def value(u):
        t = u // 2
        acc = None
        for cc in range((nkl + n_ctx) // NA_VCHUNK):
            in_window = cc < nkl // NA_VCHUNK
            v_t = (v_ref[0, 0, v_loc0[t] + cc] if in_window
                   else v_ref[0, 0, v_ctx0 + cc - nkl // NA_VCHUNK])
            pv = jnp.dot(v_t, p_buf[u][cc * NA_VCHUNK:(cc + 1) * NA_VCHUNK, :],
                         preferred_element_type=F32)
            acc = pv if acc is None else acc + pv
        accs[u] = acc

    for g in range(n_sub + 3):
        for stage, lag in ((value, 3), (prob, 2), (score, 0)):
            if 0 <= g - lag < n_sub:
                stage(g - lag)
    for t in range(nb):
        a0, a1 = accs[2 * t], accs[2 * t + 1]
        o0 = a0[0:NA_HEAD_DIM] / a0[LANES:LANES + 1]
        o1 = a1[NA_HEAD_DIM:LANES] / a1[LANES:LANES + 1]
        o_ref[0, t * nq:(t + 1) * nq, :] = jnp.concatenate([o0, o1], axis=0).T.astype(BF16)


def _na_bias_tables(rpb, rows):
    n_heads = rpb.shape[0]
    pad = GRID_W - NA_WIN_C
    rp = jnp.pad(rpb * LOG2E, ((0, 0), (0, 0), (pad, pad)))
    toep = jnp.stack([rp[:, :, GRID_W - 1 - qc:2 * GRID_W - 1 - qc] for qc in range(GRID_W)], axis=3)
    col = np.arange(GRID_W)
    cs = np.clip(col - NA_WIN_C // 2, 0, GRID_W - NA_WIN_C)
    valid_c = (col[:, None] >= cs[None, :]) & (col[:, None] < cs[None, :] + NA_WIN_C)
    toep = jnp.where(valid_c, toep, NEG_BIG)
    n_dr = 2 * NA_WIN_R - 1
    slabs = jnp.concatenate([toep, jnp.full((n_heads, 1, GRID_W, GRID_W), NEG_BIG, F32)], axis=1)
    idx = np.full((3, NA_KROWS, NA_QROWS), n_dr, np.int32)
    for typ, r0 in enumerate((0, 2 * NA_QROWS, rows - NA_QROWS)):
        ks = int(np.clip(r0 - NA_WIN_R // 2, 0, rows - NA_KROWS))
        for kr in range(NA_KROWS):
            for qr in range(NA_QROWS):
                r = r0 + qr
                rs = int(np.clip(r - NA_WIN_R // 2, 0, rows - NA_WIN_R))
                if rs <= ks + kr < rs + NA_WIN_R:
                    idx[typ, kr, qr] = ks + kr - r + NA_WIN_R - 1
    pairs = idx.reshape(3, NA_KROWS, NA_QROWS // 2, 2)
    uniq, inverse = np.unique(pairs.reshape(-1, 2), axis=0, return_inverse=True)
    table = jnp.concatenate([jnp.take(slabs, jnp.asarray(uniq[:, 0]), axis=1),
                             jnp.take(slabs, jnp.asarray(uniq[:, 1]), axis=1)], axis=-1)
    table = table.reshape(n_heads // 2, 2, uniq.shape[0], GRID_W, 2 * GRID_W)
    return table, jnp.asarray(inverse.reshape(3, NA_KROWS * (NA_QROWS // 2)).astype(np.int32))


def _na_scratch(n_keys, nq, nb):
    return ([pltpu.VMEM((n_keys, nq), F32)] * (2 * nb) + [pltpu.VMEM((n_keys, nq), BF16)] * (2 * nb)
            + [pltpu.VMEM((1, nq), F32)] * (2 * nb))


def _na_attention(q_t, k, v_t, bias, n_latent, need_ctx):
    b_, t_, _ = k.shape
    n_ctx = t_ - n_latent
    rows = n_latent // GRID_W
    nq = NA_QROWS * GRID_W
    nkl = NA_KROWS * GRID_W
    n_blocks = rows // NA_QROWS
    n_pairs = NA_HEADS // 2
    n_vc = t_ // NA_VCHUNK
    nb = NA_BLOCKS_PER_STEP
    assert n_blocks % nb == 0
    slab_table, slab_idx = bias

    o_lat = pl.pallas_call(
        functools.partial(_na_kernel, nq=nq, nb=nb, nkl=nkl, n_ctx=n_ctx, rows=rows, k_ctx0=n_latent,
                          v_ctx0=n_latent // NA_VCHUNK),
        grid=(n_pairs, b_, n_blocks // nb),
        in_specs=[
            pl.BlockSpec((1, 2 * LANES, nb * nq), lambda p, b, i: (b, p, i)),
            pl.BlockSpec((1, t_, LANES), lambda p, b, i: (b, 0, p)),
            pl.BlockSpec((1, 1, n_vc, V_EXT_ROWS, NA_VCHUNK), lambda p, b, i: (b, p, 0, 0, 0)),
            pl.BlockSpec((1,) + slab_table.shape[1:], lambda p, b, i: (p, 0, 0, 0, 0)),
            pl.BlockSpec(memory_space=pltpu.SMEM),
        ],
        out_specs=pl.BlockSpec((1, nb * nq, LANES), lambda p, b, i: (b, i, p)),
        out_shape=jax.ShapeDtypeStruct((b_, n_latent, NA_WIDTH), BF16),
        scratch_shapes=_na_scratch(nkl + n_ctx, nq, nb),
        compiler_params=_cparams(("parallel", "parallel", "arbitrary")),
        name="na_attn",
    )(q_t, k, v_t, slab_table, slab_idx)
    if not need_ctx:
        return o_lat
    cb = n_latent // n_ctx
    o_ctx = pl.pallas_call(
        functools.partial(_na_kernel, nq=n_ctx, nb=1, nkl=0, n_ctx=n_ctx, rows=rows, k_ctx0=0, v_ctx0=0),
        grid=(n_pairs, b_),
        in_specs=[
            pl.BlockSpec((1, 2 * LANES, n_ctx), lambda p, b: (b, p, cb)),
            pl.BlockSpec((1, n_ctx, LANES), lambda p, b: (b, cb, p)),
            pl.BlockSpec((1, 1, n_ctx // NA_VCHUNK, V_EXT_ROWS, NA_VCHUNK), lambda p, b: (b, p, cb, 0, 0)),
        ],
        out_specs=pl.BlockSpec((1, n_ctx, LANES), lambda p, b: (b, 0, p)),
        out_shape=jax.ShapeDtypeStruct((b_, n_ctx, NA_WIDTH), BF16),
        scratch_shapes=_na_scratch(n_ctx, n_ctx, 1),
        compiler_params=_cparams(("parallel", "parallel")),
        name="na_attn_ctx",
    )(q_t, k, v_t)
    return jnp.concatenate([o_lat, o_ctx], axis=1)


def _rope_swap_index():
    half = MLA_ROPE // 4
    idx = np.arange(MLA_ROPE)
    return np.where((idx // half) % 2 == 0, idx + half, idx - half)


def _rope_tables(n_latent, n_ctx):
    n_freq = MLA_ROPE // 4
    inv = ROPE_THETA ** (-jnp.arange(n_freq, dtype=F32) / n_freq)
    t = jnp.arange(n_latent, dtype=jnp.int32)
    ang_r = (t // GRID_W).astype(F32)[:, None] * inv
    ang_c = (t % GRID_W).astype(F32)[:, None] * inv
    cr, sr, cc, sc = jnp.cos(ang_r), jnp.sin(ang_r), jnp.cos(ang_c), jnp.sin(ang_c)
    cos32 = jnp.concatenate([cr, cr, cc, cc], axis=-1)
    sin32 = jnp.concatenate([-sr, sr, -sc, sc], axis=-1)
    ones = jnp.ones((n_latent, MLA_NOPE), F32)
    zeros = jnp.zeros((n_latent, LANES - MLA_NOPE - MLA_ROPE), F32)
    cos_l = jnp.concatenate([ones, cos32, zeros], axis=-1)
    sin_l = jnp.concatenate([0 * ones, sin32, zeros], axis=-1)
    cos_c = jnp.concatenate([jnp.ones((n_ctx, MLA_NOPE + MLA_ROPE), F32),
                             jnp.zeros((n_ctx, LANES - MLA_NOPE - MLA_ROPE), F32)], axis=-1)
    sin_c = jnp.zeros((n_ctx, LANES), F32)
    return jnp.concatenate([cos_l, cos_c], axis=0), jnp.concatenate([sin_l, sin_c], axis=0)


def _even_weights(i, ev_w_in, mla_q_norm, mla_w_uq, mla_kv_norm, mla_w_ukv, lru_conv_w, lru_conv_b,
                  lru_wa, lru_ba, lru_wx, lru_bx, lru_lambda, ev_w_out):
    sw = _rope_swap_index()
    d = ev_w_in.shape[1]
    w_in = ev_w_in[i]
    o_cq, o_ckv, o_kr = MLA_Q_RANK, MLA_Q_RANK + MLA_KV_RANK, MLA_Q_RANK + MLA_KV_RANK + MLA_ROPE
    kr = w_in[:, o_ckv:o_kr]
    pad_lo = jnp.zeros((d, MLA_NOPE), F32)
    pad_hi = jnp.zeros((d, LANES - MLA_NOPE - MLA_ROPE), F32)
    w_in_ext = jnp.concatenate(
        [w_in[:, :o_ckv], pad_lo, kr, pad_hi, pad_lo, kr[:, sw], pad_hi, w_in[:, o_kr:]], axis=1)
    wq = mla_w_uq[i].reshape(MLA_Q_RANK, MLA_HEADS, MLA_NOPE + MLA_ROPE)
    zq = jnp.zeros((MLA_Q_RANK, MLA_HEADS, LANES - MLA_NOPE - MLA_ROPE), F32)
    wqa = jnp.concatenate([wq, zq], axis=-1).reshape(MLA_Q_RANK, MLA_HEADS * LANES)
    wqb = jnp.concatenate([jnp.zeros((MLA_Q_RANK, MLA_HEADS, MLA_NOPE), F32),
                           wq[:, :, MLA_NOPE:][:, :, sw], zq], axis=-1).reshape(MLA_Q_RANK, MLA_HEADS * LANES)
    wkv = mla_w_ukv[i].reshape(MLA_KV_RANK, MLA_HEADS, MLA_NOPE + MLA_V)
    wk = jnp.concatenate([wkv[:, :, :MLA_NOPE], jnp.zeros((MLA_KV_RANK, MLA_HEADS, LANES - MLA_NOPE), F32)],
                         axis=-1).reshape(MLA_KV_RANK, MLA_HEADS * LANES)
    wv = wkv[:, :, MLA_NOPE:].reshape(MLA_KV_RANK, MLA_WIDTH)
    ng = LRU_WIDTH // LANES
    per = LANES // LRU_BLOCK

    def blockdiag(w):
        w = w.reshape(2, ng, per, LRU_BLOCK, LRU_BLOCK)
        eye = jnp.eye(per, dtype=F32)
        return jnp.einsum('dgpkj,pq->dgpkqj', w, eye).reshape(2, ng, LANES, LANES)

    w_gate = jnp.concatenate([blockdiag(lru_wa[i]), blockdiag(lru_wx[i])], axis=-1).astype(BF16)
    b_gate = jnp.concatenate([lru_ba[i].reshape(2, ng, 1, LANES), lru_bx[i].reshape(2, ng, 1, LANES)], axis=-1)
    return dict(
        w_in=w_in_ext.astype(BF16), q_norm=mla_q_norm[i][None], wqa=wqa.astype(BF16), wqb=wqb.astype(BF16),
        kv_norm=mla_kv_norm[i][None], wk=wk.astype(BF16), wv=wv.astype(BF16),
        conv_w=lru_conv_w[i], conv_b=lru_conv_b[i][None], w_gate=w_gate, b_gate=b_gate,
        softplus=jax.nn.softplus(-lru_lambda[i]).reshape(2, ng, 1, LANES),
        w_out=ev_w_out[i].astype(BF16))


def kernel(x, c, ctx, c_ctx, ada_w, ada_b, norm_g, ev_w_in, mla_q_norm, mla_w_uq, mla_kv_norm, mla_w_ukv,
           lru_conv_w, lru_conv_b, lru_wa, lru_ba, lru_wx, lru_bx, lru_lambda, ev_w_out, od_w_in, na_rpb,
           od_w_out, final_norm_g):
    b_, s_, d = x.shape
    n_ctx = ctx.shape[1]
    depth = ada_w.shape[0]
    rows = s_ // GRID_W
    assert b_ <= 7 and s_ % GRID_W == 0 and rows % NA_QROWS == 0 and rows >= NA_KROWS
    assert (s_ + n_ctx) % ROW_TILE == 0 and s_ % FINAL_ROW_TILE == 0 and s_ % MLA_TQ == 0
    assert ROW_TILE == MLA_TK and n_ctx <= MLA_TK and s_ % n_ctx == 0 and n_ctx % LRU_CHUNK == 0 and s_ % LRU_CHUNK == 0
    assert n_ctx % NA_VCHUNK == 0 and ROW_TILE % NA_VCHUNK == 0 and rows >= 3 * NA_QROWS
    assert (NA_QROWS // 2 * GRID_W) % NA_VCHUNK == 0 and depth % 2 == 0

    cond = jnp.zeros((8, d), F32).at[:b_].set(c).at[b_].set(c_ctx)
    mod = _adaln(cond, ada_w, ada_b)
    xc = jnp.concatenate([x, ctx], axis=1)
    cos_t, sin_t = _rope_tables(s_, n_ctx)

    for layer in range(depth):
        need_ctx = layer < depth - 1
        i = layer // 2
        modb = mod[layer, :b_][:, None, :]
        modc = mod[layer, b_][None, None, :]
        g = norm_g[layer][None]
        if layer % 2 == 0:
            wts = _even_weights(i, ev_w_in, mla_q_norm, mla_w_uq, mla_kv_norm, mla_w_ukv, lru_conv_w,
                                lru_conv_b, lru_wa, lru_ba, lru_wx, lru_bx, lru_lambda, ev_w_out)
            q_t, kc, v_t, gm, xl, gl = _even_pre(xc, modb, modc, g, wts, cos_t, sin_t, s_)
            o = _mla_attention(q_t, kc, v_t, s_)
            r = _lru(xl, wts, s_)
            xc = _even_post(xc, modb, modc, o, gm, r, gl, wts["w_out"], s_)
        else:
            q, k, v, gg = _odd_pre_call(xc, modb, modc, g, od_w_in[i].astype(BF16), s_)
            bias = _na_bias_tables(na_rpb[i], rows)
            o = _na_attention(q, k, v, bias, s_, need_ctx)
            xc = _odd_post(xc, modb, modc, o, gg, od_w_out[i].astype(BF16), final_norm_g[None], s_,
                           final=not need_ctx)
    return xc
```
